```python
import math
import jax
import jax.numpy as jnp
from jax import lax
import numpy as np

D_MODEL = 2048
BATCH = 2
SEQ = 16384
DEPTH = 4

N_MIXERS = 2
N_LRU_LAYERS = (DEPTH + 1) // 2
N_NSA_LAYERS = DEPTH // 2

LRU_WIDTH = D_MODEL
LRU_BLOCK = 256
LRU_NBLK = LRU_WIDTH // LRU_BLOCK
CONV_W = 4
LRU_C = 8.0

HEAD_DIM = 128
N_HEADS = D_MODEL // HEAD_DIM
N_KV = 4
GROUP = N_HEADS // N_KV
CMP_LEN = 32
CMP_STRIDE = 16
SEL_LEN = 64
SEL_TOP = 16
WINDOW = 512
Q_BLOCK = 128
NSA_IN = N_HEADS * HEAD_DIM + 6 * N_KV * HEAD_DIM + 3 * N_HEADS

NUM_BUCKETS = 32
MAX_DISTANCE = 1024

N_EXPERTS = 16
N_GROUPS = 4
EXPERTS_PER_GROUP = N_EXPERTS // N_GROUPS
TOP_K = 2
D_EXPERT = D_MODEL // 2
EXPERT_BLOCK = 256

ALPHA = (2 * DEPTH) ** 0.25
BETA = (8 * DEPTH) ** -0.25
LN_EPS = 1e-5
NEG = -1e30
BIG = 1e9

kernel_name = 'hybrid_rglru_nsa_moe_deepnorm'


def layer_norm(x, g, b):
    xf = x.astype(jnp.float32)
    mu = xf.mean(-1, keepdims=True)
    var = jnp.square(xf - mu).mean(-1, keepdims=True)
    return ((xf - mu) * lax.rsqrt(var + LN_EPS) * g + b).astype(x.dtype)


def t5_bucket(dist):
    n = jnp.maximum(dist, 0)
    max_exact = NUM_BUCKETS // 2
    nf = jnp.maximum(n, 1).astype(jnp.float32)
    large = max_exact + (jnp.log(nf / max_exact) / math.log(MAX_DISTANCE / max_exact)
                         * (NUM_BUCKETS - max_exact)).astype(jnp.int32)
    large = jnp.minimum(large, NUM_BUCKETS - 1)
    return jnp.where(n < max_exact, n, large)


def masked_softmax(s, mask):
    s = jnp.where(mask, s.astype(jnp.float32), NEG)
    m = s.max(-1, keepdims=True)
    p = jnp.exp(s - m) * mask
    return p / jnp.maximum(p.sum(-1, keepdims=True), 1e-30)


def rglru_mixer(x, w_in, conv_w, conv_b, w_a, b_a, w_i, b_i, lam, w_out):
    B, S, _ = x.shape
    xin = x @ w_in
    gate = jax.nn.gelu(xin[..., :LRU_WIDTH])
    y = xin[..., LRU_WIDTH:]
    yp = jnp.pad(y, ((0, 0), (CONV_W - 1, 0), (0, 0)))
    yc = conv_b + yp[:, 0:S] * conv_w[0]
    for k in range(1, CONV_W):
        yc = yc + yp[:, k:k + S] * conv_w[k]
    yb = yc.reshape(B, S, LRU_NBLK, LRU_BLOCK)
    r = jax.nn.sigmoid(jnp.einsum('bsnd,nde->bsne', yb, w_a).reshape(B, S, LRU_WIDTH) + b_a)
    i = jax.nn.sigmoid(jnp.einsum('bsnd,nde->bsne', yb, w_i).reshape(B, S, LRU_WIDTH) + b_i)
    log_a = -LRU_C * r.astype(jnp.float32) * jax.nn.softplus(-lam.astype(jnp.float32))
    a = jnp.exp(log_a)
    u = jnp.sqrt(-jnp.expm1(2.0 * log_a)) * (i * yc).astype(jnp.float32)

    def combine(lhs, rhs):
        return (lhs[0] * rhs[0], rhs[0] * lhs[1] + rhs[1])

    _, h = lax.associative_scan(combine, (a, u), axis=1)
    return (h.astype(x.dtype) * gate) @ w_out


def compress(t, pe, w1, w2):
    B, S = t.shape[:2]
    ch = t.reshape(B, S // CMP_STRIDE, CMP_STRIDE, N_KV, HEAD_DIM)
    blocks = jnp.concatenate([ch[:, :-1], ch[:, 1:]], axis=2) + pe[:, None, :]
    n_cmp = blocks.shape[1]
    flat = blocks.transpose(0, 1, 3, 2, 4).reshape(B, n_cmp, N_KV, CMP_LEN * HEAD_DIM)
    return jax.nn.gelu(flat @ w1) @ w2


def nsa_mixer(x, rel_bias, w_in, pe_k, w1_k, w2_k, pe_v, w1_v, w2_v, w_out):
    B, S, _ = x.shape
    qd = N_HEADS * HEAD_DIM
    kd = N_KV * HEAD_DIM
    proj = x @ w_in
    q = proj[..., :qd].reshape(B, S, N_KV, GROUP, HEAD_DIM) * (HEAD_DIM ** -0.5)
    kv = proj[..., qd:qd + 6 * kd].reshape(B, S, 6, N_KV, HEAD_DIM)
    k_cmp, v_cmp, k_sel, v_sel, k_win, v_win = (kv[:, :, j] for j in range(6))
    gates = jax.nn.sigmoid(proj[..., qd + 6 * kd:].reshape(B, S, N_KV, GROUP, 3))

    kc = compress(k_cmp, pe_k, w1_k, w2_k)
    vc = compress(v_cmp, pe_v, w1_v, w2_v)
    n_cmp = kc.shape[1]
    c_end = jnp.arange(n_cmp) * CMP_STRIDE + CMP_LEN - 1

    n_sel = S // SEL_LEN
    n_top = min(SEL_TOP, n_sel)
    cj = jnp.arange(n_cmp) * CMP_STRIDE
    sb = jnp.arange(n_sel) * SEL_LEN
    overlap = jnp.maximum(jnp.minimum(cj[:, None] + CMP_LEN, sb[None, :] + SEL_LEN)
                          - jnp.maximum(cj[:, None], sb[None, :]), 0).astype(jnp.float32) / CMP_LEN
    ks_blk = k_sel.reshape(B, n_sel, SEL_LEN, N_KV, HEAD_DIM).transpose(0, 3, 1, 2, 4)
    vs_blk = v_sel.reshape(B, n_sel, SEL_LEN, N_KV, HEAD_DIM).transpose(0, 3, 1, 2, 4)
    kwp = jnp.pad(k_win, ((0, 0), (WINDOW, 0), (0, 0), (0, 0)))
    vwp = jnp.pad(v_win, ((0, 0), (WINDOW, 0), (0, 0), (0, 0)))
    tbl_hg = rel_bias.reshape(NUM_BUCKETS, N_KV, GROUP)
    bi = jnp.arange(B)[:, None, None, None]
    hi = jnp.arange(N_KV)[None, :, None, None]
    blk = jnp.arange(n_sel)

    def query_block(ib):
        q0 = ib * Q_BLOCK
        t = q0 + jnp.arange(Q_BLOCK)
        qb = lax.dynamic_slice_in_dim(q, q0, Q_BLOCK, 1)
        gb = lax.dynamic_slice_in_dim(gates, q0, Q_BLOCK, 1)

        dc = t[:, None] - c_end[None, :]
        bias_c = rel_bias[t5_bucket(dc)].reshape(Q_BLOCK, n_cmp, N_KV, GROUP).transpose(2, 3, 0, 1)
        s_c = jnp.einsum('bqhgd,bchd->bhgqc', qb, kc) + bias_c
        p_c = masked_softmax(s_c, dc >= 0)
        o_c = jnp.einsum('bhgqc,bchd->bqhgd', p_c.astype(vc.dtype), vc)

        imp = p_c.sum(2) @ overlap
        cur = t // SEL_LEN
        valid = blk[None, :] * SEL_LEN <= t[:, None]
        forced = valid & ((blk[None, :] == 0) | (blk[None, :] == cur[:, None])
                          | (blk[None, :] == cur[:, None] - 1))
        score = jnp.where(forced, BIG, jnp.where(valid, imp, NEG))
        top_v, top_i = lax.top_k(score, n_top)
        pos = (top_i[..., None] * SEL_LEN + jnp.arange(SEL_LEN)).reshape(B, N_KV, Q_BLOCK, n_top * SEL_LEN)
        ok = jnp.repeat(top_v > NEG * 0.5, SEL_LEN, axis=-1) & (pos <= t[:, None])
        ksel = ks_blk[bi, hi, top_i].reshape(B, N_KV, Q_BLOCK, n_top * SEL_LEN, HEAD_DIM)
        vsel = vs_blk[bi, hi, top_i].reshape(B, N_KV, Q_BLOCK, n_top * SEL_LEN, HEAD_DIM)
        bias_s = jnp.moveaxis(tbl_hg[t5_bucket(t[:, None] - pos), hi], -1, 2)
        s_s = jnp.einsum('bqhgd,bhqkd->bhgqk', qb, ksel) + bias_s
        p_s = masked_softmax(s_s, ok[:, :, None])
        o_s = jnp.einsum('bhgqk,bhqkd->bqhgd', p_s.astype(vsel.dtype), vsel)

        kw = lax.dynamic_slice_in_dim(kwp, q0, Q_BLOCK + WINDOW, 1)
        vw = lax.dynamic_slice_in_dim(vwp, q0, Q_BLOCK + WINDOW, 1)
        s_pos = q0 - WINDOW + jnp.arange(Q_BLOCK + WINDOW)
        dw = t[:, None] - s_pos[None, :]
        mask_w = (dw >= 0) & (dw < WINDOW) & (s_pos[None, :] >= 0)
        bias_w = rel_bias[t5_bucket(dw)].reshape(Q_BLOCK, Q_BLOCK + WINDOW, N_KV, GROUP).transpose(2, 3, 0, 1)
        s_w = jnp.einsum('bqhgd,bkhd->bhgqk', qb, kw) + bias_w
        p_w = masked_softmax(s_w, mask_w)
        o_w = jnp.einsum('bhgqk,bkhd->bqhgd', p_w.astype(vw.dtype), vw)

        o = gb[..., 0:1] * o_c + gb[..., 1:2] * o_s + gb[..., 2:3] * o_w
        return o.reshape(B, Q_BLOCK, qd)

    out = lax.map(query_block, jnp.arange(S // Q_BLOCK))
    out = out.transpose(1, 0, 2, 3).reshape(B, S, qd)
    return out @ w_out


def moe_ffn(x, router_w, router_b, w_gate, w_up, w_down):
    B, S, D = x.shape
    T = B * S
    xf = x.reshape(T, D)
    probs = jax.nn.softmax((xf @ router_w + router_b).astype(jnp.float32), axis=-1)
    pg = probs.reshape(T, N_GROUPS, EXPERTS_PER_GROUP)
    g_score = lax.top_k(pg, TOP_K)[0].sum(-1)
    g_sel = jnp.argmax(g_score, axis=-1)
    top_v, top_i = lax.top_k(pg[jnp.arange(T), g_sel], TOP_K)
    expert = (g_sel[:, None] * EXPERTS_PER_GROUP + top_i).reshape(-1)
    weight = (top_v / top_v.sum(-1, keepdims=True)).reshape(-1)
    token = jnp.repeat(jnp.arange(T), TOP_K)

    n_assign = T * TOP_K
    n_blocks = -(-n_assign // EXPERT_BLOCK) + N_EXPERTS
    n_rows = n_blocks * EXPERT_BLOCK
    order = jnp.argsort(expert)
    e_s, tok_s, w_s = expert[order], token[order], weight[order]
    counts = jnp.bincount(expert, length=N_EXPERTS)
    padded = (counts + EXPERT_BLOCK - 1) // EXPERT_BLOCK * EXPERT_BLOCK
    start = jnp.cumsum(counts) - counts
    p_end = jnp.cumsum(padded)
    dest = (p_end - padded)[e_s] + jnp.arange(n_assign) - start[e_s]
    row_tok = jnp.zeros((n_rows,), jnp.int32).at[dest].set(tok_s)
    row_w = jnp.zeros((n_rows,), jnp.float32).at[dest].set(w_s)
    blk_exp = jnp.minimum(jnp.searchsorted(p_end, jnp.arange(n_blocks) * EXPERT_BLOCK, side='right'),
                          N_EXPERTS - 1)

    def expert_rows(args):
        tok, w, e = args
        xb = xf[tok]
        h = jax.nn.silu(xb @ w_gate[e]) * (xb @ w_up[e])
        return (h @ w_down[e]) * w[:, None].astype(x.dtype)

    y = lax.map(expert_rows, (row_tok.reshape(n_blocks, EXPERT_BLOCK),
                              row_w.reshape(n_blocks, EXPERT_BLOCK), blk_exp))
    out = jnp.zeros((T, D), x.dtype).at[row_tok].add(y.reshape(n_rows, D))
    return out.reshape(B, S, D)


def setup_inputs(seed: int = 0) -> dict:
    key = jax.random.key(seed)
    keys = iter(jax.random.split(key, 40))

    def nrm(shape, scale):
        return jax.random.normal(next(keys), shape, jnp.float32) * scale

    NA, NB = N_LRU_LAYERS, N_NSA_LAYERS
    u = jax.random.uniform(next(keys), (NA, LRU_WIDTH), jnp.float32, minval=0.9, maxval=0.999)
    a0 = u ** (1.0 / LRU_C)
    return {
        'x': nrm((BATCH, SEQ, D_MODEL), 1.0),
        'rel_bias': nrm((NUM_BUCKETS, N_HEADS), 0.1),
        'router_w': nrm((D_MODEL, N_EXPERTS), D_MODEL ** -0.5),
        'router_b': nrm((N_EXPERTS,), 0.01),
        'ln1_g': 1.0 + nrm((DEPTH, D_MODEL), 0.02),
        'ln1_b': nrm((DEPTH, D_MODEL), 0.02),
        'ln2_g': 1.0 + nrm((DEPTH, D_MODEL), 0.02),
        'ln2_b': nrm((DEPTH, D_MODEL), 0.02),
        'lru_w_in': nrm((NA, D_MODEL, 2 * LRU_WIDTH), D_MODEL ** -0.5),
        'lru_conv_w': nrm((NA, CONV_W, LRU_WIDTH), CONV_W ** -0.5),
        'lru_conv_b': nrm((NA, LRU_WIDTH), 0.02),
        'lru_w_a': nrm((NA, LRU_NBLK, LRU_BLOCK, LRU_BLOCK), LRU_BLOCK ** -0.5),
        'lru_b_a': nrm((NA, LRU_WIDTH), 0.02),
        'lru_w_i': nrm((NA, LRU_NBLK, LRU_BLOCK, LRU_BLOCK), LRU_BLOCK ** -0.5),
        'lru_b_i': nrm((NA, LRU_WIDTH), 0.02),
        'lru_lambda': jnp.log(a0) - jnp.log1p(-a0),
        'lru_w_out': nrm((NA, LRU_WIDTH, D_MODEL), LRU_WIDTH ** -0.5 * BETA),
        'nsa_w_in': nrm((NB, D_MODEL, NSA_IN), D_MODEL ** -0.5),
        'nsa_pe_k': nrm((NB, CMP_LEN, HEAD_DIM), 0.02),
        'nsa_w1_k': nrm((NB, CMP_LEN * HEAD_DIM, HEAD_DIM), (CMP_LEN * HEAD_DIM) ** -0.5),
        'nsa_w2_k': nrm((NB, HEAD_DIM, HEAD_DIM), HEAD_DIM ** -0.5),
        'nsa_pe_v': nrm((NB, CMP_LEN, HEAD_DIM), 0.02),
        'nsa_w1_v': nrm((NB, CMP_LEN * HEAD_DIM, HEAD_DIM), (CMP_LEN * HEAD_DIM) ** -0.5),
        'nsa_w2_v': nrm((NB, HEAD_DIM, HEAD_DIM), HEAD_DIM ** -0.5),
        'nsa_w_out': nrm((NB, N_HEADS * HEAD_DIM, D_MODEL), (N_HEADS * HEAD_DIM) ** -0.5 * BETA),
        'moe_w_gate': nrm((DEPTH, N_EXPERTS, D_MODEL, D_EXPERT), D_MODEL ** -0.5),
        'moe_w_up': nrm((DEPTH, N_EXPERTS, D_MODEL, D_EXPERT), D_MODEL ** -0.5),
        'moe_w_down': nrm((DEPTH, N_EXPERTS, D_EXPERT, D_MODEL), D_EXPERT ** -0.5 * BETA),
    }


def reference(x, rel_bias, router_w, router_b, ln1_g, ln1_b, ln2_g, ln2_b,
              lru_w_in, lru_conv_w, lru_conv_b, lru_w_a, lru_b_a, lru_w_i, lru_b_i,
              lru_lambda, lru_w_out, nsa_w_in, nsa_pe_k, nsa_w1_k, nsa_w2_k,
              nsa_pe_v, nsa_w1_v, nsa_w2_v, nsa_w_out, moe_w_gate, moe_w_up, moe_w_down):
    for layer in range(DEPTH):
        j = layer // N_MIXERS
        if layer % N_MIXERS == 0:
            mix = rglru_mixer(x, lru_w_in[j], lru_conv_w[j], lru_conv_b[j], lru_w_a[j], lru_b_a[j],
                              lru_w_i[j], lru_b_i[j], lru_lambda[j], lru_w_out[j])
        else:
            mix = nsa_mixer(x, rel_bias, nsa_w_in[j], nsa_pe_k[j], nsa_w1_k[j], nsa_w2_k[j],
                            nsa_pe_v[j], nsa_w1_v[j], nsa_w2_v[j], nsa_w_out[j])
        x = layer_norm(ALPHA * x + mix, ln1_g[layer], ln1_b[layer])
        ffn = moe_ffn(x, router_w, router_b, moe_w_gate[layer], moe_w_up[layer], moe_w_down[layer])
        x = layer_norm(ALPHA * x + ffn, ln2_g[layer], ln2_b[layer])
    return x
```

```python
import functools
import math

import numpy as np
import jax
import jax.numpy as jnp
from jax import lax
from jax.experimental import pallas as pl
from jax.experimental.pallas import tpu as pltpu

F32 = jnp.float32
BF16 = jnp.bfloat16

D_MODEL = 2048
DEPTH = 4
LRU_BLOCK = 256
LRU_NBLK = D_MODEL // LRU_BLOCK
CONV_W = 4
LRU_C = 8.0
HEAD_DIM = 128
N_HEADS = 16
N_KV = 4
GROUP = 4
CMP_LEN = 32
CMP_STRIDE = 16
SEL_LEN = 64
SEL_TOP = 16
WINDOW = 512
Q_BLOCK = 128
NUM_BUCKETS = 32
MAX_DISTANCE = 1024
N_EXPERTS = 16
N_GROUPS = 4
EXPERTS_PER_GROUP = 4
D_EXPERT = 1024
ALPHA = (2 * DEPTH) ** 0.25
LN_EPS = 1e-5
NEG = -1e30
BIG = 1e9

LANE = 128
QG = GROUP * Q_BLOCK
EXPERT_ROWS = 256
SEL_CHUNK = 256
SEL_NEAR = 5
CMP_TILE = 128
CMP_NEAR = 23
VMEM_LIMIT = 56 << 20


def _cparams(sem, vmem=VMEM_LIMIT):
    return pltpu.CompilerParams(dimension_semantics=sem, vmem_limit_bytes=vmem)


def _act(x, act):
    if act == "gelu":
        return jax.nn.gelu(x)
    if act == "sigmoid":
        return jax.nn.sigmoid(x)
    return x


def _proj_nn_kernel(x_ref, w_ref, o_ref, *, act, head_major):
    acc = jnp.dot(x_ref[...], w_ref[...], preferred_element_type=F32)
    acc = _act(acc, act)
    if head_major:
        for h in range(o_ref.shape[0]):
            o_ref[h] = acc[:, h * LANE:(h + 1) * LANE].astype(o_ref.dtype)
    else:
        o_ref[...] = acc.astype(o_ref.dtype)


def _proj_nn(x, w, *, out_dtype, act=None, head_major=False, tm=1024, tn=1024):
    T, K = x.shape
    N = w.shape[1]
    tm, tn = min(tm, T), min(tn, N)
    if head_major:
        out_shape = (N // LANE, T, LANE)
        out_spec = pl.BlockSpec((tn // LANE, tm, LANE), lambda j, i: (j, i, 0))
    else:
        out_shape = (T, N)
        out_spec = pl.BlockSpec((tm, tn), lambda j, i: (i, j))
    return pl.pallas_call(
        functools.partial(_proj_nn_kernel, act=act, head_major=head_major),
        grid=(N // tn, T // tm),
        in_specs=[pl.BlockSpec((tm, K), lambda j, i: (i, 0)),
                  pl.BlockSpec((K, tn), lambda j, i: (0, j))],
        out_specs=out_spec,
        out_shape=jax.ShapeDtypeStruct(out_shape, out_dtype),
        compiler_params=_cparams(("parallel", "parallel")),
        name="proj_nn",
    )(x, w)


def _proj_nt_kernel(wt_ref, x_ref, o_ref, *, act, scale, n_scaled):
    acc = lax.dot_general(wt_ref[...], x_ref[...], (((1,), (1,)), ((), ())),
                          preferred_element_type=F32)
    if n_scaled:
        acc = acc * jnp.where(pl.program_id(0) < n_scaled, scale, 1.0).astype(F32)
    o_ref[...] = _act(acc, act).astype(o_ref.dtype)


def _proj_nt(wt, x, *, out_dtype, act=None, scale=1.0, n_scaled=0, tm=1024, tn=512):
    N, K = wt.shape
    T = x.shape[0]
    tm, tn = min(tm, T), min(tn, N)
    return pl.pallas_call(
        functools.partial(_proj_nt_kernel, act=act, scale=scale, n_scaled=n_scaled),
        grid=(N // tn, T // tm),
        in_specs=[pl.BlockSpec((tn, K), lambda j, i: (j, 0)),
                  pl.BlockSpec((tm, K), lambda j, i: (i, 0))],
        out_specs=pl.BlockSpec((tn, tm), lambda j, i: (j, i)),
        out_shape=jax.ShapeDtypeStruct((N, T), out_dtype),
        compiler_params=_cparams(("parallel", "parallel")),
        name="proj_nt",
    )(wt, x)


def _layer_norm_rows(z, g, b):
    mu = jnp.mean(z, axis=-1, keepdims=True)
    zc = z - mu
    var = jnp.mean(zc * zc, axis=-1, keepdims=True)
    return zc * lax.rsqrt(var + LN_EPS) * g + b


def _proj_ln_kernel(a_ref, w_ref, res_ref, g_ref, b_ref, of_ref, ob_ref):
    acc = jnp.dot(a_ref[...], w_ref[...], preferred_element_type=F32)
    y = _layer_norm_rows(ALPHA * res_ref[...] + acc, g_ref[...], b_ref[...])
    of_ref[...] = y
    ob_ref[...] = y.astype(BF16)


def _proj_ln(a, w, res, g, b, *, tm=256):
    T, K = a.shape
    D = w.shape[1]
    tm = min(tm, T)
    row = pl.BlockSpec((tm, D), lambda i: (i, 0))
    vec = pl.BlockSpec((1, D), lambda i: (0, 0))
    return pl.pallas_call(
        _proj_ln_kernel,
        grid=(T // tm,),
        in_specs=[pl.BlockSpec((tm, K), lambda i: (i, 0)),
                  pl.BlockSpec((K, D), lambda i: (0, 0)),
                  row, vec, vec],
        out_specs=[row, row],
        out_shape=[jax.ShapeDtypeStruct((T, D), F32), jax.ShapeDtypeStruct((T, D), BF16)],
        compiler_params=_cparams(("parallel",)),
        name="proj_ln",
    )(a, w, res, g.reshape(1, D), b.reshape(1, D))


def _lru_kernel(y_ref, gate_ref, cw_ref, cb_ref, wa_ref, ba_ref, wi_ref, bi_ref, lam_ref,
                o_ref, h_ref, tail_ref, *, rows):
    @pl.when(pl.program_id(2) == 0)
    def _():
        h_ref[...] = jnp.zeros_like(h_ref)
        tail_ref[...] = jnp.zeros_like(tail_ref)

    y = y_ref[...]
    tail = tail_ref[...]
    cw = cw_ref[...]
    row8 = lax.broadcasted_iota(jnp.int32, (8, LRU_BLOCK), 0)
    yc = cb_ref[...] + y * cw[CONV_W - 1:CONV_W, :]
    for d in range(1, CONV_W):
        ys = pltpu.roll(y, d, 0)
        first = jnp.where(row8 < d, pltpu.roll(tail, d, 0), ys[:8])
        ys = jnp.concatenate([first, ys[8:]], axis=0)
        yc = yc + ys * cw[CONV_W - 1 - d:CONV_W - d, :]
    tail_ref[...] = y[rows - 8:]

    yb = yc.astype(BF16)
    r = jax.nn.sigmoid(jnp.dot(yb, wa_ref[...], preferred_element_type=F32) + ba_ref[...])
    ig = jax.nn.sigmoid(jnp.dot(yb, wi_ref[...], preferred_element_type=F32) + bi_ref[...])
    z = -lam_ref[...]
    softplus = jnp.maximum(z, 0.0) + jnp.log1p(jnp.exp(-jnp.abs(z)))
    log_a = (-LRU_C * softplus) * r
    a = jnp.exp(log_a)
    h = jnp.sqrt(1.0 - a * a) * (ig * yc)

    rowi = lax.broadcasted_iota(jnp.int32, (rows, LRU_BLOCK), 0)
    s = 1
    while s < rows:
        if s < 8:
            keep = rowi >= s
            h = jnp.where(keep, h + a * pltpu.roll(h, s, 0), h)
            a = jnp.where(keep, a * pltpu.roll(a, s, 0), a)
        else:
            h = jnp.concatenate([h[:s], h[s:] + a[s:] * h[:rows - s]], axis=0)
            a = jnp.concatenate([a[:s], a[s:] * a[:rows - s]], axis=0)
        s *= 2
    h = h + a * h_ref[0:1, :]
    h_ref[...] = jnp.broadcast_to(h[rows - 1:rows, :], h_ref.shape)
    o_ref[...] = (h * gate_ref[...].astype(F32)).astype(BF16)


def _lru_core(y, gate, conv_w, conv_b, w_a, b_a, w_i, b_i, lam, *, B, S, rows=512):
    T, W = y.shape
    rows = min(rows, S)
    ns = S // rows
    blk = pl.BlockSpec((rows, LRU_BLOCK), lambda n, b, s: (b * ns + s, n))
    vec = pl.BlockSpec((1, LRU_BLOCK), lambda n, b, s: (0, n))
    wblk = pl.BlockSpec((None, LRU_BLOCK, LRU_BLOCK), lambda n, b, s: (n, 0, 0))
    return pl.pallas_call(
        functools.partial(_lru_kernel, rows=rows),
        grid=(LRU_NBLK, B, ns),
        in_specs=[blk, blk,
                  pl.BlockSpec((CONV_W, LRU_BLOCK), lambda n, b, s: (0, n)), vec,
                  wblk, vec, wblk, vec, vec],
        out_specs=blk,
        out_shape=jax.ShapeDtypeStruct((T, W), BF16),
        scratch_shapes=[pltpu.VMEM((8, LRU_BLOCK), F32), pltpu.VMEM((8, LRU_BLOCK), F32)],
        compiler_params=_cparams(("parallel", "parallel", "arbitrary")),
        name="lru_core",
    )(y, gate, conv_w, conv_b.reshape(1, W), w_a, b_a.reshape(1, W), w_i, b_i.reshape(1, W),
      lam.reshape(1, W))


def _rglru_layer(x32, xb, p, ln_g, ln_b, *, B, S):
    gate = _proj_nn(xb, p["w_gate"], out_dtype=BF16, act="gelu")
    y = _proj_nn(xb, p["w_y"], out_dtype=F32)
    hg = _lru_core(y, gate, p["conv_w"], p["conv_b"], p["w_a"], p["b_a"], p["w_i"], p["b_i"],
                   p["lam"], B=B, S=S)
    return _proj_ln(hg, p["w_out"], x32, ln_g, ln_b)


def _router_kernel(x_ref, wt_ref, b_ref, ids_ref, wts_ref, pos_ref, cnt_ref, carry_ref, *, tm):
    @pl.when(pl.program_id(0) == 0)
    def _():
        carry_ref[...] = jnp.zeros_like(carry_ref)

    logits = lax.dot_general(wt_ref[...], x_ref[...], (((1,), (1,)), ((), ())),
                             precision=lax.Precision.HIGHEST,
                             preferred_element_type=F32) + b_ref[...]
    m = jnp.max(logits, axis=0, keepdims=True)
    e = jnp.exp(logits - m)
    p = e / jnp.sum(e, axis=0, keepdims=True)
    rows = [p[k:k + 1, :] for k in range(N_EXPERTS)]

    scores = []
    for g in range(N_GROUPS):
        a, b, c, d = rows[4 * g:4 * g + 4]
        hi1, lo1, hi2, lo2 = jnp.maximum(a, b), jnp.minimum(a, b), jnp.maximum(c, d), jnp.minimum(c, d)
        scores.append(jnp.maximum(hi1, hi2) + jnp.maximum(jnp.minimum(hi1, hi2), jnp.maximum(lo1, lo2)))
    best, gsel = scores[0], jnp.zeros((1, tm), jnp.int32)
    for g in range(1, N_GROUPS):
        upd = scores[g] > best
        gsel = jnp.where(upd, g, gsel)
        best = jnp.where(upd, scores[g], best)

    cand = []
    for k in range(EXPERTS_PER_GROUP):
        c = rows[k]
        for g in range(1, N_GROUPS):
            c = jnp.where(gsel == g, rows[4 * g + k], c)
        cand.append(c)
    v1, i1 = cand[0], jnp.zeros((1, tm), jnp.int32)
    for k in range(1, EXPERTS_PER_GROUP):
        upd = cand[k] > v1
        i1 = jnp.where(upd, k, i1)
        v1 = jnp.where(upd, cand[k], v1)
    v2, i2 = jnp.full((1, tm), -1.0, F32), jnp.zeros((1, tm), jnp.int32)
    for k in range(EXPERTS_PER_GROUP):
        c = jnp.where(i1 == k, -1.0, cand[k])
        upd = c > v2
        i2 = jnp.where(upd, k, i2)
        v2 = jnp.where(upd, c, v2)
    e0 = gsel * EXPERTS_PER_GROUP + i1
    e1 = gsel * EXPERTS_PER_GROUP + i2
    tot = v1 + v2

    eio = lax.broadcasted_iota(jnp.int32, (N_EXPERTS, tm), 0)
    hit0, hit1 = eio == e0, eio == e1
    onehot = jnp.where(hit0 | hit1, 1.0, 0.0)
    before = (lax.broadcasted_iota(jnp.int32, (tm, tm), 0)
              < lax.broadcasted_iota(jnp.int32, (tm, tm), 1))
    rank = jnp.dot(onehot.astype(BF16), jnp.where(before, 1.0, 0.0).astype(BF16),
                   preferred_element_type=F32) + carry_ref[:, 0:1]
    pos0 = jnp.sum(jnp.where(hit0, rank, 0.0), axis=0, keepdims=True)
    pos1 = jnp.sum(jnp.where(hit1, rank, 0.0), axis=0, keepdims=True)
    carry_ref[...] = carry_ref[...] + jnp.sum(onehot, axis=1, keepdims=True)
    cnt_ref[...] = carry_ref[...]

    zi = jnp.zeros((6, tm), jnp.int32)
    ids_ref[...] = jnp.concatenate([e0, e1, zi], axis=0)
    pos_ref[...] = jnp.concatenate([pos0.astype(jnp.int32), pos1.astype(jnp.int32), zi], axis=0)
    wts_ref[...] = jnp.concatenate([v1 / tot, v2 / tot, jnp.zeros((6, tm), F32)], axis=0)


def _router(x32, router_wt, router_b, *, tm=512):
    T, D = x32.shape
    tm = min(tm, T)
    tok = pl.BlockSpec((8, tm), lambda i: (0, i))
    return pl.pallas_call(
        functools.partial(_router_kernel, tm=tm),
        grid=(T // tm,),
        in_specs=[pl.BlockSpec((tm, D), lambda i: (i, 0)),
                  pl.BlockSpec((N_EXPERTS, D), lambda i: (0, 0)),
                  pl.BlockSpec((N_EXPERTS, 1), lambda i: (0, 0))],
        out_specs=[tok, tok, tok, pl.BlockSpec((N_EXPERTS, LANE), lambda i: (0, 0))],
        out_shape=[jax.ShapeDtypeStruct((8, T), jnp.int32), jax.ShapeDtypeStruct((8, T), F32),
                   jax.ShapeDtypeStruct((8, T), jnp.int32),
                   jax.ShapeDtypeStruct((N_EXPERTS, LANE), F32)],
        scratch_shapes=[pltpu.VMEM((N_EXPERTS, LANE), F32)],
        compiler_params=_cparams(("arbitrary",)),
        name="moe_router",
    )(x32, router_wt, router_b.reshape(N_EXPERTS, 1))


def _row_copy(src, src_row, dst, dst_row, sem, n=1):
    return pltpu.make_async_copy(src.at[pl.ds(src_row, n)], dst.at[pl.ds(dst_row, n)], sem)


def _dispatch_kernel(d0_ref, d1_ref, pad_ref, x_hbm, xs_hbm, sem, *, tm):
    i = pl.program_id(0)

    @pl.when(i == 0)
    def _():
        for e in range(pad_ref.shape[0]):
            fill = _row_copy(x_hbm, 0, xs_hbm, pl.multiple_of(pad_ref[e], 8), sem, EXPERT_ROWS)
            fill.start()
            fill.wait()

    def issue(r, c):
        t = i * tm + r
        _row_copy(x_hbm, t, xs_hbm, d0_ref[t], sem).start()
        _row_copy(x_hbm, t, xs_hbm, d1_ref[t], sem).start()
        return c

    def drain(r, c):
        _row_copy(x_hbm, 0, xs_hbm, 0, sem).wait()
        _row_copy(x_hbm, 0, xs_hbm, 0, sem).wait()
        return c

    lax.fori_loop(0, tm, issue, 0)
    lax.fori_loop(0, tm, drain, 0)


def _dispatch(x32, d0, d1, pad_start, n_rows, *, tm=512):
    T, D = x32.shape
    tm = min(tm, T)
    return pl.pallas_call(
        functools.partial(_dispatch_kernel, tm=tm),
        grid_spec=pltpu.PrefetchScalarGridSpec(
            num_scalar_prefetch=3, grid=(T // tm,),
            in_specs=[pl.BlockSpec(memory_space=pl.ANY)],
            out_specs=pl.BlockSpec(memory_space=pl.ANY),
            scratch_shapes=[pltpu.SemaphoreType.DMA(())]),
        out_shape=jax.ShapeDtypeStruct((n_rows, D), F32),
        compiler_params=pltpu.CompilerParams(dimension_semantics=("arbitrary",),
                                             has_side_effects=True),
        name="moe_dispatch",
    )(d0, d1, pad_start, x32)


def _ffn_kernel(be_ref, src_ref, nact_ref, x_ref, wg_ref, wu_ref, wd_ref, y_ref):
    active = pl.program_id(0) < nact_ref[0]

    @pl.when(active)
    def _():
        xb = x_ref[...].astype(BF16)
        g = jnp.dot(xb, wg_ref[...], preferred_element_type=F32)
        u = jnp.dot(xb, wu_ref[...], preferred_element_type=F32)
        h = (jax.nn.silu(g) * u).astype(BF16)
        y_ref[...] = jnp.dot(h, wd_ref[...], preferred_element_type=F32)

    @pl.when(jnp.logical_not(active))
    def _():
        y_ref[...] = jnp.zeros_like(y_ref)


def _expert_ffn(xs, blk_e, blk_src, n_act, w_gate, w_up, w_down):
    n_rows, D = xs.shape
    R = EXPERT_ROWS
    return pl.pallas_call(
        _ffn_kernel,
        grid_spec=pltpu.PrefetchScalarGridSpec(
            num_scalar_prefetch=3, grid=(n_rows // R,),
            in_specs=[pl.BlockSpec((R, D), lambda i, be, src, na: (src[i], 0)),
                      pl.BlockSpec((None, D, D_EXPERT), lambda i, be, src, na: (be[i], 0, 0)),
                      pl.BlockSpec((None, D, D_EXPERT), lambda i, be, src, na: (be[i], 0, 0)),
                      pl.BlockSpec((None, D_EXPERT, D), lambda i, be, src, na: (be[i], 0, 0))],
            out_specs=pl.BlockSpec((R, D), lambda i, be, src, na: (i, 0))),
        out_shape=jax.ShapeDtypeStruct((n_rows, D), F32),
        compiler_params=_cparams(("arbitrary",)),
        name="moe_ffn",
    )(blk_e, blk_src, n_act, xs, w_gate, w_up, w_down)


def _combine_kernel(d0_ref, d1_ref, res_ref, w_ref, g_ref, b_ref, y_hbm, of_ref, ob_ref,
                    buf, sem, *, tm):
    i = pl.program_id(0)

    def issue(r, c):
        t = i * tm + r
        _row_copy(y_hbm, d0_ref[t], buf.at[0], r, sem).start()
        _row_copy(y_hbm, d1_ref[t], buf.at[1], r, sem).start()
        return c

    def drain(r, c):
        _row_copy(y_hbm, 0, buf.at[0], 0, sem).wait()
        _row_copy(y_hbm, 0, buf.at[1], 0, sem).wait()
        return c

    lax.fori_loop(0, tm, issue, 0)
    lax.fori_loop(0, tm, drain, 0)
    w = w_ref[...]
    z = ALPHA * res_ref[...] + (w[:, 0:1] * buf[0] + w[:, 1:2] * buf[1])
    y = _layer_norm_rows(z, g_ref[...], b_ref[...])
    of_ref[...] = y
    ob_ref[...] = y.astype(BF16)


def _combine_ln(y, d0, d1, wts, res, g, b, *, tm=256):
    T, D = res.shape
    tm = min(tm, T)
    row = pl.BlockSpec((tm, D), lambda i, a, c: (i, 0))
    vec = pl.BlockSpec((1, D), lambda i, a, c: (0, 0))
    return pl.pallas_call(
        functools.partial(_combine_kernel, tm=tm),
        grid_spec=pltpu.PrefetchScalarGridSpec(
            num_scalar_prefetch=2, grid=(T // tm,),
            in_specs=[row, pl.BlockSpec((tm, 2), lambda i, a, c: (i, 0)), vec, vec,
                      pl.BlockSpec(memory_space=pl.ANY)],
            out_specs=[row, row],
            scratch_shapes=[pltpu.VMEM((2, tm, D), F32), pltpu.SemaphoreType.DMA(())]),
        out_shape=[jax.ShapeDtypeStruct((T, D), F32), jax.ShapeDtypeStruct((T, D), BF16)],
        compiler_params=_cparams(("arbitrary",)),
        name="moe_combine_ln",
    )(d0, d1, res, wts, g.reshape(1, D), b.reshape(1, D), y)


def _moe_layer(x32, router_wt, router_b, w_gate, w_up, w_down, ln_g, ln_b):
    T = x32.shape[0]
    R = EXPERT_ROWS
    ids, wts, pos, cnt = _router(x32, router_wt, router_b)
    counts = cnt[:, 0].astype(jnp.int32)
    padded = (counts + R - 1) // R * R
    p_end = jnp.cumsum(padded)
    offs = p_end - padded
    d0 = offs[ids[0]] + pos[0]
    d1 = offs[ids[1]] + pos[1]
    n_blocks = 2 * T // R + N_EXPERTS
    n_act = p_end[-1] // R
    step = jnp.arange(n_blocks, dtype=jnp.int32)
    live = step < n_act
    src = jnp.where(live, step, n_act - 1)
    blk_e = jnp.minimum(jnp.searchsorted(p_end // R, src, side="right"), N_EXPERTS - 1).astype(jnp.int32)
    n_rows = n_blocks * R
    tail = p_end[-1] + R * jnp.arange(n_blocks - 2 * T // R)
    pad_start = jnp.minimum(jnp.concatenate([(offs + counts) // 8 * 8, tail]), n_rows - R).astype(jnp.int32)
    xs = _dispatch(x32, d0, d1, pad_start, n_rows)
    y = _expert_ffn(xs, blk_e, src.astype(jnp.int32), n_act.reshape(1).astype(jnp.int32),
                    w_gate, w_up, w_down)
    return _combine_ln(y, d0, d1, wts[:2].T, x32, ln_g, ln_b)


def _t5_bucket(dist):
    n = jnp.maximum(dist, 0)
    max_exact = NUM_BUCKETS // 2
    nf = jnp.maximum(n, 1).astype(F32)
    large = max_exact + (jnp.log(nf / max_exact) / math.log(MAX_DISTANCE / max_exact)
                         * (NUM_BUCKETS - max_exact)).astype(jnp.int32)
    large = jnp.minimum(large, NUM_BUCKETS - 1)
    return jnp.where(n < max_exact, n, large)


def _stack_heads(tbl):
    parts = tbl.reshape((N_KV, GROUP) + tbl.shape[1:])
    return jnp.concatenate([parts[:, g] for g in range(GROUP)], axis=-1)


def _bias_tables(rel_bias, S):
    far = 16 * SEL_CHUNK
    bvec = rel_bias[_t5_bucket(jnp.arange(max(S, far + 1)))].T
    last = rel_bias[NUM_BUCKETS - 1]
    qi = np.arange(Q_BLOCK)

    def lookup(dist, shift=None):
        ok = dist >= 0
        val = bvec[:, np.clip(dist, 0, bvec.shape[1] - 1)]
        if shift is not None:
            val = val - shift.reshape((N_HEADS,) + (1,) * dist.ndim)
        return jnp.where(ok, val, NEG)

    ci = np.arange(CMP_TILE)
    delta = np.arange(CMP_NEAR)
    dist_c = (qi[None, None, :] - (CMP_LEN - 1) - CMP_STRIDE * ci[None, :, None]
              + Q_BLOCK * delta[:, None, None])
    near_c = lookup(dist_c)
    tc = jnp.concatenate([jnp.full((N_HEADS, 1, CMP_TILE, Q_BLOCK), NEG, F32), near_c,
                          jnp.broadcast_to(last[:, None, None, None], (N_HEADS, 1, CMP_TILE, Q_BLOCK))],
                         axis=1)
    sub = np.arange(SEL_CHUNK // Q_BLOCK)
    near = np.arange(SEL_NEAR)
    kk = np.arange(SEL_CHUNK)
    dist_s = (Q_BLOCK * sub[:, None, None, None] + qi[None, None, None, :]
              + SEL_CHUNK * (SEL_NEAR - 1 - near)[None, :, None, None] - kk[None, None, :, None])
    ts = lookup(dist_s, shift=last)
    kj = np.arange(WINDOW + Q_BLOCK)
    dw = qi[None, :] + WINDOW - kj[:, None]
    tw = jnp.where((dw < WINDOW)[None], lookup(dw), NEG)
    return _stack_heads(tc), _stack_heads(ts), _stack_heads(tw)


def _compress_kernel(x_ref, pe_ref, w1_ref, w2_ref, o_ref, *, transposed):
    x = x_ref[...]
    pe = pe_ref[...]
    half = CMP_STRIDE * HEAD_DIM
    a = jnp.dot((x + pe[0:1]).astype(BF16), w1_ref[:half], preferred_element_type=F32)
    b = jnp.dot((x + pe[1:2]).astype(BF16), w1_ref[half:], preferred_element_type=F32)
    n = x.shape[0]
    pre = a + pltpu.roll(b, n - 1, 0)
    rowi = lax.broadcasted_iota(jnp.int32, pre.shape, 0)
    hid = jnp.where(rowi < n - 1, jax.nn.gelu(pre), 0.0).astype(BF16)
    if transposed:
        out = lax.dot_general(w2_ref[...], hid, (((1,), (1,)), ((), ())), preferred_element_type=F32)
    else:
        out = jnp.dot(hid, w2_ref[...], preferred_element_type=F32)
    o_ref[...] = out.astype(BF16)


def _compress(slabs, pe, w1, w2, *, B, S, transposed):
    n = S // CMP_STRIDE
    wide = CMP_STRIDE * HEAD_DIM
    out_shape = (N_KV, B, HEAD_DIM, n) if transposed else (N_KV, B, n, HEAD_DIM)
    w2_in = w2.T if transposed else w2
    return pl.pallas_call(
        functools.partial(_compress_kernel, transposed=transposed),
        grid=(N_KV, B),
        in_specs=[pl.BlockSpec((None, None, n, wide), lambda h, b: (h, b, 0, 0)),
                  pl.BlockSpec((2, wide), lambda h, b: (0, 0)),
                  pl.BlockSpec((2 * wide, HEAD_DIM), lambda h, b: (0, 0)),
                  pl.BlockSpec((HEAD_DIM, HEAD_DIM), lambda h, b: (0, 0))],
        out_specs=pl.BlockSpec((None, None) + out_shape[2:], lambda h, b: (h, b, 0, 0)),
        out_shape=jax.ShapeDtypeStruct(out_shape, BF16),
        compiler_params=_cparams(("parallel", "parallel")),
        name="nsa_compress",
    )(slabs, pe.reshape(2, wide), w1, w2_in)


def _query_cols(q_ref):
    return jnp.concatenate([q_ref[g * HEAD_DIM:(g + 1) * HEAD_DIM, :] for g in range(GROUP)], axis=1)


def _cmp_kernel(q_ref, kc_ref, vct_ref, tc_ref, ov_ref, oc_ref, mb_ref, *, n_tiles, n_sel):
    ib = pl.program_id(2)
    qt = _query_cols(q_ref)
    s = jnp.dot(kc_ref[...], qt, preferred_element_type=F32)
    tiles = []
    for cb in range(n_tiles):
        idx = jnp.clip(ib - (CMP_TILE * CMP_STRIDE // Q_BLOCK) * cb + 1, 0, CMP_NEAR + 1)
        tiles.append(s[cb * CMP_TILE:(cb + 1) * CMP_TILE] + tc_ref[idx])
    s = jnp.concatenate(tiles, axis=0)
    valid = s > 0.5 * NEG
    m = jnp.max(s, axis=0, keepdims=True)
    p = jnp.where(valid, jnp.exp(s - m), 0.0)
    p = p / jnp.maximum(jnp.sum(p, axis=0, keepdims=True), 1e-30)
    oc_ref[...] = jnp.dot(vct_ref[...], p.astype(BF16), preferred_element_type=F32)

    psum = p[:, 0:Q_BLOCK]
    for g in range(1, GROUP):
        psum = psum + p[:, g * Q_BLOCK:(g + 1) * Q_BLOCK]
    p_hi = psum.astype(BF16)
    p_lo = (psum - p_hi.astype(F32)).astype(BF16)
    imp = (jnp.dot(ov_ref[...], p_hi, preferred_element_type=F32)
           + jnp.dot(ov_ref[...], p_lo, preferred_element_type=F32))
    blk = lax.broadcasted_iota(jnp.int32, (n_sel, Q_BLOCK), 0)
    t = ib * Q_BLOCK + lax.broadcasted_iota(jnp.int32, (n_sel, Q_BLOCK), 1)
    cur = jnp.right_shift(t, int(math.log2(SEL_LEN)))
    ok = blk * SEL_LEN <= t
    forced = ok & ((blk == 0) | (blk == cur) | (blk == cur - 1))
    score = jnp.where(forced, BIG, jnp.where(ok, imp, NEG))
    taken = jnp.zeros((n_sel, Q_BLOCK), jnp.bool_)
    for _ in range(SEL_TOP):
        top = jnp.max(score, axis=0, keepdims=True)
        first = jnp.min(jnp.where(score == top, blk, n_sel), axis=0, keepdims=True)
        pick = blk == first
        taken = taken | pick
        score = jnp.where(pick, -jnp.inf, score)
    mb_ref[...] = jnp.where(taken & ok, 0.0, NEG).astype(BF16)


def _cmp_attention(qt, kc, vct, tc, overlap_t, *, B, S):
    nq = S // Q_BLOCK
    n_cmp = S // CMP_STRIDE
    n_sel = S // SEL_LEN
    n_tiles = n_cmp // CMP_TILE
    return pl.pallas_call(
        functools.partial(_cmp_kernel, n_tiles=n_tiles, n_sel=n_sel),
        grid=(N_KV, B, nq),
        in_specs=[pl.BlockSpec((GROUP * HEAD_DIM, Q_BLOCK), lambda h, b, i: (h, b * nq + i)),
                  pl.BlockSpec((None, None, n_cmp, HEAD_DIM), lambda h, b, i: (h, b, 0, 0)),
                  pl.BlockSpec((None, None, HEAD_DIM, n_cmp), lambda h, b, i: (h, b, 0, 0)),
                  pl.BlockSpec((None, CMP_NEAR + 2, CMP_TILE, QG), lambda h, b, i: (h, 0, 0, 0)),
                  pl.BlockSpec((n_sel, n_cmp), lambda h, b, i: (0, 0))],
        out_specs=[pl.BlockSpec((None, None, HEAD_DIM, QG), lambda h, b, i: (h, b * nq + i, 0, 0)),
                   pl.BlockSpec((None, None, n_sel, Q_BLOCK), lambda h, b, i: (h, b, 0, i))],
        out_shape=[jax.ShapeDtypeStruct((N_KV, B * nq, HEAD_DIM, QG), F32),
                   jax.ShapeDtypeStruct((N_KV, B, n_sel, S), BF16)],
        compiler_params=_cparams(("parallel", "parallel", "parallel")),
        name="nsa_cmp_select",
    )(qt, kc, vct, tc, overlap_t)


def _sel_kernel(q_ref, mb_ref, ka_ref, vt_ref, ts_ref, o_ref, m_ref, l_ref, acc_ref):
    ib = pl.program_id(2)
    per = SEL_CHUNK // Q_BLOCK
    cq = ib // per
    sub = ib % per
    qt = _query_cols(q_ref)
    blocks_per_chunk = SEL_CHUNK // SEL_LEN
    chunks_per_group = LANE // blocks_per_chunk

    m_ref[...] = jnp.full(m_ref.shape, NEG, F32)
    l_ref[...] = jnp.zeros_like(l_ref)
    acc_ref[...] = jnp.zeros_like(acc_ref)

    def step(c, bias):
        grp = c // chunks_per_group
        mb = mb_ref[pl.ds(pl.multiple_of(grp * LANE, LANE), LANE), :]
        qa = jnp.concatenate([qt, jnp.concatenate([mb] * GROUP, axis=1)], axis=0)
        s = jnp.dot(ka_ref[c], qa, preferred_element_type=F32)
        if bias is not None:
            s = s + bias
        m_old = m_ref[0:1, :]
        m_new = jnp.maximum(m_old, jnp.max(s, axis=0, keepdims=True))
        alpha = jnp.exp(m_old - m_new)
        p = jnp.exp(s - m_new)
        l_ref[...] = jnp.broadcast_to(alpha * l_ref[0:1, :] + jnp.sum(p, axis=0, keepdims=True),
                                      l_ref.shape)
        acc_ref[...] = alpha * acc_ref[...] + jnp.dot(vt_ref[c], p.astype(BF16),
                                                      preferred_element_type=F32)
        m_ref[...] = jnp.broadcast_to(m_new, m_ref.shape)

    def far(c, carry):
        step(c, None)
        return carry

    lax.fori_loop(0, jnp.maximum(cq - (SEL_NEAR - 1), 0), far, 0)
    for i in range(SEL_NEAR):
        c = cq - (SEL_NEAR - 1) + i

        @pl.when(c >= 0)
        def _():
            step(c, ts_ref[sub, i])

    o_ref[...] = acc_ref[...] / jnp.maximum(l_ref[0:1, :], 1e-30)


def _sel_attention(qt, mbt, k_aug, vt, ts, *, B, S):
    nq = S // Q_BLOCK
    n_sel = mbt.shape[2]
    nc = S // SEL_CHUNK
    return pl.pallas_call(
        _sel_kernel,
        grid=(N_KV, B, nq),
        in_specs=[pl.BlockSpec((GROUP * HEAD_DIM, Q_BLOCK), lambda h, b, i: (h, b * nq + i)),
                  pl.BlockSpec((None, None, n_sel, Q_BLOCK), lambda h, b, i: (h, b, 0, i)),
                  pl.BlockSpec((None, None, nc, SEL_CHUNK, 2 * HEAD_DIM), lambda h, b, i: (h, b, 0, 0, 0)),
                  pl.BlockSpec((None, None, nc, HEAD_DIM, SEL_CHUNK), lambda h, b, i: (h, b, 0, 0, 0)),
                  pl.BlockSpec((None, SEL_CHUNK // Q_BLOCK, SEL_NEAR, SEL_CHUNK, QG),
                               lambda h, b, i: (h, 0, 0, 0, 0))],
        out_specs=pl.BlockSpec((None, None, HEAD_DIM, QG), lambda h, b, i: (h, b * nq + i, 0, 0)),
        out_shape=jax.ShapeDtypeStruct((N_KV, B * nq, HEAD_DIM, QG), F32),
        scratch_shapes=[pltpu.VMEM((8, QG), F32), pltpu.VMEM((8, QG), F32),
                        pltpu.VMEM((HEAD_DIM, QG), F32)],
        compiler_params=_cparams(("parallel", "parallel", "arbitrary")),
        name="nsa_selected",
    )(qt, mbt, k_aug, vt, ts)


def _win_kernel(*refs):
    nb = WINDOW // Q_BLOCK + 1
    q_ref = refs[0]
    k_refs = refs[1:1 + nb]
    v_refs = refs[1 + nb:1 + 2 * nb]
    tw_ref, gate_ref, oc_ref, os_ref, o_ref = refs[1 + 2 * nb:]
    ib = pl.program_id(2)
    qt = _query_cols(q_ref)
    parts = []
    for i in range(nb):
        s = jnp.dot(k_refs[i][...], qt, preferred_element_type=F32)
        s = s + tw_ref[i * Q_BLOCK:(i + 1) * Q_BLOCK, :]
        parts.append(jnp.where(ib - (nb - 1) + i >= 0, s, NEG))
    s = jnp.concatenate(parts, axis=0)
    m = jnp.max(s, axis=0, keepdims=True)
    p = jnp.exp(s - m)
    l = jnp.sum(p, axis=0, keepdims=True)
    pb = p.astype(BF16)
    ow = jnp.dot(v_refs[0][...], pb[0:Q_BLOCK], preferred_element_type=F32)
    for i in range(1, nb):
        ow = ow + jnp.dot(v_refs[i][...], pb[i * Q_BLOCK:(i + 1) * Q_BLOCK], preferred_element_type=F32)
    ow = ow / jnp.maximum(l, 1e-30)

    def gate(j):
        return jnp.concatenate([gate_ref[j, g:g + 1, :] for g in range(GROUP)], axis=1)

    o = gate(0) * oc_ref[...] + gate(1) * os_ref[...] + gate(2) * ow
    for g in range(GROUP):
        o_ref[:, g * HEAD_DIM:(g + 1) * HEAD_DIM] = o[:, g * Q_BLOCK:(g + 1) * Q_BLOCK].T.astype(BF16)


def _win_attention(qt, kw, vwt, tw, gates_t, oc, osel, *, B, S):
    nq = S // Q_BLOCK
    nb = WINDOW // Q_BLOCK + 1

    def kmap(i_off):
        return lambda h, b, i: (h, b * nq + jnp.maximum(i - (nb - 1) + i_off, 0), 0)

    def vmap_(i_off):
        return lambda h, b, i: (h, 0, b * nq + jnp.maximum(i - (nb - 1) + i_off, 0))

    blk = pl.BlockSpec((None, None, HEAD_DIM, QG), lambda h, b, i: (h, b * nq + i, 0, 0))
    return pl.pallas_call(
        _win_kernel,
        grid=(N_KV, B, nq),
        in_specs=([pl.BlockSpec((GROUP * HEAD_DIM, Q_BLOCK), lambda h, b, i: (h, b * nq + i))]
                  + [pl.BlockSpec((None, Q_BLOCK, HEAD_DIM), kmap(j)) for j in range(nb)]
                  + [pl.BlockSpec((None, HEAD_DIM, Q_BLOCK), vmap_(j)) for j in range(nb)]
                  + [pl.BlockSpec((None, WINDOW + Q_BLOCK, QG), lambda h, b, i: (h, 0, 0)),
                     pl.BlockSpec((3, None, GROUP, Q_BLOCK), lambda h, b, i: (0, h, 0, b * nq + i)),
                     blk, blk]),
        out_specs=pl.BlockSpec((Q_BLOCK, GROUP * HEAD_DIM), lambda h, b, i: (b * nq + i, h)),
        out_shape=jax.ShapeDtypeStruct((B * S, N_HEADS * HEAD_DIM), BF16),
        compiler_params=_cparams(("parallel", "parallel", "parallel")),
        name="nsa_window_merge",
    )(qt, *([kw] * nb), *([vwt] * nb), tw, gates_t, oc, osel)


def _nsa_layer(x32, xb, p, tables, consts, ln_g, ln_b, *, B, S):
    T = B * S
    tc, ts, tw = tables
    overlap_t, sel_onehot = consts
    kd = N_KV * HEAD_DIM
    qv = _proj_nt(p["w_qv_t"], xb, out_dtype=BF16, scale=HEAD_DIM ** -0.5,
                  n_scaled=N_HEADS * HEAD_DIM // 512)
    qt = qv[:N_HEADS * HEAD_DIM]
    k2 = _proj_nn(xb, p["w_k"], out_dtype=BF16, head_major=True)
    cmp = _proj_nn(xb, p["w_cmp"], out_dtype=F32, head_major=True)
    gates_t = _proj_nt(p["w_gate_t"], xb, out_dtype=F32, act="sigmoid")
    gates_t = gates_t.reshape(3, N_KV, GROUP, T)

    cmp = cmp.reshape(2, N_KV, B, S // CMP_STRIDE, CMP_STRIDE * HEAD_DIM)
    kc = _compress(cmp[0], p["pe_k"], p["w1_k"], p["w2_k"], B=B, S=S, transposed=False)
    vct = _compress(cmp[1], p["pe_v"], p["w1_v"], p["w2_v"], B=B, S=S, transposed=True)
    oc, mbt = _cmp_attention(qt, kc, vct, tc, overlap_t, B=B, S=S)
    if mbt.shape[2] % LANE:
        mbt = jnp.pad(mbt, ((0, 0), (0, 0), (0, -mbt.shape[2] % LANE), (0, 0)), constant_values=NEG)

    nc = S // SEL_CHUNK
    k_sel = k2[:N_KV].reshape(N_KV, B, nc, SEL_CHUNK, HEAD_DIM)
    k_aug = jnp.concatenate([k_sel, jnp.broadcast_to(sel_onehot, k_sel.shape)], axis=-1)
    v_sel_t = qv[N_HEADS * HEAD_DIM:N_HEADS * HEAD_DIM + kd].reshape(N_KV, HEAD_DIM, B, nc, SEL_CHUNK)
    v_sel_t = v_sel_t.transpose(0, 2, 3, 1, 4)
    osel = _sel_attention(qt, mbt, k_aug, v_sel_t, ts, B=B, S=S)

    v_win_t = qv[N_HEADS * HEAD_DIM + kd:].reshape(N_KV, HEAD_DIM, T)
    attn = _win_attention(qt, k2[N_KV:], v_win_t, tw, gates_t, oc, osel, B=B, S=S)
    return _proj_ln(attn, p["w_out"], x32, ln_g, ln_b)


def _nsa_params(w_in, pe_k, w1_k, w2_k, pe_v, w1_v, w2_v, w_out):
    D = w_in.shape[0]
    qd, kd = N_HEADS * HEAD_DIM, N_KV * HEAD_DIM
    w_kv = w_in[:, qd:qd + 6 * kd].reshape(D, 6, kd)
    gcol = np.array([[[(kv * GROUP + g) * 3 + j for g in range(GROUP)] for kv in range(N_KV)]
                     for j in range(3)]).reshape(-1)
    return dict(
        w_qv_t=jnp.concatenate([w_in[:, :qd], w_kv[:, 3], w_kv[:, 5]], axis=1).T.astype(BF16),
        w_k=jnp.concatenate([w_kv[:, 2], w_kv[:, 4]], axis=1).astype(BF16),
        w_cmp=jnp.concatenate([w_kv[:, 0], w_kv[:, 1]], axis=1).astype(BF16),
        w_gate_t=w_in[:, qd + 6 * kd:][:, gcol].T.astype(BF16),
        pe_k=pe_k, w1_k=w1_k.astype(BF16), w2_k=w2_k.astype(BF16),
        pe_v=pe_v, w1_v=w1_v.astype(BF16), w2_v=w2_v.astype(BF16),
        w_out=w_out.astype(BF16))


def _nsa_constants(S):
    n_cmp, n_sel = S // CMP_STRIDE, S // SEL_LEN
    cj = np.arange(n_cmp) * CMP_STRIDE
    sb = np.arange(n_sel) * SEL_LEN
    ov = np.maximum(np.minimum(cj[None, :] + CMP_LEN, sb[:, None] + SEL_LEN)
                    - np.maximum(cj[None, :], sb[:, None]), 0).astype(np.float32) / CMP_LEN
    ov[:, n_cmp - 1] = 0.0
    key = np.arange(SEL_CHUNK)
    chunk = np.arange(S // SEL_CHUNK)
    blk = (chunk[:, None] * (SEL_CHUNK // SEL_LEN) + key[None, :] // SEL_LEN) % LANE
    onehot = (blk[:, :, None] == np.arange(LANE)[None, None, :]).astype(np.float32)
    return jnp.asarray(ov, BF16), jnp.asarray(onehot, BF16)


def kernel(x, rel_bias, router_w, router_b, ln1_g, ln1_b, ln2_g, ln2_b, lru_w_in, lru_conv_w,
           lru_conv_b, lru_w_a, lru_b_a, lru_w_i, lru_b_i, lru_lambda, lru_w_out, nsa_w_in,
           nsa_pe_k, nsa_w1_k, nsa_w2_k, nsa_pe_v, nsa_w1_v, nsa_w2_v, nsa_w_out,
           moe_w_gate, moe_w_up, moe_w_down):
    B, S, D = x.shape
    T = B * S
    assert D == D_MODEL and S % (CMP_TILE * CMP_STRIDE) == 0 and S // SEL_LEN >= SEL_TOP
    qd, kd = N_HEADS * HEAD_DIM, N_KV * HEAD_DIM
    tables = _bias_tables(rel_bias, S)
    consts = _nsa_constants(S)
    router_wt = router_w.T

    x32 = x.reshape(T, D)
    xb = x32.astype(BF16)
    for layer in range(DEPTH):
        j = layer // 2
        if layer % 2 == 0:
            w_in = lru_w_in[j].astype(BF16)
            p = dict(w_gate=w_in[:, :D], w_y=w_in[:, D:], conv_w=lru_conv_w[j], conv_b=lru_conv_b[j],
                     w_a=lru_w_a[j].astype(BF16), b_a=lru_b_a[j], w_i=lru_w_i[j].astype(BF16),
                     b_i=lru_b_i[j], lam=lru_lambda[j], w_out=lru_w_out[j].astype(BF16))
            x32, xb = _rglru_layer(x32, xb, p, ln1_g[layer], ln1_b[layer], B=B, S=S)
        else:
            p = _nsa_params(nsa_w_in[j], nsa_pe_k[j], nsa_w1_k[j], nsa_w2_k[j], nsa_pe_v[j],
                            nsa_w1_v[j], nsa_w2_v[j], nsa_w_out[j])
            x32, xb = _nsa_layer(x32, xb, p, tables, consts, ln1_g[layer], ln1_b[layer], B=B, S=S)
        x32, xb = _moe_layer(x32, router_wt, router_b, moe_w_gate[layer].astype(BF16),
                             moe_w_up[layer].astype(BF16), moe_w_down[layer].astype(BF16),
                             ln2_g[layer], ln2_b[layer])
    return x32.reshape(B, S, D)
```

```python
import functools
import math

import numpy as np
import jax
import jax.numpy as jnp
from jax import lax
from jax.experimental import pallas as pl
from jax.experimental.pallas import tpu as pltpu

F32 = jnp.float32
BF16 = jnp.bfloat16

D_MODEL = 2048
DEPTH = 4
LRU_BLOCK = 256
LRU_NBLK = D_MODEL // LRU_BLOCK
CONV_W = 4
LRU_C = 8.0
HEAD_DIM = 128
N_HEADS = 16
N_KV = 4
GROUP = 4
CMP_LEN = 32
CMP_STRIDE = 16
SEL_LEN = 64
SEL_TOP = 16
WINDOW = 512
Q_BLOCK = 128
NUM_BUCKETS = 32
MAX_DISTANCE = 1024
N_EXPERTS = 16
N_GROUPS = 4
EXPERTS_PER_GROUP = 4
D_EXPERT = 1024
ALPHA = (2 * DEPTH) ** 0.25
LN_EPS = 1e-5
NEG = -1e30
BIG = 1e9

LANE = 128
QG = GROUP * Q_BLOCK
EXPERT_ROWS = 256
SEL_CHUNK = 256
SEL_NEAR = 5
SEL_FAR = 4
LOG2E = math.log2(math.e)
CMP_TILE = 128
CMP_NEAR = 23
VMEM_LIMIT = 56 << 20


def _cparams(sem, vmem=VMEM_LIMIT):
    return pltpu.CompilerParams(dimension_semantics=sem, vmem_limit_bytes=vmem)


def _act(x, act):
    if act == "gelu":
        return jax.nn.gelu(x)
    if act == "sigmoid":
        return jax.nn.sigmoid(x)
    return x


def _proj_nn_kernel(x_ref, w_ref, o_ref, *, act, head_major):
    acc = jnp.dot(x_ref[...], w_ref[...], preferred_element_type=F32)
    acc = _act(acc, act)
    if head_major:
        for h in range(o_ref.shape[0]):
            o_ref[h] = acc[:, h * LANE:(h + 1) * LANE].astype(o_ref.dtype)
    else:
        o_ref[...] = acc.astype(o_ref.dtype)


def _proj_nn(x, w, *, out_dtype, act=None, head_major=False, tm=1024, tn=1024):
    T, K = x.shape
    N = w.shape[1]
    tm, tn = min(tm, T), min(tn, N)
    if head_major:
        out_shape = (N // LANE, T, LANE)
        out_spec = pl.BlockSpec((tn // LANE, tm, LANE), lambda j, i: (j, i, 0))
    else:
        out_shape = (T, N)
        out_spec = pl.BlockSpec((tm, tn), lambda j, i: (i, j))
    return pl.pallas_call(
        functools.partial(_proj_nn_kernel, act=act, head_major=head_major),
        grid=(N // tn, T // tm),
        in_specs=[pl.BlockSpec((tm, K), lambda j, i: (i, 0)),
                  pl.BlockSpec((K, tn), lambda j, i: (0, j))],
        out_specs=out_spec,
        out_shape=jax.ShapeDtypeStruct(out_shape, out_dtype),
        compiler_params=_cparams(("parallel", "parallel")),
        name="proj_nn",
    )(x, w)


def _proj_nt_kernel(wt_ref, x_ref, o_ref, *, act, scale, n_scaled):
    acc = lax.dot_general(wt_ref[...], x_ref[...], (((1,), (1,)), ((), ())),
                          preferred_element_type=F32)
    if n_scaled:
        acc = acc * jnp.where(pl.program_id(0) < n_scaled, scale, 1.0).astype(F32)
    o_ref[...] = _act(acc, act).astype(o_ref.dtype)


def _proj_nt(wt, x, *, out_dtype, act=None, scale=1.0, n_scaled=0, tm=1024, tn=512):
    N, K = wt.shape
    T = x.shape[0]
    tm, tn = min(tm, T), min(tn, N)
    return pl.pallas_call(
        functools.partial(_proj_nt_kernel, act=act, scale=scale, n_scaled=n_scaled),
        grid=(N // tn, T // tm),
        in_specs=[pl.BlockSpec((tn, K), lambda j, i: (j, 0)),
                  pl.BlockSpec((tm, K), lambda j, i: (i, 0))],
        out_specs=pl.BlockSpec((tn, tm), lambda j, i: (j, i)),
        out_shape=jax.ShapeDtypeStruct((N, T), out_dtype),
        compiler_params=_cparams(("parallel", "parallel")),
        name="proj_nt",
    )(wt, x)


def _layer_norm_rows(z, g, b):
    mu = jnp.mean(z, axis=-1, keepdims=True)
    zc = z - mu
    var = jnp.mean(zc * zc, axis=-1, keepdims=True)
    return zc * lax.rsqrt(var + LN_EPS) * g + b


def _proj_ln_kernel(a_ref, w_ref, res_ref, g_ref, b_ref, of_ref, ob_ref):
    acc = jnp.dot(a_ref[...], w_ref[...], preferred_element_type=F32)
    y = _layer_norm_rows(ALPHA * res_ref[...] + acc, g_ref[...], b_ref[...])
    of_ref[...] = y
    ob_ref[...] = y.astype(BF16)


def _proj_ln(a, w, res, g, b, *, tm=256):
    T, K = a.shape
    D = w.shape[1]
    tm = min(tm, T)
    row = pl.BlockSpec((tm, D), lambda i: (i, 0))
    vec = pl.BlockSpec((1, D), lambda i: (0, 0))
    return pl.pallas_call(
        _proj_ln_kernel,
        grid=(T // tm,),
        in_specs=[pl.BlockSpec((tm, K), lambda i: (i, 0)),
                  pl.BlockSpec((K, D), lambda i: (0, 0)),
                  row, vec, vec],
        out_specs=[row, row],
        out_shape=[jax.ShapeDtypeStruct((T, D), F32), jax.ShapeDtypeStruct((T, D), BF16)],
        compiler_params=_cparams(("parallel",)),
        name="proj_ln",
    )(a, w, res, g.reshape(1, D), b.reshape(1, D))


def _lru_kernel(y_ref, gate_ref, cw_ref, cb_ref, wa_ref, ba_ref, wi_ref, bi_ref, lam_ref,
                o_ref, h_ref, tail_ref, *, rows):
    @pl.when(pl.program_id(2) == 0)
    def _():
        h_ref[...] = jnp.zeros_like(h_ref)
        tail_ref[...] = jnp.zeros_like(tail_ref)

    y = y_ref[...]
    tail = tail_ref[...]
    cw = cw_ref[...]
    row8 = lax.broadcasted_iota(jnp.int32, (8, LRU_BLOCK), 0)
    yc = cb_ref[...] + y * cw[CONV_W - 1:CONV_W, :]
    for d in range(1, CONV_W):
        ys = pltpu.roll(y, d, 0)
        first = jnp.where(row8 < d, pltpu.roll(tail, d, 0), ys[:8])
        ys = jnp.concatenate([first, ys[8:]], axis=0)
        yc = yc + ys * cw[CONV_W - 1 - d:CONV_W - d, :]
    tail_ref[...] = y[rows - 8:]

    yb = yc.astype(BF16)
    r = jax.nn.sigmoid(jnp.dot(yb, wa_ref[...], preferred_element_type=F32) + ba_ref[...])
    ig = jax.nn.sigmoid(jnp.dot(yb, wi_ref[...], preferred_element_type=F32) + bi_ref[...])
    z = -lam_ref[...]
    softplus = jnp.maximum(z, 0.0) + jnp.log1p(jnp.exp(-jnp.abs(z)))
    log_a = (-LRU_C * softplus) * r
    a = jnp.exp(log_a)
    h = jnp.sqrt(1.0 - a * a) * (ig * yc)

    rowi = lax.broadcasted_iota(jnp.int32, (rows, LRU_BLOCK), 0)
    s = 1
    while s < rows:
        if s < 8:
            keep = rowi >= s
            h = jnp.where(keep, h + a * pltpu.roll(h, s, 0), h)
            a = jnp.where(keep, a * pltpu.roll(a, s, 0), a)
        else:
            h = jnp.concatenate([h[:s], h[s:] + a[s:] * h[:rows - s]], axis=0)
            a = jnp.concatenate([a[:s], a[s:] * a[:rows - s]], axis=0)
        s *= 2
    h = h + a * h_ref[0:1, :]
    h_ref[...] = jnp.broadcast_to(h[rows - 1:rows, :], h_ref.shape)
    o_ref[...] = (h * gate_ref[...].astype(F32)).astype(BF16)


def _lru_core(y, gate, conv_w, conv_b, w_a, b_a, w_i, b_i, lam, *, B, S, rows=512):
    T, W = y.shape
    rows = min(rows, S)
    ns = S // rows
    blk = pl.BlockSpec((rows, LRU_BLOCK), lambda n, b, s: (b * ns + s, n))
    vec = pl.BlockSpec((1, LRU_BLOCK), lambda n, b, s: (0, n))
    wblk = pl.BlockSpec((None, LRU_BLOCK, LRU_BLOCK), lambda n, b, s: (n, 0, 0))
    return pl.pallas_call(
        functools.partial(_lru_kernel, rows=rows),
        grid=(LRU_NBLK, B, ns),
        in_specs=[blk, blk,
                  pl.BlockSpec((CONV_W, LRU_BLOCK), lambda n, b, s: (0, n)), vec,
                  wblk, vec, wblk, vec, vec],
        out_specs=blk,
        out_shape=jax.ShapeDtypeStruct((T, W), BF16),
        scratch_shapes=[pltpu.VMEM((8, LRU_BLOCK), F32), pltpu.VMEM((8, LRU_BLOCK), F32)],
        compiler_params=_cparams(("parallel", "parallel", "arbitrary")),
        name="lru_core",
    )(y, gate, conv_w, conv_b.reshape(1, W), w_a, b_a.reshape(1, W), w_i, b_i.reshape(1, W),
      lam.reshape(1, W))


def _rglru_layer(x32, xb, p, ln_g, ln_b, *, B, S):
    gate = _proj_nn(xb, p["w_gate"], out_dtype=BF16, act="gelu")
    y = _proj_nn(xb, p["w_y"], out_dtype=F32)
    hg = _lru_core(y, gate, p["conv_w"], p["conv_b"], p["w_a"], p["b_a"], p["w_i"], p["b_i"],
                   p["lam"], B=B, S=S)
    return _proj_ln(hg, p["w_out"], x32, ln_g, ln_b)


def _router_kernel(x_ref, wt_ref, b_ref, ids_ref, wts_ref, pos_ref, cnt_ref, carry_ref, *, tm):
    @pl.when(pl.program_id(0) == 0)
    def _():
        carry_ref[...] = jnp.zeros_like(carry_ref)

    logits = lax.dot_general(wt_ref[...], x_ref[...], (((1,), (1,)), ((), ())),
                             precision=lax.Precision.HIGHEST,
                             preferred_element_type=F32) + b_ref[...]
    m = jnp.max(logits, axis=0, keepdims=True)
    e = jnp.exp(logits - m)
    p = e / jnp.sum(e, axis=0, keepdims=True)
    rows = [p[k:k + 1, :] for k in range(N_EXPERTS)]

    scores = []
    for g in range(N_GROUPS):
        a, b, c, d = rows[4 * g:4 * g + 4]
        hi1, lo1, hi2, lo2 = jnp.maximum(a, b), jnp.minimum(a, b), jnp.maximum(c, d), jnp.minimum(c, d)
        scores.append(jnp.maximum(hi1, hi2) + jnp.maximum(jnp.minimum(hi1, hi2), jnp.maximum(lo1, lo2)))
    best, gsel = scores[0], jnp.zeros((1, tm), jnp.int32)
    for g in range(1, N_GROUPS):
        upd = scores[g] > best
        gsel = jnp.where(upd, g, gsel)
        best = jnp.where(upd, scores[g], best)

    cand = []
    for k in range(EXPERTS_PER_GROUP):
        c = rows[k]
        for g in range(1, N_GROUPS):
            c = jnp.where(gsel == g, rows[4 * g + k], c)
        cand.append(c)
    v1, i1 = cand[0], jnp.zeros((1, tm), jnp.int32)
    for k in range(1, EXPERTS_PER_GROUP):
        upd = cand[k] > v1
        i1 = jnp.where(upd, k, i1)
        v1 = jnp.where(upd, cand[k], v1)
    v2, i2 = jnp.full((1, tm), -1.0, F32), jnp.zeros((1, tm), jnp.int32)
    for k in range(EXPERTS_PER_GROUP):
        c = jnp.where(i1 == k, -1.0, cand[k])
        upd = c > v2
        i2 = jnp.where(upd, k, i2)
        v2 = jnp.where(upd, c, v2)
    e0 = gsel * EXPERTS_PER_GROUP + i1
    e1 = gsel * EXPERTS_PER_GROUP + i2
    tot = v1 + v2

    eio = lax.broadcasted_iota(jnp.int32, (N_EXPERTS, tm), 0)
    hit0, hit1 = eio == e0, eio == e1
    onehot = jnp.where(hit0 | hit1, 1.0, 0.0)
    before = (lax.broadcasted_iota(jnp.int32, (tm, tm), 0)
              < lax.broadcasted_iota(jnp.int32, (tm, tm), 1))
    rank = jnp.dot(onehot.astype(BF16), jnp.where(before, 1.0, 0.0).astype(BF16),
                   preferred_element_type=F32) + carry_ref[:, 0:1]
    pos0 = jnp.sum(jnp.where(hit0, rank, 0.0), axis=0, keepdims=True)
    pos1 = jnp.sum(jnp.where(hit1, rank, 0.0), axis=0, keepdims=True)
    carry_ref[...] = carry_ref[...] + jnp.sum(onehot, axis=1, keepdims=True)
    cnt_ref[...] = carry_ref[...]

    zi = jnp.zeros((6, tm), jnp.int32)
    ids_ref[...] = jnp.concatenate([e0, e1, zi], axis=0)
    pos_ref[...] = jnp.concatenate([pos0.astype(jnp.int32), pos1.astype(jnp.int32), zi], axis=0)
    wts_ref[...] = jnp.concatenate([v1 / tot, v2 / tot, jnp.zeros((6, tm), F32)], axis=0)


def _router(x32, router_wt, router_b, *, tm=512):
    T, D = x32.shape
    tm = min(tm, T)
    tok = pl.BlockSpec((8, tm), lambda i: (0, i))
    return pl.pallas_call(
        functools.partial(_router_kernel, tm=tm),
        grid=(T // tm,),
        in_specs=[pl.BlockSpec((tm, D), lambda i: (i, 0)),
                  pl.BlockSpec((N_EXPERTS, D), lambda i: (0, 0)),
                  pl.BlockSpec((N_EXPERTS, 1), lambda i: (0, 0))],
        out_specs=[tok, tok, tok, pl.BlockSpec((N_EXPERTS, LANE), lambda i: (0, 0))],
        out_shape=[jax.ShapeDtypeStruct((8, T), jnp.int32), jax.ShapeDtypeStruct((8, T), F32),
                   jax.ShapeDtypeStruct((8, T), jnp.int32),
                   jax.ShapeDtypeStruct((N_EXPERTS, LANE), F32)],
        scratch_shapes=[pltpu.VMEM((N_EXPERTS, LANE), F32)],
        compiler_params=_cparams(("arbitrary",)),
        name="moe_router",
    )(x32, router_wt, router_b.reshape(N_EXPERTS, 1))


def _row_copy(src, src_row, dst, dst_row, sem, n=1):
    return pltpu.make_async_copy(src.at[pl.ds(src_row, n)], dst.at[pl.ds(dst_row, n)], sem)


def _row_tokens_kernel(d0_ref, d1_ref, tok_ref, *, n_tok):
    def clear(r, c):
        tok_ref[r] = 0
        return c

    def put(t, c):
        tok_ref[d0_ref[t]] = t
        tok_ref[d1_ref[t]] = t
        return c

    lax.fori_loop(0, tok_ref.shape[0], clear, 0, unroll=8)
    lax.fori_loop(0, n_tok, put, 0, unroll=8)


def _row_tokens(d0, d1, n_rows):
    return pl.pallas_call(
        functools.partial(_row_tokens_kernel, n_tok=d0.shape[0]),
        grid_spec=pltpu.PrefetchScalarGridSpec(
            num_scalar_prefetch=2, grid=(1,), in_specs=[],
            out_specs=pl.BlockSpec(memory_space=pltpu.SMEM)),
        out_shape=jax.ShapeDtypeStruct((n_rows,), jnp.int32),
        compiler_params=pltpu.CompilerParams(dimension_semantics=("arbitrary",)),
        name="moe_row_tokens",
    )(d0, d1)


def _ffn_kernel(be_ref, nact_ref, tok_ref, x_hbm, wg_ref, wu_ref, wd_ref, y_ref, xbuf, sems):
    i = pl.program_id(0)
    n_act = nact_ref[0]
    R = EXPERT_ROWS

    def gather(block, slot):
        def issue(r, c):
            _row_copy(x_hbm, tok_ref[block * R + r], xbuf.at[slot], r, sems.at[slot]).start()
            return c
        lax.fori_loop(0, R, issue, 0, unroll=8)

    def run(slot):
        @pl.when(i + 1 < n_act)
        def _():
            gather(i + 1, 1 - slot)

        def drain(r, c):
            _row_copy(x_hbm, 0, xbuf.at[slot], 0, sems.at[slot]).wait()
            return c
        lax.fori_loop(0, R, drain, 0, unroll=8)
        xb = xbuf[slot].astype(BF16)
        g = jnp.dot(xb, wg_ref[...], preferred_element_type=F32)
        u = jnp.dot(xb, wu_ref[...], preferred_element_type=F32)
        h = (jax.nn.silu(g) * u).astype(BF16)
        y_ref[...] = jnp.dot(h, wd_ref[...], preferred_element_type=F32)

    @pl.when(i == 0)
    def _():
        gather(0, 0)

    for slot in range(2):
        @pl.when((i < n_act) & (i % 2 == slot))
        def _():
            run(slot)

    @pl.when(i >= n_act)
    def _():
        y_ref[...] = jnp.zeros_like(y_ref)


def _expert_ffn(x32, row_tok, blk_e, n_act, w_gate, w_up, w_down):
    D = x32.shape[1]
    R = EXPERT_ROWS
    n_rows = row_tok.shape[0]
    wmap = lambda i, be, na, tok: (be[i], 0, 0)
    return pl.pallas_call(
        _ffn_kernel,
        grid_spec=pltpu.PrefetchScalarGridSpec(
            num_scalar_prefetch=3, grid=(n_rows // R,),
            in_specs=[pl.BlockSpec(memory_space=pl.ANY),
                      pl.BlockSpec((None, D, D_EXPERT), wmap),
                      pl.BlockSpec((None, D, D_EXPERT), wmap),
                      pl.BlockSpec((None, D_EXPERT, D), wmap)],
            out_specs=pl.BlockSpec((R, D), lambda i, be, na, tok: (i, 0)),
            scratch_shapes=[pltpu.VMEM((2, R, D), F32), pltpu.SemaphoreType.DMA((2,))]),
        out_shape=jax.ShapeDtypeStruct((n_rows, D), F32),
        compiler_params=_cparams(("arbitrary",)),
        name="moe_ffn",
    )(blk_e, n_act, row_tok, x32, w_gate, w_up, w_down)


def _combine_kernel(d0_ref, d1_ref, res_ref, w_ref, g_ref, b_ref, y_hbm, of_ref, ob_ref,
                    buf, sem, *, tm):
    i = pl.program_id(0)

    def issue(r, c):
        t = i * tm + r
        _row_copy(y_hbm, d0_ref[t], buf.at[0], r, sem).start()
        _row_copy(y_hbm, d1_ref[t], buf.at[1], r, sem).start()
        return c

    def drain(r, c):
        _row_copy(y_hbm, 0, buf.at[0], 0, sem).wait()
        _row_copy(y_hbm, 0, buf.at[1], 0, sem).wait()
        return c

    lax.fori_loop(0, tm, issue, 0)
    lax.fori_loop(0, tm, drain, 0)
    w = w_ref[...]
    z = ALPHA * res_ref[...] + (w[:, 0:1] * buf[0] + w[:, 1:2] * buf[1])
    y = _layer_norm_rows(z, g_ref[...], b_ref[...])
    of_ref[...] = y
    ob_ref[...] = y.astype(BF16)


def _combine_ln(y, d0, d1, wts, res, g, b, *, tm=256):
    T, D = res.shape
    tm = min(tm, T)
    row = pl.BlockSpec((tm, D), lambda i, a, c: (i, 0))
    vec = pl.BlockSpec((1, D), lambda i, a, c: (0, 0))
    return pl.pallas_call(
        functools.partial(_combine_kernel, tm=tm),
        grid_spec=pltpu.PrefetchScalarGridSpec(
            num_scalar_prefetch=2, grid=(T // tm,),
            in_specs=[row, pl.BlockSpec((tm, 2), lambda i, a, c: (i, 0)), vec, vec,
                      pl.BlockSpec(memory_space=pl.ANY)],
            out_specs=[row, row],
            scratch_shapes=[pltpu.VMEM((2, tm, D), F32), pltpu.SemaphoreType.DMA(())]),
        out_shape=[jax.ShapeDtypeStruct((T, D), F32), jax.ShapeDtypeStruct((T, D), BF16)],
        compiler_params=_cparams(("arbitrary",)),
        name="moe_combine_ln",
    )(d0, d1, res, wts, g.reshape(1, D), b.reshape(1, D), y)


def _moe_layer(x32, router_wt, router_b, w_gate, w_up, w_down, ln_g, ln_b):
    T = x32.shape[0]
    R = EXPERT_ROWS
    ids, wts, pos, cnt = _router(x32, router_wt, router_b)
    counts = cnt[:, 0].astype(jnp.int32)
    padded = (counts + R - 1) // R * R
    p_end = jnp.cumsum(padded)
    offs = p_end - padded
    d0 = offs[ids[0]] + pos[0]
    d1 = offs[ids[1]] + pos[1]
    n_blocks = 2 * T // R + N_EXPERTS
    n_act = p_end[-1] // R
    step = jnp.arange(n_blocks, dtype=jnp.int32)
    live = step < n_act
    src = jnp.where(live, step, n_act - 1)
    blk_e = jnp.minimum(jnp.searchsorted(p_end // R, src, side="right"), N_EXPERTS - 1).astype(jnp.int32)
    row_tok = _row_tokens(d0, d1, n_blocks * R)
    y = _expert_ffn(x32, row_tok, blk_e, n_act.reshape(1).astype(jnp.int32), w_gate, w_up, w_down)
    return _combine_ln(y, d0, d1, wts[:2].T, x32, ln_g, ln_b)


def _t5_bucket(dist):
    n = jnp.maximum(dist, 0)
    max_exact = NUM_BUCKETS // 2
    nf = jnp.maximum(n, 1).astype(F32)
    large = max_exact + (jnp.log(nf / max_exact) / math.log(MAX_DISTANCE / max_exact)
                         * (NUM_BUCKETS - max_exact)).astype(jnp.int32)
    large = jnp.minimum(large, NUM_BUCKETS - 1)
    return jnp.where(n < max_exact, n, large)


def _stack_heads(tbl):
    parts = tbl.reshape((N_KV, GROUP) + tbl.shape[1:])
    return jnp.concatenate([parts[:, g] for g in range(GROUP)], axis=-1)


def _bias_tables(rel_bias, S):
    far = 16 * SEL_CHUNK
    bvec = rel_bias[_t5_bucket(jnp.arange(max(S, far + 1)))].T
    last = rel_bias[NUM_BUCKETS - 1]
    qi = np.arange(Q_BLOCK)

    def lookup(dist, shift=None):
        ok = dist >= 0
        val = bvec[:, np.clip(dist, 0, bvec.shape[1] - 1)]
        if shift is not None:
            val = val - shift.reshape((N_HEADS,) + (1,) * dist.ndim)
        return jnp.where(ok, val * LOG2E, NEG)

    ci = np.arange(CMP_TILE)
    delta = np.arange(CMP_NEAR)
    dist_c = (qi[None, None, :] - (CMP_LEN - 1) - CMP_STRIDE * ci[None, :, None]
              + Q_BLOCK * delta[:, None, None])
    near_c = lookup(dist_c)
    tc = jnp.concatenate([jnp.full((N_HEADS, 1, CMP_TILE, Q_BLOCK), NEG, F32), near_c,
                          jnp.broadcast_to(last[:, None, None, None] * LOG2E,
                                           (N_HEADS, 1, CMP_TILE, Q_BLOCK))],
                         axis=1)
    sub = np.arange(SEL_CHUNK // Q_BLOCK)
    near = np.arange(SEL_NEAR)
    kk = np.arange(SEL_CHUNK)
    dist_s = (Q_BLOCK * sub[:, None, None, None] + qi[None, None, None, :]
              + SEL_CHUNK * (SEL_NEAR - 1 - near)[None, :, None, None] - kk[None, None, :, None])
    ts = lookup(dist_s, shift=last)
    kj = np.arange(WINDOW + Q_BLOCK)
    dw = qi[None, :] + WINDOW - kj[:, None]
    tw = jnp.where((dw < WINDOW)[None], lookup(dw), NEG)
    return _stack_heads(tc), _stack_heads(ts), _stack_heads(tw)


def _compress_kernel(x_ref, pe_ref, w1_ref, w2_ref, o_ref, *, transposed):
    x = x_ref[...]
    pe = pe_ref[...]
    half = CMP_STRIDE * HEAD_DIM
    a = jnp.dot((x + pe[0:1]).astype(BF16), w1_ref[:half], preferred_element_type=F32)
    b = jnp.dot((x + pe[1:2]).astype(BF16), w1_ref[half:], preferred_element_type=F32)
    n = x.shape[0]
    pre = a + pltpu.roll(b, n - 1, 0)
    rowi = lax.broadcasted_iota(jnp.int32, pre.shape, 0)
    hid = jnp.where(rowi < n - 1, jax.nn.gelu(pre), 0.0).astype(BF16)
    if transposed:
        out = lax.dot_general(w2_ref[...], hid, (((1,), (1,)), ((), ())), preferred_element_type=F32)
    else:
        out = jnp.dot(hid, w2_ref[...], preferred_element_type=F32)
    o_ref[...] = out.astype(BF16)


def _compress(slabs, pe, w1, w2, *, B, S, transposed):
    n = S // CMP_STRIDE
    wide = CMP_STRIDE * HEAD_DIM
    out_shape = (N_KV, B, HEAD_DIM, n) if transposed else (N_KV, B, n, HEAD_DIM)
    w2_in = w2.T if transposed else w2
    return pl.pallas_call(
        functools.partial(_compress_kernel, transposed=transposed),
        grid=(N_KV, B),
        in_specs=[pl.BlockSpec((None, None, n, wide), lambda h, b: (h, b, 0, 0)),
                  pl.BlockSpec((2, wide), lambda h, b: (0, 0)),
                  pl.BlockSpec((2 * wide, HEAD_DIM), lambda h, b: (0, 0)),
                  pl.BlockSpec((HEAD_DIM, HEAD_DIM), lambda h, b: (0, 0))],
        out_specs=pl.BlockSpec((None, None) + out_shape[2:], lambda h, b: (h, b, 0, 0)),
        out_shape=jax.ShapeDtypeStruct(out_shape, BF16),
        compiler_params=_cparams(("parallel", "parallel")),
        name="nsa_compress",
    )(slabs, pe.reshape(2, wide), w1, w2_in)


def _query_cols(q_ref):
    return jnp.concatenate([q_ref[g * HEAD_DIM:(g + 1) * HEAD_DIM, :] for g in range(GROUP)], axis=1)


def _cmp_kernel(q_ref, kc_ref, vct_ref, tc_ref, ov_ref, oc_ref, mb_ref, *, n_tiles, n_sel):
    ib = pl.program_id(2)
    qt = _query_cols(q_ref)
    s = jnp.dot(kc_ref[...], qt, preferred_element_type=F32)
    tiles = []
    for cb in range(n_tiles):
        idx = jnp.clip(ib - (CMP_TILE * CMP_STRIDE // Q_BLOCK) * cb + 1, 0, CMP_NEAR + 1)
        tiles.append(s[cb * CMP_TILE:(cb + 1) * CMP_TILE] + tc_ref[idx])
    s = jnp.concatenate(tiles, axis=0)
    valid = s > 0.5 * NEG
    m = jnp.max(s, axis=0, keepdims=True)
    p = jnp.where(valid, jnp.exp2(s - m), 0.0)
    p = p / jnp.maximum(jnp.sum(p, axis=0, keepdims=True), 1e-30)
    oc_ref[...] = jnp.dot(vct_ref[...], p.astype(BF16), preferred_element_type=F32)

    psum = p[:, 0:Q_BLOCK]
    for g in range(1, GROUP):
        psum = psum + p[:, g * Q_BLOCK:(g + 1) * Q_BLOCK]
    p_hi = psum.astype(BF16)
    p_lo = (psum - p_hi.astype(F32)).astype(BF16)
    imp = (jnp.dot(ov_ref[...], p_hi, preferred_element_type=F32)
           + jnp.dot(ov_ref[...], p_lo, preferred_element_type=F32))
    blk = lax.broadcasted_iota(jnp.int32, (n_sel, Q_BLOCK), 0)
    t = ib * Q_BLOCK + lax.broadcasted_iota(jnp.int32, (n_sel, Q_BLOCK), 1)
    cur = jnp.right_shift(t, int(math.log2(SEL_LEN)))
    ok = blk * SEL_LEN <= t
    forced = ok & ((blk == 0) | (blk == cur) | (blk == cur - 1))
    score = jnp.where(forced, BIG, jnp.where(ok, imp, NEG))
    taken = jnp.zeros((n_sel, Q_BLOCK), jnp.bool_)
    for _ in range(SEL_TOP):
        top = jnp.max(score, axis=0, keepdims=True)
        first = jnp.min(jnp.where(score == top, blk, n_sel), axis=0, keepdims=True)
        pick = blk == first
        taken = taken | pick
        score = jnp.where(pick, -jnp.inf, score)
    mb_ref[...] = jnp.where(taken & ok, 0.0, NEG).astype(BF16)


def _cmp_attention(qt, kc, vct, tc, overlap_t, *, B, S):
    nq = S // Q_BLOCK
    n_cmp = S // CMP_STRIDE
    n_sel = S // SEL_LEN
    n_tiles = n_cmp // CMP_TILE
    return pl.pallas_call(
        functools.partial(_cmp_kernel, n_tiles=n_tiles, n_sel=n_sel),
        grid=(N_KV, B, nq),
        in_specs=[pl.BlockSpec((GROUP * HEAD_DIM, Q_BLOCK), lambda h, b, i: (h, b * nq + i)),
                  pl.BlockSpec((None, None, n_cmp, HEAD_DIM), lambda h, b, i: (h, b, 0, 0)),
                  pl.BlockSpec((None, None, HEAD_DIM, n_cmp), lambda h, b, i: (h, b, 0, 0)),
                  pl.BlockSpec((None, CMP_NEAR + 2, CMP_TILE, QG), lambda h, b, i: (h, 0, 0, 0)),
                  pl.BlockSpec((n_sel, n_cmp), lambda h, b, i: (0, 0))],
        out_specs=[pl.BlockSpec((None, None, HEAD_DIM, QG), lambda h, b, i: (h, b * nq + i, 0, 0)),
                   pl.BlockSpec((None, None, n_sel, Q_BLOCK), lambda h, b, i: (h, b, 0, i))],
        out_shape=[jax.ShapeDtypeStruct((N_KV, B * nq, HEAD_DIM, QG), F32),
                   jax.ShapeDtypeStruct((N_KV, B, n_sel, S), BF16)],
        compiler_params=_cparams(("parallel", "parallel", "parallel")),
        name="nsa_cmp_select",
    )(qt, kc, vct, tc, overlap_t)


def _sel_kernel(q_ref, mb_ref, ka_ref, vt_ref, ts_ref, o_ref, m_ref, l_ref, acc_ref):
    ib = pl.program_id(2)
    per = SEL_CHUNK // Q_BLOCK
    cq = ib // per
    sub = ib % per
    qt = _query_cols(q_ref)
    blocks_per_chunk = SEL_CHUNK // SEL_LEN
    chunks_per_group = LANE // blocks_per_chunk

    m_ref[...] = jnp.full(m_ref.shape, NEG, F32)
    l_ref[...] = jnp.zeros_like(l_ref)
    acc_ref[...] = jnp.zeros_like(acc_ref)

    def masked_q(c):
        grp = c // chunks_per_group
        mb = mb_ref[pl.ds(pl.multiple_of(grp * LANE, LANE), LANE), :]
        return jnp.concatenate([qt, jnp.concatenate([mb] * GROUP, axis=1)], axis=0)

    def update(scores, values):
        m_old = m_ref[0:1, :]
        m_new = m_old
        for s in scores:
            m_new = jnp.maximum(m_new, jnp.max(s, axis=0, keepdims=True))
        alpha = jnp.exp2(m_old - m_new)
        l_new = alpha * l_ref[0:1, :]
        acc = alpha * acc_ref[...]
        for s, v in zip(scores, values):
            p = jnp.exp2(s - m_new)
            l_new = l_new + jnp.sum(p, axis=0, keepdims=True)
            acc = acc + jnp.dot(v, p.astype(BF16), preferred_element_type=F32)
        l_ref[...] = jnp.broadcast_to(l_new, l_ref.shape)
        acc_ref[...] = acc
        m_ref[...] = jnp.broadcast_to(m_new, m_ref.shape)

    def far_many(j, carry):
        c0 = pl.multiple_of(j * SEL_FAR, SEL_FAR)
        keys = ka_ref[pl.ds(c0, SEL_FAR)].reshape(SEL_FAR * SEL_CHUNK, 2 * HEAD_DIM)
        s = jnp.dot(keys, masked_q(c0), preferred_element_type=F32)
        v = jnp.concatenate([vt_ref[c0 + k] for k in range(SEL_FAR)], axis=1)
        update([s], [v])
        return carry

    def far_one(c, carry):
        update([jnp.dot(ka_ref[c], masked_q(c), preferred_element_type=F32)], [vt_ref[c]])
        return carry

    n_far = jnp.maximum(cq - (SEL_NEAR - 1), 0)
    lax.fori_loop(0, n_far // SEL_FAR, far_many, 0)
    lax.fori_loop(n_far // SEL_FAR * SEL_FAR, n_far, far_one, 0)

    scores, values = [], []
    for i in range(SEL_NEAR):
        c = cq - (SEL_NEAR - 1) + i
        cc = jnp.maximum(c, 0)
        s = jnp.dot(ka_ref[cc], masked_q(cc), preferred_element_type=F32) + ts_ref[sub, i]
        scores.append(s + jnp.where(c >= 0, 0.0, NEG))
        values.append(vt_ref[cc])
    update(scores, values)
    o_ref[...] = acc_ref[...] / jnp.maximum(l_ref[0:1, :], 1e-30)


def _sel_attention(qt, mbt, k_aug, vt, ts, *, B, S):
    nq = S // Q_BLOCK
    n_sel = mbt.shape[2]
    nc = S // SEL_CHUNK
    return pl.pallas_call(
        _sel_kernel,
        grid=(N_KV, B, nq),
        in_specs=[pl.BlockSpec((GROUP * HEAD_DIM, Q_BLOCK), lambda h, b, i: (h, b * nq + i)),
                  pl.BlockSpec((None, None, n_sel, Q_BLOCK), lambda h, b, i: (h, b, 0, i)),
                  pl.BlockSpec((None, None, nc, SEL_CHUNK, 2 * HEAD_DIM), lambda h, b, i: (h, b, 0, 0, 0)),
                  pl.BlockSpec((None, None, nc, HEAD_DIM, SEL_CHUNK), lambda h, b, i: (h, b, 0, 0, 0)),
                  pl.BlockSpec((None, SEL_CHUNK // Q_BLOCK, SEL_NEAR, SEL_CHUNK, QG),
                               lambda h, b, i: (h, 0, 0, 0, 0))],
        out_specs=pl.BlockSpec((None, None, HEAD_DIM, QG), lambda h, b, i: (h, b * nq + i, 0, 0)),
        out_shape=jax.ShapeDtypeStruct((N_KV, B * nq, HEAD_DIM, QG), F32),
        scratch_shapes=[pltpu.VMEM((8, QG), F32), pltpu.VMEM((8, QG), F32),
                        pltpu.VMEM((HEAD_DIM, QG), F32)],
        compiler_params=_cparams(("parallel", "parallel", "arbitrary")),
        name="nsa_selected",
    )(qt, mbt, k_aug, vt, ts)


def _win_kernel(*refs):
    nb = WINDOW // Q_BLOCK + 1
    q_ref = refs[0]
    k_refs = refs[1:1 + nb]
    v_refs = refs[1 + nb:1 + 2 * nb]
    tw_ref, gate_ref, oc_ref, os_ref, o_ref = refs[1 + 2 * nb:]
    ib = pl.program_id(2)
    qt = _query_cols(q_ref)
    parts = []
    for i in range(nb):
        s = jnp.dot(k_refs[i][...], qt, preferred_element_type=F32)
        s = s + tw_ref[i * Q_BLOCK:(i + 1) * Q_BLOCK, :]
        parts.append(jnp.where(ib - (nb - 1) + i >= 0, s, NEG))
    s = jnp.concatenate(parts, axis=0)
    m = jnp.max(s, axis=0, keepdims=True)
    p = jnp.exp2(s - m)
    l = jnp.sum(p, axis=0, keepdims=True)
    pb = p.astype(BF16)
    ow = jnp.dot(v_refs[0][...], pb[0:Q_BLOCK], preferred_element_type=F32)
    for i in range(1, nb):
        ow = ow + jnp.dot(v_refs[i][...], pb[i * Q_BLOCK:(i + 1) * Q_BLOCK], preferred_element_type=F32)
    ow = ow / jnp.maximum(l, 1e-30)

    def gate(j):
        return jnp.concatenate([gate_ref[j, g:g + 1, :] for g in range(GROUP)], axis=1)

    o = gate(0) * oc_ref[...] + gate(1) * os_ref[...] + gate(2) * ow
    for g in range(GROUP):
        o_ref[:, g * HEAD_DIM:(g + 1) * HEAD_DIM] = o[:, g * Q_BLOCK:(g + 1) * Q_BLOCK].T.astype(BF16)


def _win_attention(qt, kw, vwt, tw, gates_t, oc, osel, *, B, S):
    nq = S // Q_BLOCK
    nb = WINDOW // Q_BLOCK + 1

    def kmap(i_off):
        return lambda h, b, i: (h, b * nq + jnp.maximum(i - (nb - 1) + i_off, 0), 0)

    def vmap_(i_off):
        return lambda h, b, i: (h, 0, b * nq + jnp.maximum(i - (nb - 1) + i_off, 0))

    blk = pl.BlockSpec((None, None, HEAD_DIM, QG), lambda h, b, i: (h, b * nq + i, 0, 0))
    return pl.pallas_call(
        _win_kernel,
        grid=(N_KV, B, nq),
        in_specs=([pl.BlockSpec((GROUP * HEAD_DIM, Q_BLOCK), lambda h, b, i: (h, b * nq + i))]
                  + [pl.BlockSpec((None, Q_BLOCK, HEAD_DIM), kmap(j)) for j in range(nb)]
                  + [pl.BlockSpec((None, HEAD_DIM, Q_BLOCK), vmap_(j)) for j in range(nb)]
                  + [pl.BlockSpec((None, WINDOW + Q_BLOCK, QG), lambda h, b, i: (h, 0, 0)),
                     pl.BlockSpec((3, None, GROUP, Q_BLOCK), lambda h, b, i: (0, h, 0, b * nq + i)),
                     blk, blk]),
        out_specs=pl.BlockSpec((Q_BLOCK, GROUP * HEAD_DIM), lambda h, b, i: (b * nq + i, h)),
        out_shape=jax.ShapeDtypeStruct((B * S, N_HEADS * HEAD_DIM), BF16),
        compiler_params=_cparams(("parallel", "parallel", "parallel")),
        name="nsa_window_merge",
    )(qt, *([kw] * nb), *([vwt] * nb), tw, gates_t, oc, osel)


def _nsa_layer(x32, xb, p, tables, consts, ln_g, ln_b, *, B, S):
    T = B * S
    tc, ts, tw = tables
    overlap_t, sel_onehot = consts
    kd = N_KV * HEAD_DIM
    qv = _proj_nt(p["w_qv_t"], xb, out_dtype=BF16, scale=HEAD_DIM ** -0.5 * LOG2E,
                  n_scaled=N_HEADS * HEAD_DIM // 512)
    qt = qv[:N_HEADS * HEAD_DIM]
    k2 = _proj_nn(xb, p["w_k"], out_dtype=BF16, head_major=True)
    cmp = _proj_nn(xb, p["w_cmp"], out_dtype=F32, head_major=True)
    gates_t = _proj_nt(p["w_gate_t"], xb, out_dtype=F32, act="sigmoid")
    gates_t = gates_t.reshape(3, N_KV, GROUP, T)

    cmp = cmp.reshape(2, N_KV, B, S // CMP_STRIDE, CMP_STRIDE * HEAD_DIM)
    kc = _compress(cmp[0], p["pe_k"], p["w1_k"], p["w2_k"], B=B, S=S, transposed=False)
    vct = _compress(cmp[1], p["pe_v"], p["w1_v"], p["w2_v"], B=B, S=S, transposed=True)
    oc, mbt = _cmp_attention(qt, kc, vct, tc, overlap_t, B=B, S=S)
    if mbt.shape[2] % LANE:
        mbt = jnp.pad(mbt, ((0, 0), (0, 0), (0, -mbt.shape[2] % LANE), (0, 0)), constant_values=NEG)

    nc = S // SEL_CHUNK
    k_sel = k2[:N_KV].reshape(N_KV, B, nc, SEL_CHUNK, HEAD_DIM)
    k_aug = jnp.concatenate([k_sel, jnp.broadcast_to(sel_onehot, k_sel.shape)], axis=-1)
    v_sel_t = qv[N_HEADS * HEAD_DIM:N_HEADS * HEAD_DIM + kd].reshape(N_KV, HEAD_DIM, B, nc, SEL_CHUNK)
    v_sel_t = v_sel_t.transpose(0, 2, 3, 1, 4)
    osel = _sel_attention(qt, mbt, k_aug, v_sel_t, ts, B=B, S=S)

    v_win_t = qv[N_HEADS * HEAD_DIM + kd:].reshape(N_KV, HEAD_DIM, T)
    attn = _win_attention(qt, k2[N_KV:], v_win_t, tw, gates_t, oc, osel, B=B, S=S)
    return _proj_ln(attn, p["w_out"], x32, ln_g, ln_b)


def _nsa_params(w_in, pe_k, w1_k, w2_k, pe_v, w1_v, w2_v, w_out):
    D = w_in.shape[0]
    qd, kd = N_HEADS * HEAD_DIM, N_KV * HEAD_DIM
    w_kv = w_in[:, qd:qd + 6 * kd].reshape(D, 6, kd)
    gcol = np.array([[[(kv * GROUP + g) * 3 + j for g in range(GROUP)] for kv in range(N_KV)]
                     for j in range(3)]).reshape(-1)
    return dict(
        w_qv_t=jnp.concatenate([w_in[:, :qd], w_kv[:, 3], w_kv[:, 5]], axis=1).T.astype(BF16),
        w_k=jnp.concatenate([w_kv[:, 2], w_kv[:, 4]], axis=1).astype(BF16),
        w_cmp=jnp.concatenate([w_kv[:, 0], w_kv[:, 1]], axis=1).astype(BF16),
        w_gate_t=w_in[:, qd + 6 * kd:][:, gcol].T.astype(BF16),
        pe_k=pe_k, w1_k=w1_k.astype(BF16), w2_k=w2_k.astype(BF16),
        pe_v=pe_v, w1_v=w1_v.astype(BF16), w2_v=w2_v.astype(BF16),
        w_out=w_out.astype(BF16))


def _nsa_constants(S):
    n_cmp, n_sel = S // CMP_STRIDE, S // SEL_LEN
    cj = np.arange(n_cmp) * CMP_STRIDE
    sb = np.arange(n_sel) * SEL_LEN
    ov = np.maximum(np.minimum(cj[None, :] + CMP_LEN, sb[:, None] + SEL_LEN)
                    - np.maximum(cj[None, :], sb[:, None]), 0).astype(np.float32) / CMP_LEN
    ov[:, n_cmp - 1] = 0.0
    key = np.arange(SEL_CHUNK)
    chunk = np.arange(S // SEL_CHUNK)
    blk = (chunk[:, None] * (SEL_CHUNK // SEL_LEN) + key[None, :] // SEL_LEN) % LANE
    onehot = (blk[:, :, None] == np.arange(LANE)[None, None, :]).astype(np.float32)
    return jnp.asarray(ov, BF16), jnp.asarray(onehot, BF16)


def kernel(x, rel_bias, router_w, router_b, ln1_g, ln1_b, ln2_g, ln2_b, lru_w_in, lru_conv_w,
           lru_conv_b, lru_w_a, lru_b_a, lru_w_i, lru_b_i, lru_lambda, lru_w_out, nsa_w_in,
           nsa_pe_k, nsa_w1_k, nsa_w2_k, nsa_pe_v, nsa_w1_v, nsa_w2_v, nsa_w_out,
           moe_w_gate, moe_w_up, moe_w_down):
    B, S, D = x.shape
    T = B * S
    assert D == D_MODEL and S % (CMP_TILE * CMP_STRIDE) == 0 and S // SEL_LEN >= SEL_TOP
    qd, kd = N_HEADS * HEAD_DIM, N_KV * HEAD_DIM
    tables = _bias_tables(rel_bias, S)
    consts = _nsa_constants(S)
    router_wt = router_w.T

    x32 = x.reshape(T, D)
    xb = x32.astype(BF16)
    for layer in range(DEPTH):
        j = layer // 2
        if layer % 2 == 0:
            w_in = lru_w_in[j].astype(BF16)
            p = dict(w_gate=w_in[:, :D], w_y=w_in[:, D:], conv_w=lru_conv_w[j], conv_b=lru_conv_b[j],
                     w_a=lru_w_a[j].astype(BF16), b_a=lru_b_a[j], w_i=lru_w_i[j].astype(BF16),
                     b_i=lru_b_i[j], lam=lru_lambda[j], w_out=lru_w_out[j].astype(BF16))
            x32, xb = _rglru_layer(x32, xb, p, ln1_g[layer], ln1_b[layer], B=B, S=S)
        else:
            p = _nsa_params(nsa_w_in[j], nsa_pe_k[j], nsa_w1_k[j], nsa_w2_k[j], nsa_pe_v[j],
                            nsa_w1_v[j], nsa_w2_v[j], nsa_w_out[j])
            x32, xb = _nsa_layer(x32, xb, p, tables, consts, ln1_g[layer], ln1_b[layer], B=B, S=S)
        x32, xb = _moe_layer(x32, router_wt, router_b, moe_w_gate[layer].astype(BF16),
                             moe_w_up[layer].astype(BF16), moe_w_down[layer].astype(BF16),
                             ln2_g[layer], ln2_b[layer])
    return x32.reshape(B, S, D)
```

```python
import functools
import math

import numpy as np
import jax
import jax.numpy as jnp
from jax import lax
from jax.experimental import pallas as pl
from jax.experimental.pallas import tpu as pltpu

F32 = jnp.float32
BF16 = jnp.bfloat16

D_MODEL = 2048
DEPTH = 4
LRU_BLOCK = 256
LRU_NBLK = D_MODEL // LRU_BLOCK
CONV_W = 4
LRU_C = 8.0
HEAD_DIM = 128
N_HEADS = 16
N_KV = 4
GROUP = 4
CMP_LEN = 32
CMP_STRIDE = 16
SEL_LEN = 64
SEL_TOP = 16
WINDOW = 512
Q_BLOCK = 128
NUM_BUCKETS = 32
MAX_DISTANCE = 1024
N_EXPERTS = 16
N_GROUPS = 4
EXPERTS_PER_GROUP = 4
D_EXPERT = 1024
ALPHA = (2 * DEPTH) ** 0.25
LN_EPS = 1e-5
NEG = -1e30
BIG = 1e9

LANE = 128
QG = GROUP * Q_BLOCK
EXPERT_ROWS = 256
SEL_CHUNK = 256
SEL_NEAR = 5
SEL_FAR = 4
V_ROWS = HEAD_DIM + 16
LOG2E = math.log2(math.e)
CMP_TILE = 128
CMP_NEAR = 23
VMEM_LIMIT = 56 << 20


def _cparams(sem, vmem=VMEM_LIMIT):
    return pltpu.CompilerParams(dimension_semantics=sem, vmem_limit_bytes=vmem)


def _act(x, act):
    if act == "gelu":
        return jax.nn.gelu(x)
    if act == "sigmoid":
        return jax.nn.sigmoid(x)
    return x


def _proj_nn_kernel(x_ref, w_ref, o_ref, *, act, head_major):
    acc = jnp.dot(x_ref[...], w_ref[...], preferred_element_type=F32)
    acc = _act(acc, act)
    if head_major:
        for h in range(o_ref.shape[0]):
            o_ref[h] = acc[:, h * LANE:(h + 1) * LANE].astype(o_ref.dtype)
    else:
        o_ref[...] = acc.astype(o_ref.dtype)


def _proj_nn(x, w, *, out_dtype, act=None, head_major=False, tm=1024, tn=1024):
    T, K = x.shape
    N = w.shape[1]
    tm, tn = min(tm, T), min(tn, N)
    if head_major:
        out_shape = (N // LANE, T, LANE)
        out_spec = pl.BlockSpec((tn // LANE, tm, LANE), lambda j, i: (j, i, 0))
    else:
        out_shape = (T, N)
        out_spec = pl.BlockSpec((tm, tn), lambda j, i: (i, j))
    return pl.pallas_call(
        functools.partial(_proj_nn_kernel, act=act, head_major=head_major),
        grid=(N // tn, T // tm),
        in_specs=[pl.BlockSpec((tm, K), lambda j, i: (i, 0)),
                  pl.BlockSpec((K, tn), lambda j, i: (0, j))],
        out_specs=out_spec,
        out_shape=jax.ShapeDtypeStruct(out_shape, out_dtype),
        compiler_params=_cparams(("parallel", "parallel")),
        name="proj_nn",
    )(x, w)


def _proj_nt_kernel(wt_ref, x_ref, o_ref, *, act, scale, n_scaled):
    acc = lax.dot_general(wt_ref[...], x_ref[...], (((1,), (1,)), ((), ())),
                          preferred_element_type=F32)
    if n_scaled:
        acc = acc * jnp.where(pl.program_id(0) < n_scaled, scale, 1.0).astype(F32)
    o_ref[...] = _act(acc, act).astype(o_ref.dtype)


def _proj_nt(wt, x, *, out_dtype, act=None, scale=1.0, n_scaled=0, tm=1024, tn=512):
    N, K = wt.shape
    T = x.shape[0]
    tm, tn = min(tm, T), min(tn, N)
    return pl.pallas_call(
        functools.partial(_proj_nt_kernel, act=act, scale=scale, n_scaled=n_scaled),
        grid=(N // tn, T // tm),
        in_specs=[pl.BlockSpec((tn, K), lambda j, i: (j, 0)),
                  pl.BlockSpec((tm, K), lambda j, i: (i, 0))],
        out_specs=pl.BlockSpec((tn, tm), lambda j, i: (j, i)),
        out_shape=jax.ShapeDtypeStruct((N, T), out_dtype),
        compiler_params=_cparams(("parallel", "parallel")),
        name="proj_nt",
    )(wt, x)


def _layer_norm_rows(z, g, b):
    mu = jnp.mean(z, axis=-1, keepdims=True)
    zc = z - mu
    var = jnp.mean(zc * zc, axis=-1, keepdims=True)
    return zc * lax.rsqrt(var + LN_EPS) * g + b


def _proj_ln_kernel(a_ref, w_ref, res_ref, g_ref, b_ref, of_ref, ob_ref):
    acc = jnp.dot(a_ref[...], w_ref[...], preferred_element_type=F32)
    y = _layer_norm_rows(ALPHA * res_ref[...] + acc, g_ref[...], b_ref[...])
    of_ref[...] = y
    ob_ref[...] = y.astype(BF16)


def _proj_ln(a, w, res, g, b, *, tm=256):
    T, K = a.shape
    D = w.shape[1]
    tm = min(tm, T)
    row = pl.BlockSpec((tm, D), lambda i: (i, 0))
    vec = pl.BlockSpec((1, D), lambda i: (0, 0))
    return pl.pallas_call(
        _proj_ln_kernel,
        grid=(T // tm,),
        in_specs=[pl.BlockSpec((tm, K), lambda i: (i, 0)),
                  pl.BlockSpec((K, D), lambda i: (0, 0)),
                  row, vec, vec],
        out_specs=[row, row],
        out_shape=[jax.ShapeDtypeStruct((T, D), F32), jax.ShapeDtypeStruct((T, D), BF16)],
        compiler_params=_cparams(("parallel",)),
        name="proj_ln",
    )(a, w, res, g.reshape(1, D), b.reshape(1, D))


def _lru_kernel(y_ref, gate_ref, cw_ref, cb_ref, wa_ref, ba_ref, wi_ref, bi_ref, lam_ref,
                o_ref, h_ref, tail_ref, *, rows):
    @pl.when(pl.program_id(2) == 0)
    def _():
        h_ref[...] = jnp.zeros_like(h_ref)
        tail_ref[...] = jnp.zeros_like(tail_ref)

    y = y_ref[...]
    tail = tail_ref[...]
    cw = cw_ref[...]
    row8 = lax.broadcasted_iota(jnp.int32, (8, LRU_BLOCK), 0)
    yc = cb_ref[...] + y * cw[CONV_W - 1:CONV_W, :]
    for d in range(1, CONV_W):
        ys = pltpu.roll(y, d, 0)
        first = jnp.where(row8 < d, pltpu.roll(tail, d, 0), ys[:8])
        ys = jnp.concatenate([first, ys[8:]], axis=0)
        yc = yc + ys * cw[CONV_W - 1 - d:CONV_W - d, :]
    tail_ref[...] = y[rows - 8:]

    yb = yc.astype(BF16)
    r = jax.nn.sigmoid(jnp.dot(yb, wa_ref[...], preferred_element_type=F32) + ba_ref[...])
    ig = jax.nn.sigmoid(jnp.dot(yb, wi_ref[...], preferred_element_type=F32) + bi_ref[...])
    z = -lam_ref[...]
    softplus = jnp.maximum(z, 0.0) + jnp.log1p(jnp.exp(-jnp.abs(z)))
    log_a = (-LRU_C * softplus) * r
    a = jnp.exp(log_a)
    h = jnp.sqrt(1.0 - a * a) * (ig * yc)

    rowi = lax.broadcasted_iota(jnp.int32, (rows, LRU_BLOCK), 0)
    s = 1
    while s < rows:
        if s < 8:
            keep = rowi >= s
            h = jnp.where(keep, h + a * pltpu.roll(h, s, 0), h)
            a = jnp.where(keep, a * pltpu.roll(a, s, 0), a)
        else:
            h = jnp.concatenate([h[:s], h[s:] + a[s:] * h[:rows - s]], axis=0)
            a = jnp.concatenate([a[:s], a[s:] * a[:rows - s]], axis=0)
        s *= 2
    h = h + a * h_ref[0:1, :]
    h_ref[...] = jnp.broadcast_to(h[rows - 1:rows, :], h_ref.shape)
    o_ref[...] = (h * gate_ref[...].astype(F32)).astype(BF16)


def _lru_core(y, gate, conv_w, conv_b, w_a, b_a, w_i, b_i, lam, *, B, S, rows=512):
    T, W = y.shape
    rows = min(rows, S)
    ns = S // rows
    blk = pl.BlockSpec((rows, LRU_BLOCK), lambda n, b, s: (b * ns + s, n))
    vec = pl.BlockSpec((1, LRU_BLOCK), lambda n, b, s: (0, n))
    wblk = pl.BlockSpec((None, LRU_BLOCK, LRU_BLOCK), lambda n, b, s: (n, 0, 0))
    return pl.pallas_call(
        functools.partial(_lru_kernel, rows=rows),
        grid=(LRU_NBLK, B, ns),
        in_specs=[blk, blk,
                  pl.BlockSpec((CONV_W, LRU_BLOCK), lambda n, b, s: (0, n)), vec,
                  wblk, vec, wblk, vec, vec],
        out_specs=blk,
        out_shape=jax.ShapeDtypeStruct((T, W), BF16),
        scratch_shapes=[pltpu.VMEM((8, LRU_BLOCK), F32), pltpu.VMEM((8, LRU_BLOCK), F32)],
        compiler_params=_cparams(("parallel", "parallel", "arbitrary")),
        name="lru_core",
    )(y, gate, conv_w, conv_b.reshape(1, W), w_a, b_a.reshape(1, W), w_i, b_i.reshape(1, W),
      lam.reshape(1, W))


def _rglru_layer(x32, xb, p, ln_g, ln_b, *, B, S):
    gate = _proj_nn(xb, p["w_gate"], out_dtype=BF16, act="gelu")
    y = _proj_nn(xb, p["w_y"], out_dtype=F32)
    hg = _lru_core(y, gate, p["conv_w"], p["conv_b"], p["w_a"], p["b_a"], p["w_i"], p["b_i"],
                   p["lam"], B=B, S=S)
    return _proj_ln(hg, p["w_out"], x32, ln_g, ln_b)


def _router_kernel(x_ref, wt_ref, b_ref, ids_ref, wts_ref, pos_ref, cnt_ref, carry_ref, *, tm):
    @pl.when(pl.program_id(0) == 0)
    def _():
        carry_ref[...] = jnp.zeros_like(carry_ref)

    logits = lax.dot_general(wt_ref[...], x_ref[...], (((1,), (1,)), ((), ())),
                             precision=lax.Precision.HIGHEST,
                             preferred_element_type=F32) + b_ref[...]
    m = jnp.max(logits, axis=0, keepdims=True)
    e = jnp.exp(logits - m)
    p = e / jnp.sum(e, axis=0, keepdims=True)
    rows = [p[k:k + 1, :] for k in range(N_EXPERTS)]

    scores = []
    for g in range(N_GROUPS):
        a, b, c, d = rows[4 * g:4 * g + 4]
        hi1, lo1, hi2, lo2 = jnp.maximum(a, b), jnp.minimum(a, b), jnp.maximum(c, d), jnp.minimum(c, d)
        scores.append(jnp.maximum(hi1, hi2) + jnp.maximum(jnp.minimum(hi1, hi2), jnp.maximum(lo1, lo2)))
    best, gsel = scores[0], jnp.zeros((1, tm), jnp.int32)
    for g in range(1, N_GROUPS):
        upd = scores[g] > best
        gsel = jnp.where(upd, g, gsel)
        best = jnp.where(upd, scores[g], best)

    cand = []
    for k in range(EXPERTS_PER_GROUP):
        c = rows[k]
        for g in range(1, N_GROUPS):
            c = jnp.where(gsel == g, rows[4 * g + k], c)
        cand.append(c)
    v1, i1 = cand[0], jnp.zeros((1, tm), jnp.int32)
    for k in range(1, EXPERTS_PER_GROUP):
        upd = cand[k] > v1
        i1 = jnp.where(upd, k, i1)
        v1 = jnp.where(upd, cand[k], v1)
    v2, i2 = jnp.full((1, tm), -1.0, F32), jnp.zeros((1, tm), jnp.int32)
    for k in range(EXPERTS_PER_GROUP):
        c = jnp.where(i1 == k, -1.0, cand[k])
        upd = c > v2
        i2 = jnp.where(upd, k, i2)
        v2 = jnp.where(upd, c, v2)
    e0 = gsel * EXPERTS_PER_GROUP + i1
    e1 = gsel * EXPERTS_PER_GROUP + i2
    tot = v1 + v2

    eio = lax.broadcasted_iota(jnp.int32, (N_EXPERTS, tm), 0)
    hit0, hit1 = eio == e0, eio == e1
    onehot = jnp.where(hit0 | hit1, 1.0, 0.0)
    before = (lax.broadcasted_iota(jnp.int32, (tm, tm), 0)
              < lax.broadcasted_iota(jnp.int32, (tm, tm), 1))
    rank = jnp.dot(onehot.astype(BF16), jnp.where(before, 1.0, 0.0).astype(BF16),
                   preferred_element_type=F32) + carry_ref[:, 0:1]
    pos0 = jnp.sum(jnp.where(hit0, rank, 0.0), axis=0, keepdims=True)
    pos1 = jnp.sum(jnp.where(hit1, rank, 0.0), axis=0, keepdims=True)
    carry_ref[...] = carry_ref[...] + jnp.sum(onehot, axis=1, keepdims=True)
    cnt_ref[...] = carry_ref[...]

    zi = jnp.zeros((6, tm), jnp.int32)
    ids_ref[...] = jnp.concatenate([e0, e1, zi], axis=0)
    pos_ref[...] = jnp.concatenate([pos0.astype(jnp.int32), pos1.astype(jnp.int32), zi], axis=0)
    wts_ref[...] = jnp.concatenate([v1 / tot, v2 / tot, jnp.zeros((6, tm), F32)], axis=0)


def _router(x32, router_wt, router_b, *, tm=512):
    T, D = x32.shape
    tm = min(tm, T)
    tok = pl.BlockSpec((8, tm), lambda i: (0, i))
    return pl.pallas_call(
        functools.partial(_router_kernel, tm=tm),
        grid=(T // tm,),
        in_specs=[pl.BlockSpec((tm, D), lambda i: (i, 0)),
                  pl.BlockSpec((N_EXPERTS, D), lambda i: (0, 0)),
                  pl.BlockSpec((N_EXPERTS, 1), lambda i: (0, 0))],
        out_specs=[tok, tok, tok, pl.BlockSpec((N_EXPERTS, LANE), lambda i: (0, 0))],
        out_shape=[jax.ShapeDtypeStruct((8, T), jnp.int32), jax.ShapeDtypeStruct((8, T), F32),
                   jax.ShapeDtypeStruct((8, T), jnp.int32),
                   jax.ShapeDtypeStruct((N_EXPERTS, LANE), F32)],
        scratch_shapes=[pltpu.VMEM((N_EXPERTS, LANE), F32)],
        compiler_params=_cparams(("arbitrary",)),
        name="moe_router",
    )(x32, router_wt, router_b.reshape(N_EXPERTS, 1))


def _row_copy(src, src_row, dst, dst_row, sem, n=1):
    return pltpu.make_async_copy(src.at[pl.ds(src_row, n)], dst.at[pl.ds(dst_row, n)], sem)


def _row_tokens_kernel(n_ref, d0_ref, d1_ref, tok_ref):
    step = 8

    def clear(i, c):
        for k in range(step):
            tok_ref[i * step + k] = 0
        return c

    def put(i, c):
        for k in range(step):
            t = i * step + k
            tok_ref[d0_ref[t]] = t
            tok_ref[d1_ref[t]] = t
        return c

    lax.fori_loop(0, n_ref[0] // step, clear, 0)
    lax.fori_loop(0, n_ref[1] // step, put, 0)


def _row_tokens(d0, d1, n_rows):
    n_tok = d0.shape[0]
    assert n_rows % 8 == 0 and n_tok % 8 == 0
    return pl.pallas_call(
        _row_tokens_kernel,
        grid_spec=pltpu.PrefetchScalarGridSpec(
            num_scalar_prefetch=3, grid=(1,), in_specs=[],
            out_specs=pl.BlockSpec(memory_space=pltpu.SMEM)),
        out_shape=jax.ShapeDtypeStruct((n_rows,), jnp.int32),
        compiler_params=pltpu.CompilerParams(dimension_semantics=("arbitrary",)),
        name="moe_row_tokens",
    )(jnp.array([n_rows, n_tok], jnp.int32), d0, d1)


def _ffn_kernel(be_ref, nact_ref, tok_ref, x_hbm, wg_ref, wu_ref, wd_ref, y_ref, xbuf, sems):
    i = pl.program_id(0)
    n_act = nact_ref[0]
    R = EXPERT_ROWS

    def gather(block, slot):
        def issue(r, c):
            _row_copy(x_hbm, tok_ref[block * R + r], xbuf.at[slot], r, sems.at[slot]).start()
            return c
        lax.fori_loop(0, R, issue, 0, unroll=8)

    def run(slot):
        @pl.when(i + 1 < n_act)
        def _():
            gather(i + 1, 1 - slot)

        def drain(r, c):
            _row_copy(x_hbm, 0, xbuf.at[slot], 0, sems.at[slot]).wait()
            return c
        lax.fori_loop(0, R, drain, 0, unroll=8)
        xb = xbuf[slot].astype(BF16)
        g = jnp.dot(xb, wg_ref[...], preferred_element_type=F32)
        u = jnp.dot(xb, wu_ref[...], preferred_element_type=F32)
        h = (jax.nn.silu(g) * u).astype(BF16)
        y_ref[...] = jnp.dot(h, wd_ref[...], preferred_element_type=F32)

    @pl.when(i == 0)
    def _():
        gather(0, 0)

    for slot in range(2):
        @pl.when((i < n_act) & (i % 2 == slot))
        def _():
            run(slot)

    @pl.when(i >= n_act)
    def _():
        y_ref[...] = jnp.zeros_like(y_ref)


def _expert_ffn(x32, row_tok, blk_e, n_act, w_gate, w_up, w_down):
    D = x32.shape[1]
    R = EXPERT_ROWS
    n_rows = row_tok.shape[0]
    wmap = lambda i, be, na, tok: (be[i], 0, 0)
    return pl.pallas_call(
        _ffn_kernel,
        grid_spec=pltpu.PrefetchScalarGridSpec(
            num_scalar_prefetch=3, grid=(n_rows // R,),
            in_specs=[pl.BlockSpec(memory_space=pl.ANY),
                      pl.BlockSpec((None, D, D_EXPERT), wmap),
                      pl.BlockSpec((None, D, D_EXPERT), wmap),
                      pl.BlockSpec((None, D_EXPERT, D), wmap)],
            out_specs=pl.BlockSpec((R, D), lambda i, be, na, tok: (i, 0)),
            scratch_shapes=[pltpu.VMEM((2, R, D), F32), pltpu.SemaphoreType.DMA((2,))]),
        out_shape=jax.ShapeDtypeStruct((n_rows, D), F32),
        compiler_params=_cparams(("arbitrary",)),
        name="moe_ffn",
    )(blk_e, n_act, row_tok, x32, w_gate, w_up, w_down)


def _combine_kernel(d0_ref, d1_ref, res_ref, w_ref, g_ref, b_ref, y_hbm, of_ref, ob_ref,
                    buf, sem, *, tm):
    i = pl.program_id(0)

    def issue(r, c):
        t = i * tm + r
        _row_copy(y_hbm, d0_ref[t], buf.at[0], r, sem).start()
        _row_copy(y_hbm, d1_ref[t], buf.at[1], r, sem).start()
        return c

    def drain(r, c):
        _row_copy(y_hbm, 0, buf.at[0], 0, sem).wait()
        _row_copy(y_hbm, 0, buf.at[1], 0, sem).wait()
        return c

    lax.fori_loop(0, tm, issue, 0)
    lax.fori_loop(0, tm, drain, 0)
    w = w_ref[...]
    z = ALPHA * res_ref[...] + (w[:, 0:1] * buf[0] + w[:, 1:2] * buf[1])
    y = _layer_norm_rows(z, g_ref[...], b_ref[...])
    of_ref[...] = y
    ob_ref[...] = y.astype(BF16)


def _combine_ln(y, d0, d1, wts, res, g, b, *, tm=256):
    T, D = res.shape
    tm = min(tm, T)
    row = pl.BlockSpec((tm, D), lambda i, a, c: (i, 0))
    vec = pl.BlockSpec((1, D), lambda i, a, c: (0, 0))
    return pl.pallas_call(
        functools.partial(_combine_kernel, tm=tm),
        grid_spec=pltpu.PrefetchScalarGridSpec(
            num_scalar_prefetch=2, grid=(T // tm,),
            in_specs=[row, pl.BlockSpec((tm, 2), lambda i, a, c: (i, 0)), vec, vec,
                      pl.BlockSpec(memory_space=pl.ANY)],
            out_specs=[row, row],
            scratch_shapes=[pltpu.VMEM((2, tm, D), F32), pltpu.SemaphoreType.DMA(())]),
        out_shape=[jax.ShapeDtypeStruct((T, D), F32), jax.ShapeDtypeStruct((T, D), BF16)],
        compiler_params=_cparams(("arbitrary",)),
        name="moe_combine_ln",
    )(d0, d1, res, wts, g.reshape(1, D), b.reshape(1, D), y)


def _moe_layer(x32, router_wt, router_b, w_gate, w_up, w_down, ln_g, ln_b):
    T = x32.shape[0]
    R = EXPERT_ROWS
    ids, wts, pos, cnt = _router(x32, router_wt, router_b)
    counts = cnt[:, 0].astype(jnp.int32)
    padded = (counts + R - 1) // R * R
    p_end = jnp.cumsum(padded)
    offs = p_end - padded
    d0 = offs[ids[0]] + pos[0]
    d1 = offs[ids[1]] + pos[1]
    n_blocks = 2 * T // R + N_EXPERTS
    n_act = p_end[-1] // R
    step = jnp.arange(n_blocks, dtype=jnp.int32)
    live = step < n_act
    src = jnp.where(live, step, n_act - 1)
    blk_e = jnp.minimum(jnp.searchsorted(p_end // R, src, side="right"), N_EXPERTS - 1).astype(jnp.int32)
    row_tok = _row_tokens(d0, d1, n_blocks * R)
    y = _expert_ffn(x32, row_tok, blk_e, n_act.reshape(1).astype(jnp.int32), w_gate, w_up, w_down)
    return _combine_ln(y, d0, d1, wts[:2].T, x32, ln_g, ln_b)


def _t5_bucket(dist):
    n = jnp.maximum(dist, 0)
    max_exact = NUM_BUCKETS // 2
    nf = jnp.maximum(n, 1).astype(F32)
    large = max_exact + (jnp.log(nf / max_exact) / math.log(MAX_DISTANCE / max_exact)
                         * (NUM_BUCKETS - max_exact)).astype(jnp.int32)
    large = jnp.minimum(large, NUM_BUCKETS - 1)
    return jnp.where(n < max_exact, n, large)


def _stack_heads(tbl):
    parts = tbl.reshape((N_KV, GROUP) + tbl.shape[1:])
    return jnp.concatenate([parts[:, g] for g in range(GROUP)], axis=-1)


def _bias_tables(rel_bias, S):
    off = CMP_STRIDE * CMP_TILE + Q_BLOCK
    n_dist = CMP_NEAR * Q_BLOCK
    bvec = rel_bias[_t5_bucket(jnp.arange(n_dist))].T
    last = rel_bias[NUM_BUCKETS - 1]
    masked = jnp.full((N_HEADS, off), NEG, F32)
    w_abs = jnp.concatenate([masked, bvec * LOG2E], axis=1)
    w_rel = jnp.concatenate([masked, (bvec - last[:, None]) * LOG2E], axis=1)
    w_win = jnp.where(jnp.arange(-off, n_dist) < WINDOW, w_abs, NEG)

    def toeplitz(w, first, step, n):
        lo = first - step * (n - 1) + off
        span = step * (n - 1) + Q_BLOCK
        u = w[:, lo:lo + span]
        width = span + step
        tiled = jnp.tile(u, (1, -(-n * width // span)))[:, :n * width]
        return tiled.reshape(N_HEADS, n, width)[:, ::-1, :Q_BLOCK]

    const = lambda v: jnp.broadcast_to(v, (N_HEADS, 1, CMP_TILE, Q_BLOCK)).astype(F32)
    near_c = [toeplitz(w_abs, Q_BLOCK * d - (CMP_LEN - 1), CMP_STRIDE, CMP_TILE)[:, None]
              for d in range(CMP_NEAR)]
    tc = jnp.concatenate([const(NEG)] + near_c + [const(last[:, None, None, None] * LOG2E)], axis=1)
    zero = jnp.zeros((N_HEADS, 1, SEL_CHUNK, Q_BLOCK), F32)
    ts = jnp.stack([
        jnp.concatenate(
            [zero] + [toeplitz(w_rel, Q_BLOCK * sub + SEL_CHUNK * (SEL_NEAR - 1 - i), 1, SEL_CHUNK)[:, None]
                      for i in range(SEL_NEAR)] + [zero + NEG], axis=1)
        for sub in range(SEL_CHUNK // Q_BLOCK)], axis=1)
    tw = toeplitz(w_win, WINDOW, 1, WINDOW + Q_BLOCK)
    return _stack_heads(tc), _stack_heads(ts), _stack_heads(tw)


def _compress_kernel(x_ref, pe_ref, w1_ref, w2_ref, o_ref, *, transposed):
    x = x_ref[...]
    pe = pe_ref[...]
    half = CMP_STRIDE * HEAD_DIM
    a = jnp.dot((x + pe[0:1]).astype(BF16), w1_ref[:half], preferred_element_type=F32)
    b = jnp.dot((x + pe[1:2]).astype(BF16), w1_ref[half:], preferred_element_type=F32)
    n = x.shape[0]
    pre = a + pltpu.roll(b, n - 1, 0)
    rowi = lax.broadcasted_iota(jnp.int32, pre.shape, 0)
    hid = jnp.where(rowi < n - 1, jax.nn.gelu(pre), 0.0).astype(BF16)
    if transposed:
        out = lax.dot_general(w2_ref[...], hid, (((1,), (1,)), ((), ())), preferred_element_type=F32)
    else:
        out = jnp.dot(hid, w2_ref[...], preferred_element_type=F32)
    o_ref[...] = out.astype(BF16)


def _compress(slabs, pe, w1, w2, *, B, S, transposed):
    n = S // CMP_STRIDE
    wide = CMP_STRIDE * HEAD_DIM
    out_shape = (N_KV, B, HEAD_DIM, n) if transposed else (N_KV, B, n, HEAD_DIM)
    w2_in = w2.T if transposed else w2
    return pl.pallas_call(
        functools.partial(_compress_kernel, transposed=transposed),
        grid=(N_KV, B),
        in_specs=[pl.BlockSpec((None, None, n, wide), lambda h, b: (h, b, 0, 0)),
                  pl.BlockSpec((2, wide), lambda h, b: (0, 0)),
                  pl.BlockSpec((2 * wide, HEAD_DIM), lambda h, b: (0, 0)),
                  pl.BlockSpec((HEAD_DIM, HEAD_DIM), lambda h, b: (0, 0))],
        out_specs=pl.BlockSpec((None, None) + out_shape[2:], lambda h, b: (h, b, 0, 0)),
        out_shape=jax.ShapeDtypeStruct(out_shape, BF16),
        compiler_params=_cparams(("parallel", "parallel")),
        name="nsa_compress",
    )(slabs, pe.reshape(2, wide), w1, w2_in)


def _query_cols(q_ref):
    return jnp.concatenate([q_ref[g * HEAD_DIM:(g + 1) * HEAD_DIM, :] for g in range(GROUP)], axis=1)


def _cmp_body(ib, n_tiles, q_ref, kc_ref, vct_ref, tc_ref, ov_ref, oc_ref, mb_ref):
    n_cmp = n_tiles * CMP_TILE
    n_sel = n_cmp * CMP_STRIDE // SEL_LEN
    qt = _query_cols(q_ref)
    s = jnp.dot(kc_ref[:n_cmp], qt, preferred_element_type=F32)
    tiles = []
    for cb in range(n_tiles):
        idx = jnp.clip(ib - (CMP_TILE * CMP_STRIDE // Q_BLOCK) * cb + 1, 0, CMP_NEAR + 1)
        tiles.append(s[cb * CMP_TILE:(cb + 1) * CMP_TILE] + tc_ref[idx])
    s = jnp.concatenate(tiles, axis=0)
    valid = s > 0.5 * NEG
    m = jnp.max(s, axis=0, keepdims=True)
    p = jnp.where(valid, jnp.exp2(s - m), 0.0)
    p = p / jnp.maximum(jnp.sum(p, axis=0, keepdims=True), 1e-30)
    oc_ref[...] = jnp.dot(vct_ref[:, :n_cmp], p.astype(BF16), preferred_element_type=F32)
    ov = ov_ref[:n_sel, :n_cmp]

    psum = p[:, 0:Q_BLOCK]
    for g in range(1, GROUP):
        psum = psum + p[:, g * Q_BLOCK:(g + 1) * Q_BLOCK]
    p_hi = psum.astype(BF16)
    p_lo = (psum - p_hi.astype(F32)).astype(BF16)
    imp = (jnp.dot(ov, p_hi, preferred_element_type=F32)
           + jnp.dot(ov, p_lo, preferred_element_type=F32))
    blk = lax.broadcasted_iota(jnp.int32, (n_sel, Q_BLOCK), 0)
    t = ib * Q_BLOCK + lax.broadcasted_iota(jnp.int32, (n_sel, Q_BLOCK), 1)
    cur = jnp.right_shift(t, int(math.log2(SEL_LEN)))
    ok = blk * SEL_LEN <= t
    forced = ok & ((blk == 0) | (blk == cur) | (blk == cur - 1))
    score = jnp.where(forced, BIG, jnp.where(ok, imp, NEG))
    taken = jnp.zeros((n_sel, Q_BLOCK), jnp.bool_)
    for _ in range(SEL_TOP):
        top = jnp.max(score, axis=0, keepdims=True)
        first = jnp.min(jnp.where(score == top, blk, n_sel), axis=0, keepdims=True)
        pick = blk == first
        taken = taken | pick
        score = jnp.where(pick, -jnp.inf, score)
    mb_ref[:n_sel] = jnp.where(taken & ok, 0.0, NEG).astype(BF16)
    if n_sel < mb_ref.shape[0]:
        mb_ref[n_sel:] = jnp.full((mb_ref.shape[0] - n_sel, Q_BLOCK), NEG, BF16)


def _cmp_kernel(*refs, n_tiles):
    ib = pl.program_id(2)
    per_tile = CMP_TILE * CMP_STRIDE // Q_BLOCK
    counts = sorted({min(1 << k, n_tiles) for k in range(n_tiles.bit_length() + 1)})
    lo = 0
    for nt in counts:
        hi = nt * per_tile

        @pl.when((ib >= lo) & (ib < hi))
        def _():
            _cmp_body(ib, nt, *refs)
        lo = hi


def _cmp_attention(qt, kc, vct, tc, overlap_t, *, B, S):
    nq = S // Q_BLOCK
    n_cmp = S // CMP_STRIDE
    n_sel = S // SEL_LEN
    n_tiles = n_cmp // CMP_TILE
    return pl.pallas_call(
        functools.partial(_cmp_kernel, n_tiles=n_tiles),
        grid=(N_KV, B, nq),
        in_specs=[pl.BlockSpec((GROUP * HEAD_DIM, Q_BLOCK), lambda h, b, i: (h, b * nq + i)),
                  pl.BlockSpec((None, None, n_cmp, HEAD_DIM), lambda h, b, i: (h, b, 0, 0)),
                  pl.BlockSpec((None, None, HEAD_DIM, n_cmp), lambda h, b, i: (h, b, 0, 0)),
                  pl.BlockSpec((None, CMP_NEAR + 2, CMP_TILE, QG), lambda h, b, i: (h, 0, 0, 0)),
                  pl.BlockSpec((n_sel, n_cmp), lambda h, b, i: (0, 0))],
        out_specs=[pl.BlockSpec((None, None, HEAD_DIM, QG), lambda h, b, i: (h, b * nq + i, 0, 0)),
                   pl.BlockSpec((None, None, n_sel, Q_BLOCK), lambda h, b, i: (h, b, 0, i))],
        out_shape=[jax.ShapeDtypeStruct((N_KV, B * nq, HEAD_DIM, QG), F32),
                   jax.ShapeDtypeStruct((N_KV, B, n_sel, S), BF16)],
        compiler_params=_cparams(("parallel", "parallel", "parallel")),
        name="nsa_cmp_select",
    )(qt, kc, vct, tc, overlap_t)


def _sel_kernel(q_ref, mb_ref, ka_ref, vt_ref, ts_ref, o_ref, m_ref, acc_ref, sa_ref, sb_ref):
    ib = pl.program_id(2)
    per = SEL_CHUNK // Q_BLOCK
    cq = ib // per
    sub = ib % per
    qt = _query_cols(q_ref)
    blocks_per_chunk = SEL_CHUNK // SEL_LEN
    chunks_per_group = LANE // blocks_per_chunk

    m_ref[...] = jnp.full(m_ref.shape, NEG, F32)
    acc_ref[...] = jnp.zeros_like(acc_ref)

    qq = cq // SEL_FAR
    rows = SEL_FAR * SEL_CHUNK

    def raw_scores(quad):
        c0 = pl.multiple_of(quad * SEL_FAR, SEL_FAR)
        grp = c0 // chunks_per_group
        mb = mb_ref[pl.ds(pl.multiple_of(grp * LANE, LANE), LANE), :]
        qa = jnp.concatenate([qt, jnp.concatenate([mb] * GROUP, axis=1)], axis=0)
        keys = ka_ref[pl.ds(c0, SEL_FAR)].reshape(rows, 2 * HEAD_DIM)
        return jnp.dot(keys, qa, preferred_element_type=F32)

    def update(s, quad):
        c0 = quad * SEL_FAR
        v = jnp.concatenate([vt_ref[c0 + k] for k in range(SEL_FAR)], axis=1)
        m_old = m_ref[0:1, :]
        m_new = jnp.maximum(m_old, jnp.max(s, axis=0, keepdims=True))
        p = jnp.exp2((s - m_new).astype(BF16))
        acc_ref[...] = (jnp.exp2(m_old - m_new) * acc_ref[...]
                        + jnp.dot(v, p, preferred_element_type=F32))
        m_ref[...] = jnp.broadcast_to(m_new, m_ref.shape)

    def bias(quad, live):
        tiles = []
        for k in range(SEL_FAR):
            near = jnp.clip(quad * SEL_FAR + k - cq + SEL_NEAR, 0, SEL_NEAR + 1)
            tiles.append(ts_ref[sub, jnp.where(live, near, SEL_NEAR + 1)])
        return jnp.concatenate(tiles, axis=0)

    def finish(near0_ref):
        s1 = raw_scores(qq)
        update(near0_ref[...] + bias(qq - 1, qq >= 1), jnp.maximum(qq - 1, 0))
        update(s1 + bias(qq, True), qq)
        o_ref[...] = acc_ref[:HEAD_DIM] / jnp.maximum(acc_ref[HEAD_DIM:HEAD_DIM + 1], 1e-30)

    n_far = jnp.maximum(qq - 1, 0)
    sa_ref[...] = raw_scores(0)

    def far_pair(i, carry):
        sb_ref[...] = raw_scores(2 * i + 1)
        update(sa_ref[...], 2 * i)
        sa_ref[...] = raw_scores(2 * i + 2)
        update(sb_ref[...], 2 * i + 1)
        return carry

    lax.fori_loop(0, n_far // 2, far_pair, 0)

    @pl.when(n_far % 2 == 1)
    def _():
        sb_ref[...] = raw_scores(n_far)
        update(sa_ref[...], n_far - 1)
        finish(sb_ref)

    @pl.when(n_far % 2 == 0)
    def _():
        finish(sa_ref)


def _sel_attention(qt, mbt, k_aug, vt, ts, *, B, S):
    nq = S // Q_BLOCK
    n_sel = mbt.shape[2]
    nc = S // SEL_CHUNK
    return pl.pallas_call(
        _sel_kernel,
        grid=(N_KV, B, nq),
        in_specs=[pl.BlockSpec((GROUP * HEAD_DIM, Q_BLOCK), lambda h, b, i: (h, b * nq + i)),
                  pl.BlockSpec((None, None, n_sel, Q_BLOCK), lambda h, b, i: (h, b, 0, i)),
                  pl.BlockSpec((None, None, nc, SEL_CHUNK, 2 * HEAD_DIM), lambda h, b, i: (h, b, 0, 0, 0)),
                  pl.BlockSpec((None, None, nc, V_ROWS, SEL_CHUNK), lambda h, b, i: (h, b, 0, 0, 0)),
                  pl.BlockSpec((None, SEL_CHUNK // Q_BLOCK, SEL_NEAR + 2, SEL_CHUNK, QG),
                               lambda h, b, i: (h, 0, 0, 0, 0))],
        out_specs=pl.BlockSpec((None, None, HEAD_DIM, QG), lambda h, b, i: (h, b * nq + i, 0, 0)),
        out_shape=jax.ShapeDtypeStruct((N_KV, B * nq, HEAD_DIM, QG), F32),
        scratch_shapes=[pltpu.VMEM((8, QG), F32), pltpu.VMEM((V_ROWS, QG), F32),
                        pltpu.VMEM((SEL_FAR * SEL_CHUNK, QG), F32),
                        pltpu.VMEM((SEL_FAR * SEL_CHUNK, QG), F32)],
        compiler_params=_cparams(("parallel", "parallel", "arbitrary")),
        name="nsa_selected",
    )(qt, mbt, k_aug, vt, ts)


def _win_kernel(*refs):
    nb = WINDOW // Q_BLOCK + 1
    q_ref = refs[0]
    k_refs = refs[1:1 + nb]
    v_refs = refs[1 + nb:1 + 2 * nb]
    tw_ref, gate_ref, oc_ref, os_ref, o_ref = refs[1 + 2 * nb:]
    ib = pl.program_id(2)
    qt = _query_cols(q_ref)
    parts = []
    for i in range(nb):
        s = jnp.dot(k_refs[i][...], qt, preferred_element_type=F32)
        s = s + tw_ref[i * Q_BLOCK:(i + 1) * Q_BLOCK, :]
        parts.append(jnp.where(ib - (nb - 1) + i >= 0, s, NEG))
    s = jnp.concatenate(parts, axis=0)
    m = jnp.max(s, axis=0, keepdims=True)
    p = jnp.exp2(s - m)
    l = jnp.sum(p, axis=0, keepdims=True)
    pb = p.astype(BF16)
    ow = jnp.dot(v_refs[0][...], pb[0:Q_BLOCK], preferred_element_type=F32)
    for i in range(1, nb):
        ow = ow + jnp.dot(v_refs[i][...], pb[i * Q_BLOCK:(i + 1) * Q_BLOCK], preferred_element_type=F32)
    ow = ow / jnp.maximum(l, 1e-30)

    def gate(j):
        return jnp.concatenate([gate_ref[j, g:g + 1, :] for g in range(GROUP)], axis=1)

    o = gate(0) * oc_ref[...] + gate(1) * os_ref[...] + gate(2) * ow
    for g in range(GROUP):
        o_ref[:, g * HEAD_DIM:(g + 1) * HEAD_DIM] = o[:, g * Q_BLOCK:(g + 1) * Q_BLOCK].T.astype(BF16)


def _win_attention(qt, kw, vwt, tw, gates_t, oc, osel, *, B, S):
    nq = S // Q_BLOCK
    nb = WINDOW // Q_BLOCK + 1

    def kmap(i_off):
        return lambda h, b, i: (h, b * nq + jnp.maximum(i - (nb - 1) + i_off, 0), 0)

    def vmap_(i_off):
        return lambda h, b, i: (h, 0, b * nq + jnp.maximum(i - (nb - 1) + i_off, 0))

    blk = pl.BlockSpec((None, None, HEAD_DIM, QG), lambda h, b, i: (h, b * nq + i, 0, 0))
    return pl.pallas_call(
        _win_kernel,
        grid=(N_KV, B, nq),
        in_specs=([pl.BlockSpec((GROUP * HEAD_DIM, Q_BLOCK), lambda h, b, i: (h, b * nq + i))]
                  + [pl.BlockSpec((None, Q_BLOCK, HEAD_DIM), kmap(j)) for j in range(nb)]
                  + [pl.BlockSpec((None, HEAD_DIM, Q_BLOCK), vmap_(j)) for j in range(nb)]
                  + [pl.BlockSpec((None, WINDOW + Q_BLOCK, QG), lambda h, b, i: (h, 0, 0)),
                     pl.BlockSpec((3, None, GROUP, Q_BLOCK), lambda h, b, i: (0, h, 0, b * nq + i)),
                     blk, blk]),
        out_specs=pl.BlockSpec((Q_BLOCK, GROUP * HEAD_DIM), lambda h, b, i: (b * nq + i, h)),
        out_shape=jax.ShapeDtypeStruct((B * S, N_HEADS * HEAD_DIM), BF16),
        compiler_params=_cparams(("parallel", "parallel", "parallel")),
        name="nsa_window_merge",
    )(qt, *([kw] * nb), *([vwt] * nb), tw, gates_t, oc, osel)


def _nsa_layer(x32, xb, p, tables, consts, ln_g, ln_b, *, B, S):
    T = B * S
    tc, ts, tw = tables
    overlap_t, sel_onehot = consts
    kd = N_KV * HEAD_DIM
    qv = _proj_nt(p["w_qv_t"], xb, out_dtype=BF16, scale=HEAD_DIM ** -0.5 * LOG2E,
                  n_scaled=N_HEADS * HEAD_DIM // 512)
    qt = qv[:N_HEADS * HEAD_DIM]
    k2 = _proj_nn(xb, p["w_k"], out_dtype=BF16, head_major=True)
    cmp = _proj_nn(xb, p["w_cmp"], out_dtype=F32, head_major=True)
    gates_t = _proj_nt(p["w_gate_t"], xb, out_dtype=F32, act="sigmoid")
    gates_t = gates_t.reshape(3, N_KV, GROUP, T)

    cmp = cmp.reshape(2, N_KV, B, S // CMP_STRIDE, CMP_STRIDE * HEAD_DIM)
    kc = _compress(cmp[0], p["pe_k"], p["w1_k"], p["w2_k"], B=B, S=S, transposed=False)
    vct = _compress(cmp[1], p["pe_v"], p["w1_v"], p["w2_v"], B=B, S=S, transposed=True)
    oc, mbt = _cmp_attention(qt, kc, vct, tc, overlap_t, B=B, S=S)
    if mbt.shape[2] % LANE:
        mbt = jnp.pad(mbt, ((0, 0), (0, 0), (0, -mbt.shape[2] % LANE), (0, 0)), constant_values=NEG)

    nc = S // SEL_CHUNK
    k_sel = k2[:N_KV].reshape(N_KV, B, nc, SEL_CHUNK, HEAD_DIM)
    k_aug = jnp.concatenate([k_sel, jnp.broadcast_to(sel_onehot, k_sel.shape)], axis=-1)
    v_sel_t = qv[N_HEADS * HEAD_DIM:N_HEADS * HEAD_DIM + kd].reshape(N_KV, HEAD_DIM, B, nc, SEL_CHUNK)
    v_sel_t = v_sel_t.transpose(0, 2, 3, 1, 4)
    ones_rows = (jnp.arange(V_ROWS - HEAD_DIM) == 0).astype(BF16)[:, None]
    v_sel_t = jnp.concatenate(
        [v_sel_t, jnp.broadcast_to(ones_rows, v_sel_t.shape[:3] + (V_ROWS - HEAD_DIM, SEL_CHUNK))], axis=3)
    osel = _sel_attention(qt, mbt, k_aug, v_sel_t, ts, B=B, S=S)

    v_win_t = qv[N_HEADS * HEAD_DIM + kd:].reshape(N_KV, HEAD_DIM, T)
    attn = _win_attention(qt, k2[N_KV:], v_win_t, tw, gates_t, oc, osel, B=B, S=S)
    return _proj_ln(attn, p["w_out"], x32, ln_g, ln_b)


def _nsa_params(w_in, pe_k, w1_k, w2_k, pe_v, w1_v, w2_v, w_out):
    D = w_in.shape[0]
    qd, kd = N_HEADS * HEAD_DIM, N_KV * HEAD_DIM
    w_kv = w_in[:, qd:qd + 6 * kd].reshape(D, 6, kd)
    gcol = np.array([[[(kv * GROUP + g) * 3 + j for g in range(GROUP)] for kv in range(N_KV)]
                     for j in range(3)]).reshape(-1)
    return dict(
        w_qv_t=jnp.concatenate([w_in[:, :qd], w_kv[:, 3], w_kv[:, 5]], axis=1).T.astype(BF16),
        w_k=jnp.concatenate([w_kv[:, 2], w_kv[:, 4]], axis=1).astype(BF16),
        w_cmp=jnp.concatenate([w_kv[:, 0], w_kv[:, 1]], axis=1).astype(BF16),
        w_gate_t=w_in[:, qd + 6 * kd:][:, gcol].T.astype(BF16),
        pe_k=pe_k, w1_k=w1_k.astype(BF16), w2_k=w2_k.astype(BF16),
        pe_v=pe_v, w1_v=w1_v.astype(BF16), w2_v=w2_v.astype(BF16),
        w_out=w_out.astype(BF16))


def _nsa_constants(S):
    n_cmp, n_sel = S // CMP_STRIDE, S // SEL_LEN
    cj = np.arange(n_cmp) * CMP_STRIDE
    sb = np.arange(n_sel) * SEL_LEN
    ov = np.maximum(np.minimum(cj[None, :] + CMP_LEN, sb[:, None] + SEL_LEN)
                    - np.maximum(cj[None, :], sb[:, None]), 0).astype(np.float32) / CMP_LEN
    ov[:, n_cmp - 1] = 0.0
    key = np.arange(SEL_CHUNK)
    chunk = np.arange(S // SEL_CHUNK)
    blk = (chunk[:, None] * (SEL_CHUNK // SEL_LEN) + key[None, :] // SEL_LEN) % LANE
    onehot = (blk[:, :, None] == np.arange(LANE)[None, None, :]).astype(np.float32)
    return jnp.asarray(ov, BF16), jnp.asarray(onehot, BF16)


def kernel(x, rel_bias, router_w, router_b, ln1_g, ln1_b, ln2_g, ln2_b, lru_w_in, lru_conv_w,
           lru_conv_b, lru_w_a, lru_b_a, lru_w_i, lru_b_i, lru_lambda, lru_w_out, nsa_w_in,
           nsa_pe_k, nsa_w1_k, nsa_w2_k, nsa_pe_v, nsa_w1_v, nsa_w2_v, nsa_w_out,
           moe_w_gate, moe_w_up, moe_w_down):
    B, S, D = x.shape
    T = B * S
    assert D == D_MODEL and S % (CMP_TILE * CMP_STRIDE) == 0 and S // SEL_LEN >= SEL_TOP
    qd, kd = N_HEADS * HEAD_DIM, N_KV * HEAD_DIM
    tables = _bias_tables(rel_bias, S)
    consts = _nsa_constants(S)
    router_wt = router_w.T

    x32 = x.reshape(T, D)
    xb = x32.astype(BF16)
    for layer in range(DEPTH):
        j = layer // 2
        if layer % 2 == 0:
            w_in = lru_w_in[j].astype(BF16)
            p = dict(w_gate=w_in[:, :D], w_y=w_in[:, D:], conv_w=lru_conv_w[j], conv_b=lru_conv_b[j],
                     w_a=lru_w_a[j].astype(BF16), b_a=lru_b_a[j], w_i=lru_w_i[j].astype(BF16),
                     b_i=lru_b_i[j], lam=lru_lambda[j], w_out=lru_w_out[j].astype(BF16))
            x32, xb = _rglru_layer(x32, xb, p, ln1_g[layer], ln1_b[layer], B=B, S=S)
        else:
            p = _nsa_params(nsa_w_in[j], nsa_pe_k[j], nsa_w1_k[j], nsa_w2_k[j], nsa_pe_v[j],
                            nsa_w1_v[j], nsa_w2_v[j], nsa_w_out[j])
            x32, xb = _nsa_layer(x32, xb, p, tables, consts, ln1_g[layer], ln1_b[layer], B=B, S=S)
        x32, xb = _moe_layer(x32, router_wt, router_b, moe_w_gate[layer].astype(BF16),
                             moe_w_up[layer].astype(BF16), moe_w_down[layer].astype(BF16),
                             ln2_g[layer], ln2_b[layer])
    return x32.reshape(B, S, D)
```

```python
import functools
import math

import numpy as np
import jax
import jax.numpy as jnp
from jax import lax
from jax.experimental import pallas as pl
from jax.experimental.pallas import tpu as pltpu

F32 = jnp.float32
BF16 = jnp.bfloat16

D_MODEL = 2048
DEPTH = 4
LRU_BLOCK = 256
LRU_NBLK = D_MODEL // LRU_BLOCK
CONV_W = 4
LRU_C = 8.0
HEAD_DIM = 128
N_HEADS = 16
N_KV = 4
GROUP = 4
CMP_LEN = 32
CMP_STRIDE = 16
SEL_LEN = 64
SEL_TOP = 16
WINDOW = 512
Q_BLOCK = 128
NUM_BUCKETS = 32
MAX_DISTANCE = 1024
N_EXPERTS = 16
N_GROUPS = 4
EXPERTS_PER_GROUP = 4
D_EXPERT = 1024
ALPHA = (2 * DEPTH) ** 0.25
LN_EPS = 1e-5
NEG = -1e30
BIG = 1e9

LANE = 128
QG = GROUP * Q_BLOCK
EXPERT_ROWS = 256
SEL_CHUNK = 256
SEL_NEAR = 5
SEL_FAR = 4
V_ROWS = HEAD_DIM + 16
LOG2E = math.log2(math.e)
CMP_TILE = 128
CMP_NEAR = 23
VMEM_LIMIT = 56 << 20


def _cparams(sem, vmem=VMEM_LIMIT):
    return pltpu.CompilerParams(dimension_semantics=sem, vmem_limit_bytes=vmem)


def _act(x, act):
    if act == "gelu":
        return jax.nn.gelu(x)
    if act == "sigmoid":
        return jax.nn.sigmoid(x)
    return x


def _proj_nn_kernel(x_ref, w_ref, o_ref, *, act, head_major):
    acc = jnp.dot(x_ref[...], w_ref[...], preferred_element_type=F32)
    acc = _act(acc, act)
    if head_major:
        for h in range(o_ref.shape[0]):
            o_ref[h] = acc[:, h * LANE:(h + 1) * LANE].astype(o_ref.dtype)
    else:
        o_ref[...] = acc.astype(o_ref.dtype)


def _proj_nn(x, w, *, out_dtype, act=None, head_major=False, tm=1024, tn=1024):
    T, K = x.shape
    N = w.shape[1]
    tm, tn = min(tm, T), min(tn, N)
    if head_major:
        out_shape = (N // LANE, T, LANE)
        out_spec = pl.BlockSpec((tn // LANE, tm, LANE), lambda j, i: (j, i, 0))
    else:
        out_shape = (T, N)
        out_spec = pl.BlockSpec((tm, tn), lambda j, i: (i, j))
    return pl.pallas_call(
        functools.partial(_proj_nn_kernel, act=act, head_major=head_major),
        grid=(N // tn, T // tm),
        in_specs=[pl.BlockSpec((tm, K), lambda j, i: (i, 0)),
                  pl.BlockSpec((K, tn), lambda j, i: (0, j))],
        out_specs=out_spec,
        out_shape=jax.ShapeDtypeStruct(out_shape, out_dtype),
        compiler_params=_cparams(("parallel", "parallel")),
        name="proj_nn",
    )(x, w)


def _proj_nt_kernel(wt_ref, x_ref, o_ref, *, act, scale, n_scaled):
    acc = lax.dot_general(wt_ref[...], x_ref[...], (((1,), (1,)), ((), ())),
                          preferred_element_type=F32)
    if n_scaled:
        acc = acc * jnp.where(pl.program_id(0) < n_scaled, scale, 1.0).astype(F32)
    o_ref[...] = _act(acc, act).astype(o_ref.dtype)


def _proj_nt(wt, x, *, out_dtype, act=None, scale=1.0, n_scaled=0, tm=1024, tn=512):
    N, K = wt.shape
    T = x.shape[0]
    tm, tn = min(tm, T), min(tn, N)
    return pl.pallas_call(
        functools.partial(_proj_nt_kernel, act=act, scale=scale, n_scaled=n_scaled),
        grid=(N // tn, T // tm),
        in_specs=[pl.BlockSpec((tn, K), lambda j, i: (j, 0)),
                  pl.BlockSpec((tm, K), lambda j, i: (i, 0))],
        out_specs=pl.BlockSpec((tn, tm), lambda j, i: (j, i)),
        out_shape=jax.ShapeDtypeStruct((N, T), out_dtype),
        compiler_params=_cparams(("parallel", "parallel")),
        name="proj_nt",
    )(wt, x)


def _layer_norm_rows(z, g, b):
    mu = jnp.mean(z, axis=-1, keepdims=True)
    zc = z - mu
    var = jnp.mean(zc * zc, axis=-1, keepdims=True)
    return zc * lax.rsqrt(var + LN_EPS) * g + b


def _proj_ln_kernel(a_ref, w_ref, res_ref, g_ref, b_ref, of_ref, ob_ref):
    acc = jnp.dot(a_ref[...], w_ref[...], preferred_element_type=F32)
    y = _layer_norm_rows(ALPHA * res_ref[...] + acc, g_ref[...], b_ref[...])
    of_ref[...] = y
    ob_ref[...] = y.astype(BF16)


def _proj_ln(a, w, res, g, b, *, tm=256):
    T, K = a.shape
    D = w.shape[1]
    tm = min(tm, T)
    row = pl.BlockSpec((tm, D), lambda i: (i, 0))
    vec = pl.BlockSpec((1, D), lambda i: (0, 0))
    return pl.pallas_call(
        _proj_ln_kernel,
        grid=(T // tm,),
        in_specs=[pl.BlockSpec((tm, K), lambda i: (i, 0)),
                  pl.BlockSpec((K, D), lambda i: (0, 0)),
                  row, vec, vec],
        out_specs=[row, row],
        out_shape=[jax.ShapeDtypeStruct((T, D), F32), jax.ShapeDtypeStruct((T, D), BF16)],
        compiler_params=_cparams(("parallel",)),
        name="proj_ln",
    )(a, w, res, g.reshape(1, D), b.reshape(1, D))


def _lru_kernel(y_ref, gate_ref, cw_ref, cb_ref, wa_ref, ba_ref, wi_ref, bi_ref, lam_ref,
                o_ref, h_ref, tail_ref, *, rows):
    @pl.when(pl.program_id(2) == 0)
    def _():
        h_ref[...] = jnp.zeros_like(h_ref)
        tail_ref[...] = jnp.zeros_like(tail_ref)

    y = y_ref[...]
    tail = tail_ref[...]
    cw = cw_ref[...]
    row8 = lax.broadcasted_iota(jnp.int32, (8, LRU_BLOCK), 0)
    yc = cb_ref[...] + y * cw[CONV_W - 1:CONV_W, :]
    for d in range(1, CONV_W):
        ys = pltpu.roll(y, d, 0)
        first = jnp.where(row8 < d, pltpu.roll(tail, d, 0), ys[:8])
        ys = jnp.concatenate([first, ys[8:]], axis=0)
        yc = yc + ys * cw[CONV_W - 1 - d:CONV_W - d, :]
    tail_ref[...] = y[rows - 8:]

    yb = yc.astype(BF16)
    r = jax.nn.sigmoid(jnp.dot(yb, wa_ref[...], preferred_element_type=F32) + ba_ref[...])
    ig = jax.nn.sigmoid(jnp.dot(yb, wi_ref[...], preferred_element_type=F32) + bi_ref[...])
    z = -lam_ref[...]
    softplus = jnp.maximum(z, 0.0) + jnp.log1p(jnp.exp(-jnp.abs(z)))
    log_a = (-LRU_C * softplus) * r
    a = jnp.exp(log_a)
    h = jnp.sqrt(1.0 - a * a) * (ig * yc)

    rowi = lax.broadcasted_iota(jnp.int32, (rows, LRU_BLOCK), 0)
    s = 1
    while s < rows:
        if s < 8:
            keep = rowi >= s
            h = jnp.where(keep, h + a * pltpu.roll(h, s, 0), h)
            a = jnp.where(keep, a * pltpu.roll(a, s, 0), a)
        else:
            h = jnp.concatenate([h[:s], h[s:] + a[s:] * h[:rows - s]], axis=0)
            a = jnp.concatenate([a[:s], a[s:] * a[:rows - s]], axis=0)
        s *= 2
    h = h + a * h_ref[0:1, :]
    h_ref[...] = jnp.broadcast_to(h[rows - 1:rows, :], h_ref.shape)
    o_ref[...] = (h * gate_ref[...].astype(F32)).astype(BF16)


def _lru_core(y, gate, conv_w, conv_b, w_a, b_a, w_i, b_i, lam, *, B, S, rows=512):
    T, W = y.shape
    rows = min(rows, S)
    ns = S // rows
    blk = pl.BlockSpec((rows, LRU_BLOCK), lambda n, b, s: (b * ns + s, n))
    vec = pl.BlockSpec((1, LRU_BLOCK), lambda n, b, s: (0, n))
    wblk = pl.BlockSpec((None, LRU_BLOCK, LRU_BLOCK), lambda n, b, s: (n, 0, 0))
    return pl.pallas_call(
        functools.partial(_lru_kernel, rows=rows),
        grid=(LRU_NBLK, B, ns),
        in_specs=[blk, blk,
                  pl.BlockSpec((CONV_W, LRU_BLOCK), lambda n, b, s: (0, n)), vec,
                  wblk, vec, wblk, vec, vec],
        out_specs=blk,
        out_shape=jax.ShapeDtypeStruct((T, W), BF16),
        scratch_shapes=[pltpu.VMEM((8, LRU_BLOCK), F32), pltpu.VMEM((8, LRU_BLOCK), F32)],
        compiler_params=_cparams(("parallel", "parallel", "arbitrary")),
        name="lru_core",
    )(y, gate, conv_w, conv_b.reshape(1, W), w_a, b_a.reshape(1, W), w_i, b_i.reshape(1, W),
      lam.reshape(1, W))


def _rglru_layer(x32, xb, p, ln_g, ln_b, *, B, S):
    gate = _proj_nn(xb, p["w_gate"], out_dtype=BF16, act="gelu")
    y = _proj_nn(xb, p["w_y"], out_dtype=F32)
    hg = _lru_core(y, gate, p["conv_w"], p["conv_b"], p["w_a"], p["b_a"], p["w_i"], p["b_i"],
                   p["lam"], B=B, S=S)
    return _proj_ln(hg, p["w_out"], x32, ln_g, ln_b)


def _router_kernel(x_ref, wt_ref, b_ref, ids_ref, wts_ref, pos_ref, cnt_ref, carry_ref, *, tm):
    @pl.when(pl.program_id(0) == 0)
    def _():
        carry_ref[...] = jnp.zeros_like(carry_ref)

    logits = lax.dot_general(wt_ref[...], x_ref[...], (((1,), (1,)), ((), ())),
                             precision=lax.Precision.HIGHEST,
                             preferred_element_type=F32) + b_ref[...]
    m = jnp.max(logits, axis=0, keepdims=True)
    e = jnp.exp(logits - m)
    p = e / jnp.sum(e, axis=0, keepdims=True)
    rows = [p[k:k + 1, :] for k in range(N_EXPERTS)]

    scores = []
    for g in range(N_GROUPS):
        a, b, c, d = rows[4 * g:4 * g + 4]
        hi1, lo1, hi2, lo2 = jnp.maximum(a, b), jnp.minimum(a, b), jnp.maximum(c, d), jnp.minimum(c, d)
        scores.append(jnp.maximum(hi1, hi2) + jnp.maximum(jnp.minimum(hi1, hi2), jnp.maximum(lo1, lo2)))
    best, gsel = scores[0], jnp.zeros((1, tm), jnp.int32)
    for g in range(1, N_GROUPS):
        upd = scores[g] > best
        gsel = jnp.where(upd, g, gsel)
        best = jnp.where(upd, scores[g], best)

    cand = []
    for k in range(EXPERTS_PER_GROUP):
        c = rows[k]
        for g in range(1, N_GROUPS):
            c = jnp.where(gsel == g, rows[4 * g + k], c)
        cand.append(c)
    v1, i1 = cand[0], jnp.zeros((1, tm), jnp.int32)
    for k in range(1, EXPERTS_PER_GROUP):
        upd = cand[k] > v1
        i1 = jnp.where(upd, k, i1)
        v1 = jnp.where(upd, cand[k], v1)
    v2, i2 = jnp.full((1, tm), -1.0, F32), jnp.zeros((1, tm), jnp.int32)
    for k in range(EXPERTS_PER_GROUP):
        c = jnp.where(i1 == k, -1.0, cand[k])
        upd = c > v2
        i2 = jnp.where(upd, k, i2)
        v2 = jnp.where(upd, c, v2)
    e0 = gsel * EXPERTS_PER_GROUP + i1
    e1 = gsel * EXPERTS_PER_GROUP + i2
    tot = v1 + v2

    eio = lax.broadcasted_iota(jnp.int32, (N_EXPERTS, tm), 0)
    hit0, hit1 = eio == e0, eio == e1
    onehot = jnp.where(hit0 | hit1, 1.0, 0.0)
    before = (lax.broadcasted_iota(jnp.int32, (tm, tm), 0)
              < lax.broadcasted_iota(jnp.int32, (tm, tm), 1))
    rank = jnp.dot(onehot.astype(BF16), jnp.where(before, 1.0, 0.0).astype(BF16),
                   preferred_element_type=F32) + carry_ref[:, 0:1]
    pos0 = jnp.sum(jnp.where(hit0, rank, 0.0), axis=0, keepdims=True)
    pos1 = jnp.sum(jnp.where(hit1, rank, 0.0), axis=0, keepdims=True)
    carry_ref[...] = carry_ref[...] + jnp.sum(onehot, axis=1, keepdims=True)
    cnt_ref[...] = carry_ref[...]

    zi = jnp.zeros((6, tm), jnp.int32)
    ids_ref[...] = jnp.concatenate([e0, e1, zi], axis=0)
    pos_ref[...] = jnp.concatenate([pos0.astype(jnp.int32), pos1.astype(jnp.int32), zi], axis=0)
    wts_ref[...] = jnp.concatenate([v1 / tot, v2 / tot, jnp.zeros((6, tm), F32)], axis=0)


def _router(x32, router_wt, router_b, *, tm=512):
    T, D = x32.shape
    tm = min(tm, T)
    tok = pl.BlockSpec((8, tm), lambda i: (0, i))
    return pl.pallas_call(
        functools.partial(_router_kernel, tm=tm),
        grid=(T // tm,),
        in_specs=[pl.BlockSpec((tm, D), lambda i: (i, 0)),
                  pl.BlockSpec((N_EXPERTS, D), lambda i: (0, 0)),
                  pl.BlockSpec((N_EXPERTS, 1), lambda i: (0, 0))],
        out_specs=[tok, tok, tok, pl.BlockSpec((N_EXPERTS, LANE), lambda i: (0, 0))],
        out_shape=[jax.ShapeDtypeStruct((8, T), jnp.int32), jax.ShapeDtypeStruct((8, T), F32),
                   jax.ShapeDtypeStruct((8, T), jnp.int32),
                   jax.ShapeDtypeStruct((N_EXPERTS, LANE), F32)],
        scratch_shapes=[pltpu.VMEM((N_EXPERTS, LANE), F32)],
        compiler_params=_cparams(("arbitrary",)),
        name="moe_router",
    )(x32, router_wt, router_b.reshape(N_EXPERTS, 1))


def _row_copy(src, src_row, dst, dst_row, sem, n=1):
    return pltpu.make_async_copy(src.at[pl.ds(src_row, n)], dst.at[pl.ds(dst_row, n)], sem)


def _row_tokens_kernel(n_ref, d0_ref, d1_ref, tok_ref):
    step = 8

    def clear(i, c):
        for k in range(step):
            tok_ref[i * step + k] = 0
        return c

    def put(i, c):
        for k in range(step):
            t = i * step + k
            tok_ref[d0_ref[t]] = t
            tok_ref[d1_ref[t]] = t
        return c

    lax.fori_loop(0, n_ref[0] // step, clear, 0)
    lax.fori_loop(0, n_ref[1] // step, put, 0)


def _row_tokens(d0, d1, n_rows):
    n_tok = d0.shape[0]
    assert n_rows % 8 == 0 and n_tok % 8 == 0
    return pl.pallas_call(
        _row_tokens_kernel,
        grid_spec=pltpu.PrefetchScalarGridSpec(
            num_scalar_prefetch=3, grid=(1,), in_specs=[],
            out_specs=pl.BlockSpec(memory_space=pltpu.SMEM)),
        out_shape=jax.ShapeDtypeStruct((n_rows,), jnp.int32),
        compiler_params=pltpu.CompilerParams(dimension_semantics=("arbitrary",)),
        name="moe_row_tokens",
    )(jnp.array([n_rows, n_tok], jnp.int32), d0, d1)


def _ffn_kernel(be_ref, nact_ref, tok_ref, x_hbm, wg_ref, wu_ref, wd_ref, y_ref, xbuf, sems):
    i = pl.program_id(0)
    n_act = nact_ref[0]
    R = EXPERT_ROWS

    def gather(block, slot):
        def issue(r, c):
            _row_copy(x_hbm, tok_ref[block * R + r], xbuf.at[slot], r, sems.at[slot]).start()
            return c
        lax.fori_loop(0, R, issue, 0, unroll=8)

    def run(slot):
        @pl.when(i + 1 < n_act)
        def _():
            gather(i + 1, 1 - slot)

        def drain(r, c):
            _row_copy(x_hbm, 0, xbuf.at[slot], 0, sems.at[slot]).wait()
            return c
        lax.fori_loop(0, R, drain, 0, unroll=8)
        xb = xbuf[slot].astype(BF16)
        g = jnp.dot(xb, wg_ref[...], preferred_element_type=F32)
        u = jnp.dot(xb, wu_ref[...], preferred_element_type=F32)
        h = (jax.nn.silu(g) * u).astype(BF16)
        y_ref[...] = jnp.dot(h, wd_ref[...], preferred_element_type=F32)

    @pl.when(i == 0)
    def _():
        gather(0, 0)

    for slot in range(2):
        @pl.when((i < n_act) & (i % 2 == slot))
        def _():
            run(slot)

    @pl.when(i >= n_act)
    def _():
        y_ref[...] = jnp.zeros_like(y_ref)


def _expert_ffn(x32, row_tok, blk_e, n_act, w_gate, w_up, w_down):
    D = x32.shape[1]
    R = EXPERT_ROWS
    n_rows = row_tok.shape[0]
    wmap = lambda i, be, na, tok: (be[i], 0, 0)
    return pl.pallas_call(
        _ffn_kernel,
        grid_spec=pltpu.PrefetchScalarGridSpec(
            num_scalar_prefetch=3, grid=(n_rows // R,),
            in_specs=[pl.BlockSpec(memory_space=pl.ANY),
                      pl.BlockSpec((None, D, D_EXPERT), wmap),
                      pl.BlockSpec((None, D, D_EXPERT), wmap),
                      pl.BlockSpec((None, D_EXPERT, D), wmap)],
            out_specs=pl.BlockSpec((R, D), lambda i, be, na, tok: (i, 0)),
            scratch_shapes=[pltpu.VMEM((2, R, D), F32), pltpu.SemaphoreType.DMA((2,))]),
        out_shape=jax.ShapeDtypeStruct((n_rows, D), F32),
        compiler_params=_cparams(("arbitrary",)),
        name="moe_ffn",
    )(blk_e, n_act, row_tok, x32, w_gate, w_up, w_down)


def _combine_kernel(d0_ref, d1_ref, res_ref, w_ref, g_ref, b_ref, y_hbm, of_ref, ob_ref,
                    buf, sem, *, tm):
    i = pl.program_id(0)

    def issue(r, c):
        t = i * tm + r
        _row_copy(y_hbm, d0_ref[t], buf.at[0], r, sem).start()
        _row_copy(y_hbm, d1_ref[t], buf.at[1], r, sem).start()
        return c

    def drain(r, c):
        _row_copy(y_hbm, 0, buf.at[0], 0, sem).wait()
        _row_copy(y_hbm, 0, buf.at[1], 0, sem).wait()
        return c

    lax.fori_loop(0, tm, issue, 0)
    lax.fori_loop(0, tm, drain, 0)
    w = w_ref[...]
    z = ALPHA * res_ref[...] + (w[:, 0:1] * buf[0] + w[:, 1:2] * buf[1])
    y = _layer_norm_rows(z, g_ref[...], b_ref[...])
    of_ref[...] = y
    ob_ref[...] = y.astype(BF16)


def _combine_ln(y, d0, d1, wts, res, g, b, *, tm=256):
    T, D = res.shape
    tm = min(tm, T)
    row = pl.BlockSpec((tm, D), lambda i, a, c: (i, 0))
    vec = pl.BlockSpec((1, D), lambda i, a, c: (0, 0))
    return pl.pallas_call(
        functools.partial(_combine_kernel, tm=tm),
        grid_spec=pltpu.PrefetchScalarGridSpec(
            num_scalar_prefetch=2, grid=(T // tm,),
            in_specs=[row, pl.BlockSpec((tm, 2), lambda i, a, c: (i, 0)), vec, vec,
                      pl.BlockSpec(memory_space=pl.ANY)],
            out_specs=[row, row],
            scratch_shapes=[pltpu.VMEM((2, tm, D), F32), pltpu.SemaphoreType.DMA(())]),
        out_shape=[jax.ShapeDtypeStruct((T, D), F32), jax.ShapeDtypeStruct((T, D), BF16)],
        compiler_params=_cparams(("arbitrary",)),
        name="moe_combine_ln",
    )(d0, d1, res, wts, g.reshape(1, D), b.reshape(1, D), y)


def _moe_layer(x32, router_wt, router_b, w_gate, w_up, w_down, ln_g, ln_b):
    T = x32.shape[0]
    R = EXPERT_ROWS
    ids, wts, pos, cnt = _router(x32, router_wt, router_b)
    counts = cnt[:, 0].astype(jnp.int32)
    padded = (counts + R - 1) // R * R
    p_end = jnp.cumsum(padded)
    offs = p_end - padded
    d0 = offs[ids[0]] + pos[0]
    d1 = offs[ids[1]] + pos[1]
    n_blocks = 2 * T // R + N_EXPERTS
    n_act = p_end[-1] // R
    step = jnp.arange(n_blocks, dtype=jnp.int32)
    live = step < n_act
    src = jnp.where(live, step, n_act - 1)
    blk_e = jnp.minimum(jnp.sum(src[:, None] >= (p_end // R)[None, :], axis=1),
                        N_EXPERTS - 1).astype(jnp.int32)
    row_tok = _row_tokens(d0, d1, n_blocks * R)
    y = _expert_ffn(x32, row_tok, blk_e, n_act.reshape(1).astype(jnp.int32), w_gate, w_up, w_down)
    return _combine_ln(y, d0, d1, wts[:2].T, x32, ln_g, ln_b)


def _t5_bucket(dist):
    n = jnp.maximum(dist, 0)
    max_exact = NUM_BUCKETS // 2
    nf = jnp.maximum(n, 1).astype(F32)
    large = max_exact + (jnp.log(nf / max_exact) / math.log(MAX_DISTANCE / max_exact)
                         * (NUM_BUCKETS - max_exact)).astype(jnp.int32)
    large = jnp.minimum(large, NUM_BUCKETS - 1)
    return jnp.where(n < max_exact, n, large)


def _stack_heads(tbl):
    parts = tbl.reshape((N_KV, GROUP) + tbl.shape[1:])
    return jnp.concatenate([parts[:, g] for g in range(GROUP)], axis=-1)


def _bias_tables(rel_bias, S):
    off = CMP_STRIDE * CMP_TILE + Q_BLOCK
    n_dist = CMP_NEAR * Q_BLOCK
    bvec = rel_bias[_t5_bucket(jnp.arange(n_dist))].T
    last = rel_bias[NUM_BUCKETS - 1]
    masked = jnp.full((N_HEADS, off), NEG, F32)
    w_abs = jnp.concatenate([masked, bvec * LOG2E], axis=1)
    w_rel = jnp.concatenate([masked, (bvec - last[:, None]) * LOG2E], axis=1)
    w_win = jnp.where(jnp.arange(-off, n_dist) < WINDOW, w_abs, NEG)

    def toeplitz(w, first, step, n):
        lo = first - step * (n - 1) + off
        span = step * (n - 1) + Q_BLOCK
        u = w[:, lo:lo + span]
        width = span + step
        tiled = jnp.tile(u, (1, -(-n * width // span)))[:, :n * width]
        return tiled.reshape(N_HEADS, n, width)[:, ::-1, :Q_BLOCK]

    const = lambda v: jnp.broadcast_to(v, (N_HEADS, 1, CMP_TILE, Q_BLOCK)).astype(F32)
    near_c = [toeplitz(w_abs, Q_BLOCK * d - (CMP_LEN - 1), CMP_STRIDE, CMP_TILE)[:, None]
              for d in range(CMP_NEAR)]
    tc = jnp.concatenate([const(NEG)] + near_c + [const(last[:, None, None, None] * LOG2E)], axis=1)
    zero = jnp.zeros((N_HEADS, 1, SEL_CHUNK, Q_BLOCK), F32)
    ts = jnp.stack([
        jnp.concatenate(
            [zero] + [toeplitz(w_rel, Q_BLOCK * sub + SEL_CHUNK * (SEL_NEAR - 1 - i), 1, SEL_CHUNK)[:, None]
                      for i in range(SEL_NEAR)] + [zero + NEG], axis=1)
        for sub in range(SEL_CHUNK // Q_BLOCK)], axis=1)
    tw = toeplitz(w_win, WINDOW, 1, WINDOW + Q_BLOCK)
    return _stack_heads(tc), _stack_heads(ts), _stack_heads(tw)


def _compress_kernel(x_ref, pe_ref, w1_ref, w2_ref, o_ref, *, transposed):
    x = x_ref[...]
    pe = pe_ref[...]
    half = CMP_STRIDE * HEAD_DIM
    a = jnp.dot((x + pe[0:1]).astype(BF16), w1_ref[:half], preferred_element_type=F32)
    b = jnp.dot((x + pe[1:2]).astype(BF16), w1_ref[half:], preferred_element_type=F32)
    n = x.shape[0]
    pre = a + pltpu.roll(b, n - 1, 0)
    rowi = lax.broadcasted_iota(jnp.int32, pre.shape, 0)
    hid = jnp.where(rowi < n - 1, jax.nn.gelu(pre), 0.0).astype(BF16)
    if transposed:
        out = lax.dot_general(w2_ref[...], hid, (((1,), (1,)), ((), ())), preferred_element_type=F32)
    else:
        out = jnp.dot(hid, w2_ref[...], preferred_element_type=F32)
    o_ref[...] = out.astype(BF16)


def _compress(slabs, pe, w1, w2, *, B, S, transposed):
    n = S // CMP_STRIDE
    wide = CMP_STRIDE * HEAD_DIM
    out_shape = (N_KV, B, HEAD_DIM, n) if transposed else (N_KV, B, n, HEAD_DIM)
    w2_in = w2.T if transposed else w2
    return pl.pallas_call(
        functools.partial(_compress_kernel, transposed=transposed),
        grid=(N_KV, B),
        in_specs=[pl.BlockSpec((None, None, n, wide), lambda h, b: (h, b, 0, 0)),
                  pl.BlockSpec((2, wide), lambda h, b: (0, 0)),
                  pl.BlockSpec((2 * wide, HEAD_DIM), lambda h, b: (0, 0)),
                  pl.BlockSpec((HEAD_DIM, HEAD_DIM), lambda h, b: (0, 0))],
        out_specs=pl.BlockSpec((None, None) + out_shape[2:], lambda h, b: (h, b, 0, 0)),
        out_shape=jax.ShapeDtypeStruct(out_shape, BF16),
        compiler_params=_cparams(("parallel", "parallel")),
        name="nsa_compress",
    )(slabs, pe.reshape(2, wide), w1, w2_in)


def _query_cols(q_ref):
    return jnp.concatenate([q_ref[g * HEAD_DIM:(g + 1) * HEAD_DIM, :] for g in range(GROUP)], axis=1)


def _cmp_body(ib, n_tiles, q_ref, kc_ref, vct_ref, tc_ref, ov_ref, oc_ref, mb_ref):
    n_cmp = n_tiles * CMP_TILE
    n_sel = n_cmp * CMP_STRIDE // SEL_LEN
    qt = _query_cols(q_ref)
    s = jnp.dot(kc_ref[:n_cmp], qt, preferred_element_type=F32)
    tiles = []
    for cb in range(n_tiles):
        idx = jnp.clip(ib - (CMP_TILE * CMP_STRIDE // Q_BLOCK) * cb + 1, 0, CMP_NEAR + 1)
        tiles.append(s[cb * CMP_TILE:(cb + 1) * CMP_TILE] + tc_ref[idx])
    s = jnp.concatenate(tiles, axis=0)
    valid = s > 0.5 * NEG
    m = jnp.max(s, axis=0, keepdims=True)
    p = jnp.where(valid, jnp.exp2(s - m), 0.0)
    p = p / jnp.maximum(jnp.sum(p, axis=0, keepdims=True), 1e-30)
    oc_ref[...] = jnp.dot(vct_ref[:, :n_cmp], p.astype(BF16), preferred_element_type=F32)
    ov = ov_ref[:n_sel, :n_cmp]

    psum = p[:, 0:Q_BLOCK]
    for g in range(1, GROUP):
        psum = psum + p[:, g * Q_BLOCK:(g + 1) * Q_BLOCK]
    p_hi = psum.astype(BF16)
    p_lo = (psum - p_hi.astype(F32)).astype(BF16)
    imp = (jnp.dot(ov, p_hi, preferred_element_type=F32)
           + jnp.dot(ov, p_lo, preferred_element_type=F32))
    blk = lax.broadcasted_iota(jnp.int32, (n_sel, Q_BLOCK), 0)
    t = ib * Q_BLOCK + lax.broadcasted_iota(jnp.int32, (n_sel, Q_BLOCK), 1)
    cur = jnp.right_shift(t, int(math.log2(SEL_LEN)))
    ok = blk * SEL_LEN <= t
    forced = ok & ((blk == 0) | (blk == cur) | (blk == cur - 1))
    score = jnp.where(forced, BIG, jnp.where(ok, imp, NEG))
    taken = jnp.zeros((n_sel, Q_BLOCK), jnp.bool_)
    for _ in range(SEL_TOP):
        top = jnp.max(score, axis=0, keepdims=True)
        first = jnp.min(jnp.where(score == top, blk, n_sel), axis=0, keepdims=True)
        pick = blk == first
        taken = taken | pick
        score = jnp.where(pick, -jnp.inf, score)
    mb_ref[:n_sel] = jnp.where(taken & ok, 0.0, NEG).astype(BF16)
    if n_sel < mb_ref.shape[0]:
        mb_ref[n_sel:] = jnp.full((mb_ref.shape[0] - n_sel, Q_BLOCK), NEG, BF16)


def _cmp_kernel(*refs, n_tiles):
    ib = pl.program_id(2)
    per_tile = CMP_TILE * CMP_STRIDE // Q_BLOCK
    lo = 0
    for nt in range(1, n_tiles + 1):
        hi = nt * per_tile

        @pl.when((ib >= lo) & (ib < hi))
        def _():
            _cmp_body(ib, nt, *refs)
        lo = hi


def _cmp_attention(qt, kc, vct, tc, overlap_t, *, B, S):
    nq = S // Q_BLOCK
    n_cmp = S // CMP_STRIDE
    n_sel = S // SEL_LEN
    n_tiles = n_cmp // CMP_TILE
    return pl.pallas_call(
        functools.partial(_cmp_kernel, n_tiles=n_tiles),
        grid=(N_KV, B, nq),
        in_specs=[pl.BlockSpec((GROUP * HEAD_DIM, Q_BLOCK), lambda h, b, i: (h, b * nq + i)),
                  pl.BlockSpec((None, None, n_cmp, HEAD_DIM), lambda h, b, i: (h, b, 0, 0)),
                  pl.BlockSpec((None, None, HEAD_DIM, n_cmp), lambda h, b, i: (h, b, 0, 0)),
                  pl.BlockSpec((None, CMP_NEAR + 2, CMP_TILE, QG), lambda h, b, i: (h, 0, 0, 0)),
                  pl.BlockSpec((n_sel, n_cmp), lambda h, b, i: (0, 0))],
        out_specs=[pl.BlockSpec((None, None, HEAD_DIM, QG), lambda h, b, i: (h, b * nq + i, 0, 0)),
                   pl.BlockSpec((None, None, n_sel, Q_BLOCK), lambda h, b, i: (h, b, 0, i))],
        out_shape=[jax.ShapeDtypeStruct((N_KV, B * nq, HEAD_DIM, QG), F32),
                   jax.ShapeDtypeStruct((N_KV, B, n_sel, S), BF16)],
        compiler_params=_cparams(("parallel", "parallel", "parallel")),
        name="nsa_cmp_select",
    )(qt, kc, vct, tc, overlap_t)


def _sel_kernel(q_ref, mb_ref, k_ref, oh_ref, vt_ref, ts_ref, o_ref, m_ref, acc_ref, sa_ref, sb_ref):
    ib = pl.program_id(2)
    per = SEL_CHUNK // Q_BLOCK
    cq = ib // per
    sub = ib % per
    qt = _query_cols(q_ref)
    blocks_per_chunk = SEL_CHUNK // SEL_LEN
    chunks_per_group = LANE // blocks_per_chunk

    m_ref[...] = jnp.full(m_ref.shape, NEG, F32)
    acc_ref[...] = jnp.zeros_like(acc_ref)

    qq = cq // SEL_FAR
    rows = SEL_FAR * SEL_CHUNK

    def raw_scores(quad):
        c0 = pl.multiple_of(quad * SEL_FAR, SEL_FAR)
        grp = c0 // chunks_per_group
        mb = mb_ref[pl.ds(pl.multiple_of(grp * LANE, LANE), LANE), :]
        qa = jnp.concatenate([qt, jnp.concatenate([mb] * GROUP, axis=1)], axis=0)
        o0 = pl.multiple_of(c0 % oh_ref.shape[0], SEL_FAR)
        keys = jnp.concatenate([k_ref[pl.ds(c0, SEL_FAR)].reshape(rows, HEAD_DIM),
                                oh_ref[pl.ds(o0, SEL_FAR)].reshape(rows, HEAD_DIM)], axis=1)
        return jnp.dot(keys, qa, preferred_element_type=F32)

    def update(s, quad):
        c0 = quad * SEL_FAR
        v = jnp.concatenate([vt_ref[c0 + k] for k in range(SEL_FAR)], axis=1)
        m_old = m_ref[0:1, :]
        m_new = jnp.maximum(m_old, jnp.max(s, axis=0, keepdims=True))
        p = jnp.exp2((s - m_new).astype(BF16))
        acc_ref[...] = (jnp.exp2(m_old - m_new) * acc_ref[...]
                        + jnp.dot(v, p, preferred_element_type=F32))
        m_ref[...] = jnp.broadcast_to(m_new, m_ref.shape)

    def bias(quad, live):
        tiles = []
        for k in range(SEL_FAR):
            near = jnp.clip(quad * SEL_FAR + k - cq + SEL_NEAR, 0, SEL_NEAR + 1)
            tiles.append(ts_ref[sub, jnp.where(live, near, SEL_NEAR + 1)])
        return jnp.concatenate(tiles, axis=0)

    def finish(near0_ref):
        s1 = raw_scores(qq)
        update(near0_ref[...] + bias(qq - 1, qq >= 1), jnp.maximum(qq - 1, 0))
        update(s1 + bias(qq, True), qq)
        o_ref[...] = acc_ref[:HEAD_DIM] / jnp.maximum(acc_ref[HEAD_DIM:HEAD_DIM + 1], 1e-30)

    n_far = jnp.maximum(qq - 1, 0)
    sa_ref[...] = raw_scores(0)

    def far_pair(i, carry):
        sb_ref[...] = raw_scores(2 * i + 1)
        update(sa_ref[...], 2 * i)
        sa_ref[...] = raw_scores(2 * i + 2)
        update(sb_ref[...], 2 * i + 1)
        return carry

    lax.fori_loop(0, n_far // 2, far_pair, 0)

    @pl.when(n_far % 2 == 1)
    def _():
        sb_ref[...] = raw_scores(n_far)
        update(sa_ref[...], n_far - 1)
        finish(sb_ref)

    @pl.when(n_far % 2 == 0)
    def _():
        finish(sa_ref)


def _sel_attention(qt, mbt, k_sel, onehot, vt, ts, *, B, S):
    nq = S // Q_BLOCK
    n_sel = mbt.shape[2]
    nc = S // SEL_CHUNK
    return pl.pallas_call(
        _sel_kernel,
        grid=(N_KV, B, nq),
        in_specs=[pl.BlockSpec((GROUP * HEAD_DIM, Q_BLOCK), lambda h, b, i: (h, b * nq + i)),
                  pl.BlockSpec((None, None, n_sel, Q_BLOCK), lambda h, b, i: (h, b, 0, i)),
                  pl.BlockSpec((None, None, nc, SEL_CHUNK, HEAD_DIM), lambda h, b, i: (h, b, 0, 0, 0)),
                  pl.BlockSpec(onehot.shape, lambda h, b, i: (0, 0, 0)),
                  pl.BlockSpec((None, None, nc, V_ROWS, SEL_CHUNK), lambda h, b, i: (h, b, 0, 0, 0)),
                  pl.BlockSpec((None, SEL_CHUNK // Q_BLOCK, SEL_NEAR + 2, SEL_CHUNK, QG),
                               lambda h, b, i: (h, 0, 0, 0, 0))],
        out_specs=pl.BlockSpec((None, None, HEAD_DIM, QG), lambda h, b, i: (h, b * nq + i, 0, 0)),
        out_shape=jax.ShapeDtypeStruct((N_KV, B * nq, HEAD_DIM, QG), F32),
        scratch_shapes=[pltpu.VMEM((8, QG), F32), pltpu.VMEM((V_ROWS, QG), F32),
                        pltpu.VMEM((SEL_FAR * SEL_CHUNK, QG), F32),
                        pltpu.VMEM((SEL_FAR * SEL_CHUNK, QG), F32)],
        compiler_params=_cparams(("parallel", "parallel", "arbitrary")),
        name="nsa_selected",
    )(qt, mbt, k_sel, onehot, vt, ts)


def _win_kernel(*refs):
    nb = WINDOW // Q_BLOCK + 1
    q_ref = refs[0]
    k_refs = refs[1:1 + nb]
    v_refs = refs[1 + nb:1 + 2 * nb]
    tw_ref, gate_ref, oc_ref, os_ref, o_ref = refs[1 + 2 * nb:]
    ib = pl.program_id(2)
    qt = _query_cols(q_ref)
    parts = []
    for i in range(nb):
        s = jnp.dot(k_refs[i][...], qt, preferred_element_type=F32)
        s = s + tw_ref[i * Q_BLOCK:(i + 1) * Q_BLOCK, :]
        parts.append(jnp.where(ib - (nb - 1) + i >= 0, s, NEG))
    s = jnp.concatenate(parts, axis=0)
    m = jnp.max(s, axis=0, keepdims=True)
    pb = jnp.exp2((s - m).astype(BF16))
    ow = jnp.dot(v_refs[0][...], pb[0:Q_BLOCK], preferred_element_type=F32)
    for i in range(1, nb):
        ow = ow + jnp.dot(v_refs[i][...], pb[i * Q_BLOCK:(i + 1) * Q_BLOCK], preferred_element_type=F32)
    ow = ow[:HEAD_DIM] / jnp.maximum(ow[HEAD_DIM:HEAD_DIM + 1], 1e-30)

    def gate(j):
        return jnp.concatenate([gate_ref[j, g:g + 1, :] for g in range(GROUP)], axis=1)

    o = gate(0) * oc_ref[...] + gate(1) * os_ref[...] + gate(2) * ow
    for g in range(GROUP):
        o_ref[:, g * HEAD_DIM:(g + 1) * HEAD_DIM] = o[:, g * Q_BLOCK:(g + 1) * Q_BLOCK].T.astype(BF16)


def _win_attention(qt, kw, vwt, tw, gates_t, oc, osel, *, B, S):
    nq = S // Q_BLOCK
    nb = WINDOW // Q_BLOCK + 1

    def kmap(i_off):
        return lambda h, b, i: (N_KV + h, b * nq + jnp.maximum(i - (nb - 1) + i_off, 0), 0)

    def vmap_(i_off):
        return lambda h, b, i: (h, 0, b * nq + jnp.maximum(i - (nb - 1) + i_off, 0))

    blk = pl.BlockSpec((None, None, HEAD_DIM, QG), lambda h, b, i: (h, b * nq + i, 0, 0))
    return pl.pallas_call(
        _win_kernel,
        grid=(N_KV, B, nq),
        in_specs=([pl.BlockSpec((GROUP * HEAD_DIM, Q_BLOCK), lambda h, b, i: (h, b * nq + i))]
                  + [pl.BlockSpec((None, Q_BLOCK, HEAD_DIM), kmap(j)) for j in range(nb)]
                  + [pl.BlockSpec((None, V_ROWS, Q_BLOCK), vmap_(j)) for j in range(nb)]
                  + [pl.BlockSpec((None, WINDOW + Q_BLOCK, QG), lambda h, b, i: (h, 0, 0)),
                     pl.BlockSpec((3, None, GROUP, Q_BLOCK), lambda h, b, i: (0, h, 0, b * nq + i)),
                     blk, blk]),
        out_specs=pl.BlockSpec((Q_BLOCK, GROUP * HEAD_DIM), lambda h, b, i: (b * nq + i, h)),
        out_shape=jax.ShapeDtypeStruct((B * S, N_HEADS * HEAD_DIM), BF16),
        compiler_params=_cparams(("parallel", "parallel", "parallel")),
        name="nsa_window_merge",
    )(qt, *([kw] * nb), *([vwt] * nb), tw, gates_t, oc, osel)


def _nsa_layer(x32, xb, p, tables, consts, ln_g, ln_b, *, B, S):
    T = B * S
    tc, ts, tw = tables
    overlap_t, sel_onehot = consts
    kd = N_KV * HEAD_DIM
    qv = _proj_nt(p["w_qv_t"], xb, out_dtype=BF16, scale=HEAD_DIM ** -0.5 * LOG2E,
                  n_scaled=N_HEADS * HEAD_DIM // 512)
    qt = qv
    k2 = _proj_nn(xb, p["w_k"], out_dtype=BF16, head_major=True)
    cmp = _proj_nn(xb, p["w_cmp"], out_dtype=F32, head_major=True)
    gates_t = _proj_nt(p["w_gate_t"], xb, out_dtype=F32, act="sigmoid")
    gates_t = gates_t.reshape(3, N_KV, GROUP, T)

    cmp = cmp.reshape(2, N_KV, B, S // CMP_STRIDE, CMP_STRIDE * HEAD_DIM)
    kc = _compress(cmp[0], p["pe_k"], p["w1_k"], p["w2_k"], B=B, S=S, transposed=False)
    vct = _compress(cmp[1], p["pe_v"], p["w1_v"], p["w2_v"], B=B, S=S, transposed=True)
    oc, mbt = _cmp_attention(qt, kc, vct, tc, overlap_t, B=B, S=S)
    if mbt.shape[2] % LANE:
        mbt = jnp.pad(mbt, ((0, 0), (0, 0), (0, -mbt.shape[2] % LANE), (0, 0)), constant_values=NEG)

    nc = S // SEL_CHUNK
    k_sel = k2.reshape(2 * N_KV, B, nc, SEL_CHUNK, HEAD_DIM)
    v_sel_t = qv[N_HEADS * HEAD_DIM:N_HEADS * HEAD_DIM + kd].reshape(N_KV, HEAD_DIM, B, nc, SEL_CHUNK)
    v_sel_t = v_sel_t.transpose(0, 2, 3, 1, 4)
    ones_rows = (jnp.arange(V_ROWS - HEAD_DIM) == 0).astype(BF16)[:, None]
    v_sel_t = jnp.concatenate(
        [v_sel_t, jnp.broadcast_to(ones_rows, v_sel_t.shape[:3] + (V_ROWS - HEAD_DIM, SEL_CHUNK))], axis=3)
    osel = _sel_attention(qt, mbt, k_sel, sel_onehot, v_sel_t, ts, B=B, S=S)

    v_win_t = qv[N_HEADS * HEAD_DIM + kd:].reshape(N_KV, HEAD_DIM, T)
    v_win_t = jnp.concatenate(
        [v_win_t, jnp.broadcast_to(ones_rows, (N_KV, V_ROWS - HEAD_DIM, T))], axis=1)
    attn = _win_attention(qt, k2, v_win_t, tw, gates_t, oc, osel, B=B, S=S)
    return _proj_ln(attn, p["w_out"], x32, ln_g, ln_b)


def _nsa_params(w_in, pe_k, w1_k, w2_k, pe_v, w1_v, w2_v, w_out):
    D = w_in.shape[0]
    qd, kd = N_HEADS * HEAD_DIM, N_KV * HEAD_DIM
    w_kv = w_in[:, qd:qd + 6 * kd].reshape(D, 6, kd)
    gcol = np.array([[[(kv * GROUP + g) * 3 + j for g in range(GROUP)] for kv in range(N_KV)]
                     for j in range(3)]).reshape(-1)
    return dict(
        w_qv_t=jnp.concatenate([w_in[:, :qd], w_kv[:, 3], w_kv[:, 5]], axis=1).T.astype(BF16),
        w_k=jnp.concatenate([w_kv[:, 2], w_kv[:, 4]], axis=1).astype(BF16),
        w_cmp=jnp.concatenate([w_kv[:, 0], w_kv[:, 1]], axis=1).astype(BF16),
        w_gate_t=w_in[:, qd + 6 * kd:][:, gcol].T.astype(BF16),
        pe_k=pe_k, w1_k=w1_k.astype(BF16), w2_k=w2_k.astype(BF16),
        pe_v=pe_v, w1_v=w1_v.astype(BF16), w2_v=w2_v.astype(BF16),
        w_out=w_out.astype(BF16))


def _nsa_constants(S):
    n_cmp, n_sel = S // CMP_STRIDE, S // SEL_LEN
    cj = np.arange(n_cmp) * CMP_STRIDE
    sb = np.arange(n_sel) * SEL_LEN
    ov = np.maximum(np.minimum(cj[None, :] + CMP_LEN, sb[:, None] + SEL_LEN)
                    - np.maximum(cj[None, :], sb[:, None]), 0).astype(np.float32) / CMP_LEN
    ov[:, n_cmp - 1] = 0.0
    key = np.arange(SEL_CHUNK)
    chunk = np.arange(min(S // SEL_CHUNK, LANE * SEL_LEN // SEL_CHUNK))
    blk = (chunk[:, None] * (SEL_CHUNK // SEL_LEN) + key[None, :] // SEL_LEN) % LANE
    onehot = (blk[:, :, None] == np.arange(LANE)[None, None, :]).astype(np.float32)
    return jnp.asarray(ov, BF16), jnp.asarray(onehot, BF16)


def kernel(x, rel_bias, router_w, router_b, ln1_g, ln1_b, ln2_g, ln2_b, lru_w_in, lru_conv_w,
           lru_conv_b, lru_w_a, lru_b_a, lru_w_i, lru_b_i, lru_lambda, lru_w_out, nsa_w_in,
           nsa_pe_k, nsa_w1_k, nsa_w2_k, nsa_pe_v, nsa_w1_v, nsa_w2_v, nsa_w_out,
           moe_w_gate, moe_w_up, moe_w_down):
    B, S, D = x.shape
    T = B * S
    assert D == D_MODEL and S % (CMP_TILE * CMP_STRIDE) == 0 and S // SEL_LEN >= SEL_TOP
    qd, kd = N_HEADS * HEAD_DIM, N_KV * HEAD_DIM
    tables = _bias_tables(rel_bias, S)
    consts = _nsa_constants(S)
    router_wt = router_w.T

    x32 = x.reshape(T, D)
    xb = x32.astype(BF16)
    for layer in range(DEPTH):
        j = layer // 2
        if layer % 2 == 0:
            w_in = lru_w_in[j].astype(BF16)
            p = dict(w_gate=w_in[:, :D], w_y=w_in[:, D:], conv_w=lru_conv_w[j], conv_b=lru_conv_b[j],
                     w_a=lru_w_a[j].astype(BF16), b_a=lru_b_a[j], w_i=lru_w_i[j].astype(BF16),
                     b_i=lru_b_i[j], lam=lru_lambda[j], w_out=lru_w_out[j].astype(BF16))
            x32, xb = _rglru_layer(x32, xb, p, ln1_g[layer], ln1_b[layer], B=B, S=S)
        else:
            p = _nsa_params(nsa_w_in[j], nsa_pe_k[j], nsa_w1_k[j], nsa_w2_k[j], nsa_pe_v[j],
                            nsa_w1_v[j], nsa_w2_v[j], nsa_w_out[j])
            x32, xb = _nsa_layer(x32, xb, p, tables, consts, ln1_g[layer], ln1_b[layer], B=B, S=S)
        x32, xb = _moe_layer(x32, router_wt, router_b, moe_w_gate[layer].astype(BF16),
                             moe_w_up[layer].astype(BF16), moe_w_down[layer].astype(BF16),
                             ln2_g[layer], ln2_b[layer])
    return x32.reshape(B, S, D)
```

```python
import functools
import math

import numpy as np
import jax
import jax.numpy as jnp
from jax import lax
from jax.experimental import pallas as pl
from jax.experimental.pallas import tpu as pltpu

F32 = jnp.float32
BF16 = jnp.bfloat16

D_MODEL = 2048
DEPTH = 4
LRU_BLOCK = 256
LRU_NBLK = D_MODEL // LRU_BLOCK
CONV_W = 4
LRU_C = 8.0
HEAD_DIM = 128
N_HEADS = 16
N_KV = 4
GROUP = 4
CMP_LEN = 32
CMP_STRIDE = 16
SEL_LEN = 64
SEL_TOP = 16
WINDOW = 512
Q_BLOCK = 128
NUM_BUCKETS = 32
MAX_DISTANCE = 1024
N_EXPERTS = 16
N_GROUPS = 4
EXPERTS_PER_GROUP = 4
D_EXPERT = 1024
ALPHA = (2 * DEPTH) ** 0.25
LN_EPS = 1e-5
NEG = -1e30
BIG = 1e9

LANE = 128
QG = GROUP * Q_BLOCK
EXPERT_ROWS = 256
SEL_CHUNK = 256
SEL_NEAR = 5
SEL_FAR = 4
V_ROWS = HEAD_DIM + 16
LOG2E = math.log2(math.e)
CMP_TILE = 128
CMP_NEAR = 23
VMEM_LIMIT = 56 << 20


def _cparams(sem, vmem=VMEM_LIMIT):
    return pltpu.CompilerParams(dimension_semantics=sem, vmem_limit_bytes=vmem)


def _act(x, act):
    if act == "gelu":
        return jax.nn.gelu(x)
    if act == "sigmoid":
        return jax.nn.sigmoid(x)
    return x


def _proj_nn_kernel(x_ref, w_ref, o_ref, *, act, head_major):
    acc = jnp.dot(x_ref[...], w_ref[...], preferred_element_type=F32)
    acc = _act(acc, act)
    if head_major:
        for h in range(o_ref.shape[0]):
            o_ref[h] = acc[:, h * LANE:(h + 1) * LANE].astype(o_ref.dtype)
    else:
        o_ref[...] = acc.astype(o_ref.dtype)


def _proj_nn(x, w, *, out_dtype, act=None, head_major=False, tm=1024, tn=1024):
    T, K = x.shape
    N = w.shape[1]
    tm, tn = min(tm, T), min(tn, N)
    if head_major:
        out_shape = (N // LANE, T, LANE)
        out_spec = pl.BlockSpec((tn // LANE, tm, LANE), lambda j, i: (j, i, 0))
    else:
        out_shape = (T, N)
        out_spec = pl.BlockSpec((tm, tn), lambda j, i: (i, j))
    return pl.pallas_call(
        functools.partial(_proj_nn_kernel, act=act, head_major=head_major),
        grid=(N // tn, T // tm),
        in_specs=[pl.BlockSpec((tm, K), lambda j, i: (i, 0)),
                  pl.BlockSpec((K, tn), lambda j, i: (0, j))],
        out_specs=out_spec,
        out_shape=jax.ShapeDtypeStruct(out_shape, out_dtype),
        compiler_params=_cparams(("parallel", "parallel")),
        name="proj_nn",
    )(x, w)


def _proj_nt_kernel(wt_ref, x_ref, o_ref, *, act, scale, n_scaled):
    acc = lax.dot_general(wt_ref[...], x_ref[...], (((1,), (1,)), ((), ())),
                          preferred_element_type=F32)
    if n_scaled:
        acc = acc * jnp.where(pl.program_id(0) < n_scaled, scale, 1.0).astype(F32)
    o_ref[...] = _act(acc, act).astype(o_ref.dtype)


def _proj_nt(wt, x, *, out_dtype, act=None, scale=1.0, n_scaled=0, tm=1024, tn=512):
    N, K = wt.shape
    T = x.shape[0]
    tm, tn = min(tm, T), min(tn, N)
    return pl.pallas_call(
        functools.partial(_proj_nt_kernel, act=act, scale=scale, n_scaled=n_scaled),
        grid=(N // tn, T // tm),
        in_specs=[pl.BlockSpec((tn, K), lambda j, i: (j, 0)),
                  pl.BlockSpec((tm, K), lambda j, i: (i, 0))],
        out_specs=pl.BlockSpec((tn, tm), lambda j, i: (j, i)),
        out_shape=jax.ShapeDtypeStruct((N, T), out_dtype),
        compiler_params=_cparams(("parallel", "parallel")),
        name="proj_nt",
    )(wt, x)


def _layer_norm_rows(z, g, b):
    mu = jnp.mean(z, axis=-1, keepdims=True)
    zc = z - mu
    var = jnp.mean(zc * zc, axis=-1, keepdims=True)
    return zc * lax.rsqrt(var + LN_EPS) * g + b


def _proj_ln_kernel(a_ref, w_ref, res_ref, g_ref, b_ref, rw_ref, rb_ref, of_ref, ob_ref, lg_ref):
    acc = jnp.dot(a_ref[...], w_ref[...], preferred_element_type=F32)
    y = _layer_norm_rows(ALPHA * res_ref[...] + acc, g_ref[...], b_ref[...])
    of_ref[...] = y
    ob_ref[...] = y.astype(BF16)
    lg_ref[...] = jnp.dot(y, rw_ref[...], precision=lax.Precision.HIGHEST,
                          preferred_element_type=F32) + rb_ref[...]


def _proj_ln(a, w, res, g, b, router_w, router_b, *, tm=256):
    T, K = a.shape
    D = w.shape[1]
    tm = min(tm, T)
    row = pl.BlockSpec((tm, D), lambda i: (i, 0))
    vec = pl.BlockSpec((1, D), lambda i: (0, 0))
    lgt = pl.BlockSpec((tm, LANE), lambda i: (i, 0))
    return pl.pallas_call(
        _proj_ln_kernel,
        grid=(T // tm,),
        in_specs=[pl.BlockSpec((tm, K), lambda i: (i, 0)),
                  pl.BlockSpec((K, D), lambda i: (0, 0)),
                  row, vec, vec,
                  pl.BlockSpec((D, LANE), lambda i: (0, 0)),
                  pl.BlockSpec((1, LANE), lambda i: (0, 0))],
        out_specs=[row, row, lgt],
        out_shape=[jax.ShapeDtypeStruct((T, D), F32), jax.ShapeDtypeStruct((T, D), BF16),
                   jax.ShapeDtypeStruct((T, LANE), F32)],
        compiler_params=_cparams(("parallel",)),
        name="proj_ln",
    )(a, w, res, g.reshape(1, D), b.reshape(1, D), router_w, router_b)


def _lru_kernel(y_ref, gate_ref, cw_ref, cb_ref, wa_ref, ba_ref, wi_ref, bi_ref, lam_ref,
                o_ref, h_ref, tail_ref, *, rows):
    @pl.when(pl.program_id(2) == 0)
    def _():
        h_ref[...] = jnp.zeros_like(h_ref)
        tail_ref[...] = jnp.zeros_like(tail_ref)

    y = y_ref[...]
    tail = tail_ref[...]
    cw = cw_ref[...]
    row8 = lax.broadcasted_iota(jnp.int32, (8, LRU_BLOCK), 0)
    yc = cb_ref[...] + y * cw[CONV_W - 1:CONV_W, :]
    for d in range(1, CONV_W):
        ys = pltpu.roll(y, d, 0)
        first = jnp.where(row8 < d, pltpu.roll(tail, d, 0), ys[:8])
        ys = jnp.concatenate([first, ys[8:]], axis=0)
        yc = yc + ys * cw[CONV_W - 1 - d:CONV_W - d, :]
    tail_ref[...] = y[rows - 8:]

    yb = yc.astype(BF16)
    r = jax.nn.sigmoid(jnp.dot(yb, wa_ref[...], preferred_element_type=F32) + ba_ref[...])
    ig = jax.nn.sigmoid(jnp.dot(yb, wi_ref[...], preferred_element_type=F32) + bi_ref[...])
    z = -lam_ref[...]
    softplus = jnp.maximum(z, 0.0) + jnp.log1p(jnp.exp(-jnp.abs(z)))
    log_a = (-LRU_C * softplus) * r
    a = jnp.exp(log_a)
    h = jnp.sqrt(1.0 - a * a) * (ig * yc)

    rowi = lax.broadcasted_iota(jnp.int32, (rows, LRU_BLOCK), 0)
    s = 1
    while s < rows:
        if s < 8:
            keep = rowi >= s
            h = jnp.where(keep, h + a * pltpu.roll(h, s, 0), h)
            a = jnp.where(keep, a * pltpu.roll(a, s, 0), a)
        else:
            h = jnp.concatenate([h[:s], h[s:] + a[s:] * h[:rows - s]], axis=0)
            a = jnp.concatenate([a[:s], a[s:] * a[:rows - s]], axis=0)
        s *= 2
    h = h + a * h_ref[0:1, :]
    h_ref[...] = jnp.broadcast_to(h[rows - 1:rows, :], h_ref.shape)
    o_ref[...] = (h * gate_ref[...].astype(F32)).astype(BF16)


def _lru_core(y, gate, conv_w, conv_b, w_a, b_a, w_i, b_i, lam, *, B, S, rows=512):
    T, W = y.shape
    rows = min(rows, S)
    ns = S // rows
    blk = pl.BlockSpec((rows, LRU_BLOCK), lambda n, b, s: (b * ns + s, n))
    vec = pl.BlockSpec((1, LRU_BLOCK), lambda n, b, s: (0, n))
    wblk = pl.BlockSpec((None, LRU_BLOCK, LRU_BLOCK), lambda n, b, s: (n, 0, 0))
    return pl.pallas_call(
        functools.partial(_lru_kernel, rows=rows),
        grid=(LRU_NBLK, B, ns),
        in_specs=[blk, blk,
                  pl.BlockSpec((CONV_W, LRU_BLOCK), lambda n, b, s: (0, n)), vec,
                  wblk, vec, wblk, vec, vec],
        out_specs=blk,
        out_shape=jax.ShapeDtypeStruct((T, W), BF16),
        scratch_shapes=[pltpu.VMEM((8, LRU_BLOCK), F32), pltpu.VMEM((8, LRU_BLOCK), F32)],
        compiler_params=_cparams(("parallel", "parallel", "arbitrary")),
        name="lru_core",
    )(y, gate, conv_w, conv_b.reshape(1, W), w_a, b_a.reshape(1, W), w_i, b_i.reshape(1, W),
      lam.reshape(1, W))


def _rglru_layer(x32, xb, p, ln_g, ln_b, *, B, S):
    gate = _proj_nn(xb, p["w_gate"], out_dtype=BF16, act="gelu")
    y = _proj_nn(xb, p["w_y"], out_dtype=F32)
    hg = _lru_core(y, gate, p["conv_w"], p["conv_b"], p["w_a"], p["b_a"], p["w_i"], p["b_i"],
                   p["lam"], B=B, S=S)
    return _proj_ln(hg, p["w_out"], x32, ln_g, ln_b, *p["router"])


def _router_kernel(lg_ref, ids_ref, wts_ref, pos_ref, cnt_ref, carry_ref, *, tm):
    @pl.when(pl.program_id(0) == 0)
    def _():
        carry_ref[...] = jnp.zeros_like(carry_ref)

    logits = lg_ref[...].T[:N_EXPERTS]
    m = jnp.max(logits, axis=0, keepdims=True)
    e = jnp.exp(logits - m)
    p = e / jnp.sum(e, axis=0, keepdims=True)
    rows = [p[k:k + 1, :] for k in range(N_EXPERTS)]

    scores = []
    for g in range(N_GROUPS):
        a, b, c, d = rows[4 * g:4 * g + 4]
        hi1, lo1, hi2, lo2 = jnp.maximum(a, b), jnp.minimum(a, b), jnp.maximum(c, d), jnp.minimum(c, d)
        scores.append(jnp.maximum(hi1, hi2) + jnp.maximum(jnp.minimum(hi1, hi2), jnp.maximum(lo1, lo2)))
    best, gsel = scores[0], jnp.zeros((1, tm), jnp.int32)
    for g in range(1, N_GROUPS):
        upd = scores[g] > best
        gsel = jnp.where(upd, g, gsel)
        best = jnp.where(upd, scores[g], best)

    cand = []
    for k in range(EXPERTS_PER_GROUP):
        c = rows[k]
        for g in range(1, N_GROUPS):
            c = jnp.where(gsel == g, rows[4 * g + k], c)
        cand.append(c)
    v1, i1 = cand[0], jnp.zeros((1, tm), jnp.int32)
    for k in range(1, EXPERTS_PER_GROUP):
        upd = cand[k] > v1
        i1 = jnp.where(upd, k, i1)
        v1 = jnp.where(upd, cand[k], v1)
    v2, i2 = jnp.full((1, tm), -1.0, F32), jnp.zeros((1, tm), jnp.int32)
    for k in range(EXPERTS_PER_GROUP):
        c = jnp.where(i1 == k, -1.0, cand[k])
        upd = c > v2
        i2 = jnp.where(upd, k, i2)
        v2 = jnp.where(upd, c, v2)
    e0 = gsel * EXPERTS_PER_GROUP + i1
    e1 = gsel * EXPERTS_PER_GROUP + i2
    tot = v1 + v2

    eio = lax.broadcasted_iota(jnp.int32, (N_EXPERTS, tm), 0)
    hit0, hit1 = eio == e0, eio == e1
    onehot = jnp.where(hit0 | hit1, 1.0, 0.0)
    before = (lax.broadcasted_iota(jnp.int32, (tm, tm), 0)
              < lax.broadcasted_iota(jnp.int32, (tm, tm), 1))
    rank = jnp.dot(onehot.astype(BF16), jnp.where(before, 1.0, 0.0).astype(BF16),
                   preferred_element_type=F32) + carry_ref[:, 0:1]
    pos0 = jnp.sum(jnp.where(hit0, rank, 0.0), axis=0, keepdims=True)
    pos1 = jnp.sum(jnp.where(hit1, rank, 0.0), axis=0, keepdims=True)
    carry_ref[...] = carry_ref[...] + jnp.sum(onehot, axis=1, keepdims=True)
    cnt_ref[...] = carry_ref[...]

    zi = jnp.zeros((6, tm), jnp.int32)
    ids_ref[...] = jnp.concatenate([e0, e1, zi], axis=0)
    pos_ref[...] = jnp.concatenate([pos0.astype(jnp.int32), pos1.astype(jnp.int32), zi], axis=0)
    wts_ref[...] = jnp.concatenate([v1 / tot, v2 / tot, jnp.zeros((6, tm), F32)], axis=0)


def _router(logits, *, tm=512):
    T = logits.shape[0]
    tm = min(tm, T)
    tok = pl.BlockSpec((8, tm), lambda i: (0, i))
    return pl.pallas_call(
        functools.partial(_router_kernel, tm=tm),
        grid=(T // tm,),
        in_specs=[pl.BlockSpec((tm, LANE), lambda i: (i, 0))],
        out_specs=[tok, tok, tok, pl.BlockSpec((N_EXPERTS, LANE), lambda i: (0, 0))],
        out_shape=[jax.ShapeDtypeStruct((8, T), jnp.int32), jax.ShapeDtypeStruct((8, T), F32),
                   jax.ShapeDtypeStruct((8, T), jnp.int32),
                   jax.ShapeDtypeStruct((N_EXPERTS, LANE), F32)],
        scratch_shapes=[pltpu.VMEM((N_EXPERTS, LANE), F32)],
        compiler_params=_cparams(("arbitrary",)),
        name="moe_router",
    )(logits)


def _router_params(router_w, router_b):
    pad = LANE - N_EXPERTS
    return (jnp.pad(router_w, ((0, 0), (0, pad))), jnp.pad(router_b, (0, pad)).reshape(1, LANE))


def _row_copy(src, src_row, dst, dst_row, sem, n=1):
    return pltpu.make_async_copy(src.at[pl.ds(src_row, n)], dst.at[pl.ds(dst_row, n)], sem)


def _row_tokens_kernel(n_ref, lo_ref, hi_ref, d0_ref, d1_ref, tok_ref):
    step = 8

    def clear(r, c):
        tok_ref[r] = 0
        return c

    def put(i, c):
        for k in range(step):
            t = i * step + k
            tok_ref[d0_ref[t]] = t
            tok_ref[d1_ref[t]] = t
        return c

    for e in range(lo_ref.shape[0]):
        lax.fori_loop(lo_ref[e], hi_ref[e], clear, 0)
    lax.fori_loop(0, n_ref[0] // step, put, 0)


def _row_tokens(d0, d1, free_lo, free_hi, n_rows):
    n_tok = d0.shape[0]
    assert n_tok % 8 == 0
    return pl.pallas_call(
        _row_tokens_kernel,
        grid_spec=pltpu.PrefetchScalarGridSpec(
            num_scalar_prefetch=5, grid=(1,), in_specs=[],
            out_specs=pl.BlockSpec(memory_space=pltpu.SMEM)),
        out_shape=jax.ShapeDtypeStruct((n_rows,), jnp.int32),
        compiler_params=pltpu.CompilerParams(dimension_semantics=("arbitrary",)),
        name="moe_row_tokens",
    )(jnp.array([n_tok], jnp.int32), free_lo, free_hi, d0, d1)


def _ffn_kernel(be_ref, nact_ref, tok_ref, x_hbm, wg_ref, wu_ref, wd_ref, y_ref, xbuf, sems):
    i = pl.program_id(0)
    n_act = nact_ref[0]
    R = EXPERT_ROWS

    def gather(block, slot):
        def issue(r, c):
            _row_copy(x_hbm, tok_ref[block * R + r], xbuf.at[slot], r, sems.at[slot]).start()
            return c
        lax.fori_loop(0, R, issue, 0, unroll=8)

    def run(slot):
        @pl.when(i + 1 < n_act)
        def _():
            gather(i + 1, 1 - slot)

        def drain(r, c):
            _row_copy(x_hbm, 0, xbuf.at[slot], 0, sems.at[slot]).wait()
            return c
        lax.fori_loop(0, R, drain, 0, unroll=8)
        xb = xbuf[slot].astype(BF16)
        g = jnp.dot(xb, wg_ref[...], preferred_element_type=F32)
        u = jnp.dot(xb, wu_ref[...], preferred_element_type=F32)
        h = (jax.nn.silu(g) * u).astype(BF16)
        y_ref[...] = jnp.dot(h, wd_ref[...], preferred_element_type=F32)

    @pl.when(i == 0)
    def _():
        gather(0, 0)

    for slot in range(2):
        @pl.when((i < n_act) & (i % 2 == slot))
        def _():
            run(slot)

    @pl.when(i >= n_act)
    def _():
        y_ref[...] = jnp.zeros_like(y_ref)


def _expert_ffn(x32, row_tok, blk_e, n_act, w_gate, w_up, w_down):
    D = x32.shape[1]
    R = EXPERT_ROWS
    n_rows = row_tok.shape[0]
    wmap = lambda i, be, na, tok: (be[i], 0, 0)
    return pl.pallas_call(
        _ffn_kernel,
        grid_spec=pltpu.PrefetchScalarGridSpec(
            num_scalar_prefetch=3, grid=(n_rows // R,),
            in_specs=[pl.BlockSpec(memory_space=pl.ANY),
                      pl.BlockSpec((None, D, D_EXPERT), wmap),
                      pl.BlockSpec((None, D, D_EXPERT), wmap),
                      pl.BlockSpec((None, D_EXPERT, D), wmap)],
            out_specs=pl.BlockSpec((R, D), lambda i, be, na, tok: (i, 0)),
            scratch_shapes=[pltpu.VMEM((2, R, D), F32), pltpu.SemaphoreType.DMA((2,))]),
        out_shape=jax.ShapeDtypeStruct((n_rows, D), F32),
        compiler_params=_cparams(("arbitrary",)),
        name="moe_ffn",
    )(blk_e, n_act, row_tok, x32, w_gate, w_up, w_down)


def _combine_kernel(d0_ref, d1_ref, res_ref, w_ref, g_ref, b_ref, y_hbm, of_ref, ob_ref,
                    buf, sem, *, tm):
    i = pl.program_id(0)

    def issue(r, c):
        t = i * tm + r
        _row_copy(y_hbm, d0_ref[t], buf.at[0], r, sem).start()
        _row_copy(y_hbm, d1_ref[t], buf.at[1], r, sem).start()
        return c

    def drain(r, c):
        _row_copy(y_hbm, 0, buf.at[0], 0, sem).wait()
        _row_copy(y_hbm, 0, buf.at[1], 0, sem).wait()
        return c

    lax.fori_loop(0, tm, issue, 0)
    lax.fori_loop(0, tm, drain, 0)
    w = w_ref[...]
    z = ALPHA * res_ref[...] + (w[:, 0:1] * buf[0] + w[:, 1:2] * buf[1])
    y = _layer_norm_rows(z, g_ref[...], b_ref[...])
    of_ref[...] = y
    ob_ref[...] = y.astype(BF16)


def _combine_ln(y, d0, d1, wts, res, g, b, *, tm=256):
    T, D = res.shape
    tm = min(tm, T)
    row = pl.BlockSpec((tm, D), lambda i, a, c: (i, 0))
    vec = pl.BlockSpec((1, D), lambda i, a, c: (0, 0))
    return pl.pallas_call(
        functools.partial(_combine_kernel, tm=tm),
        grid_spec=pltpu.PrefetchScalarGridSpec(
            num_scalar_prefetch=2, grid=(T // tm,),
            in_specs=[row, pl.BlockSpec((tm, 2), lambda i, a, c: (i, 0)), vec, vec,
                      pl.BlockSpec(memory_space=pl.ANY)],
            out_specs=[row, row],
            scratch_shapes=[pltpu.VMEM((2, tm, D), F32), pltpu.SemaphoreType.DMA(())]),
        out_shape=[jax.ShapeDtypeStruct((T, D), F32), jax.ShapeDtypeStruct((T, D), BF16)],
        compiler_params=_cparams(("arbitrary",)),
        name="moe_combine_ln",
    )(d0, d1, res, wts, g.reshape(1, D), b.reshape(1, D), y)


def _moe_layer(x32, logits, w_gate, w_up, w_down, ln_g, ln_b):
    T = x32.shape[0]
    R = EXPERT_ROWS
    ids, wts, pos, cnt = _router(logits)
    counts = cnt[:, 0].astype(jnp.int32)
    padded = (counts + R - 1) // R * R
    p_end = jnp.cumsum(padded)
    offs = p_end - padded
    d0 = offs[ids[0]] + pos[0]
    d1 = offs[ids[1]] + pos[1]
    n_blocks = 2 * T // R + N_EXPERTS
    n_act = p_end[-1] // R
    step = jnp.arange(n_blocks, dtype=jnp.int32)
    live = step < n_act
    src = jnp.where(live, step, n_act - 1)
    blk_e = jnp.minimum(jnp.sum(src[:, None] >= (p_end // R)[None, :], axis=1),
                        N_EXPERTS - 1).astype(jnp.int32)
    n_rows = n_blocks * R
    free_lo = jnp.concatenate([offs + counts, p_end[-1:]]).astype(jnp.int32)
    free_hi = jnp.concatenate([p_end, jnp.array([n_rows], p_end.dtype)]).astype(jnp.int32)
    row_tok = _row_tokens(d0, d1, free_lo, free_hi, n_rows)
    y = _expert_ffn(x32, row_tok, blk_e, n_act.reshape(1).astype(jnp.int32), w_gate, w_up, w_down)
    return _combine_ln(y, d0, d1, wts[:2].T, x32, ln_g, ln_b)


def _t5_bucket(dist):
    n = jnp.maximum(dist, 0)
    max_exact = NUM_BUCKETS // 2
    nf = jnp.maximum(n, 1).astype(F32)
    large = max_exact + (jnp.log(nf / max_exact) / math.log(MAX_DISTANCE / max_exact)
                         * (NUM_BUCKETS - max_exact)).astype(jnp.int32)
    large = jnp.minimum(large, NUM_BUCKETS - 1)
    return jnp.where(n < max_exact, n, large)


def _stack_heads(tbl):
    parts = tbl.reshape((N_KV, GROUP) + tbl.shape[1:])
    return jnp.concatenate([parts[:, g] for g in range(GROUP)], axis=-1)


def _bias_tables(rel_bias, S):
    off = CMP_STRIDE * CMP_TILE + Q_BLOCK
    n_dist = CMP_NEAR * Q_BLOCK
    bvec = rel_bias[_t5_bucket(jnp.arange(n_dist))].T
    last = rel_bias[NUM_BUCKETS - 1]
    masked = jnp.full((N_HEADS, off), NEG, F32)
    w_abs = jnp.concatenate([masked, bvec * LOG2E], axis=1)
    w_rel = jnp.concatenate([masked, (bvec - last[:, None]) * LOG2E], axis=1)
    w_win = jnp.where(jnp.arange(-off, n_dist) < WINDOW, w_abs, NEG)

    def toeplitz(w, first, step, n):
        lo = first - step * (n - 1) + off
        span = step * (n - 1) + Q_BLOCK
        u = w[:, lo:lo + span]
        width = span + step
        tiled = jnp.tile(u, (1, -(-n * width // span)))[:, :n * width]
        return tiled.reshape(N_HEADS, n, width)[:, ::-1, :Q_BLOCK]

    def hankel(u, n, m, step):
        span = step * (n - 1) + m
        width = span + step
        u = u[..., :span, :]
        tiled = jnp.tile(u, (1,) * (u.ndim - 2) + (-(-n * width // span), 1))[..., :n * width, :]
        return tiled.reshape(u.shape[:-2] + (n, width, u.shape[-1]))[..., :m, :]

    per = Q_BLOCK // CMP_STRIDE
    n_seq = per * (CMP_NEAR - 1) + (per - 1) + CMP_TILE
    lo = off - (CMP_LEN - 1) - CMP_STRIDE * (CMP_TILE - 1)
    seq = w_abs[:, lo:lo + CMP_STRIDE * n_seq].reshape(N_HEADS, n_seq, CMP_STRIDE)
    by_tile = hankel(seq, CMP_NEAR, per - 1 + CMP_TILE, per)
    near_c = hankel(by_tile, per, CMP_TILE, 1)
    near_c = near_c[:, :, :, ::-1, :].transpose(0, 1, 3, 2, 4).reshape(N_HEADS, CMP_NEAR, CMP_TILE, Q_BLOCK)
    const = lambda v: jnp.broadcast_to(v, (N_HEADS, 1, CMP_TILE, Q_BLOCK)).astype(F32)
    tc = jnp.concatenate([const(NEG), near_c, const(last[:, None, None, None] * LOG2E)], axis=1)
    zero = jnp.zeros((N_HEADS, 1, SEL_CHUNK, Q_BLOCK), F32)
    ts = jnp.stack([
        jnp.concatenate(
            [zero] + [toeplitz(w_rel, Q_BLOCK * sub + SEL_CHUNK * (SEL_NEAR - 1 - i), 1, SEL_CHUNK)[:, None]
                      for i in range(SEL_NEAR)] + [zero + NEG], axis=1)
        for sub in range(SEL_CHUNK // Q_BLOCK)], axis=1)
    tw = toeplitz(w_win, WINDOW, 1, WINDOW + Q_BLOCK)
    return _stack_heads(tc), _stack_heads(ts), _stack_heads(tw)


def _compress_kernel(x_ref, pe_ref, w1_ref, w2_ref, o_ref, *, transposed):
    n = x_ref.shape[0] // CMP_STRIDE
    a = b = None
    for tok in range(CMP_STRIDE):
        xt = x_ref[pl.ds(tok, n, stride=CMP_STRIDE), :]
        as_first = (xt + pe_ref[tok:tok + 1, :]).astype(BF16)
        as_second = (xt + pe_ref[CMP_STRIDE + tok:CMP_STRIDE + tok + 1, :]).astype(BF16)
        da = jnp.dot(as_first, w1_ref[tok * HEAD_DIM:(tok + 1) * HEAD_DIM], preferred_element_type=F32)
        db = jnp.dot(as_second, w1_ref[(CMP_STRIDE + tok) * HEAD_DIM:(CMP_STRIDE + tok + 1) * HEAD_DIM],
                     preferred_element_type=F32)
        a = da if a is None else a + da
        b = db if b is None else b + db
    pre = a + pltpu.roll(b, n - 1, 0)
    rowi = lax.broadcasted_iota(jnp.int32, pre.shape, 0)
    hid = jnp.where(rowi < n - 1, jax.nn.gelu(pre), 0.0).astype(BF16)
    if transposed:
        out = lax.dot_general(w2_ref[...], hid, (((1,), (1,)), ((), ())), preferred_element_type=F32)
    else:
        out = jnp.dot(hid, w2_ref[...], preferred_element_type=F32)
    o_ref[...] = out.astype(BF16)


def _compress(slabs, first, pe, w1, w2, *, B, S, transposed):
    n = S // CMP_STRIDE
    wide = CMP_STRIDE * HEAD_DIM
    out_shape = (N_KV, B, HEAD_DIM, n) if transposed else (N_KV, B, n, HEAD_DIM)
    w2_in = w2.T if transposed else w2
    return pl.pallas_call(
        functools.partial(_compress_kernel, transposed=transposed),
        grid=(N_KV, B),
        in_specs=[pl.BlockSpec((None, None, S, HEAD_DIM), lambda h, b: (first + h, b, 0, 0)),
                  pl.BlockSpec((CMP_LEN, HEAD_DIM), lambda h, b: (0, 0)),
                  pl.BlockSpec((2 * wide, HEAD_DIM), lambda h, b: (0, 0)),
                  pl.BlockSpec((HEAD_DIM, HEAD_DIM), lambda h, b: (0, 0))],
        out_specs=pl.BlockSpec((None, None) + out_shape[2:], lambda h, b: (h, b, 0, 0)),
        out_shape=jax.ShapeDtypeStruct(out_shape, BF16),
        compiler_params=_cparams(("parallel", "parallel")),
        name="nsa_compress",
    )(slabs, pe, w1, w2_in)


def _query_cols(q_ref):
    return jnp.concatenate([q_ref[g * HEAD_DIM:(g + 1) * HEAD_DIM, :] for g in range(GROUP)], axis=1)


def _cmp_body(ib, n_tiles, q_ref, kc_ref, vct_ref, tc_ref, ov_ref, oc_ref, mb_ref):
    n_cmp = n_tiles * CMP_TILE
    n_sel = n_cmp * CMP_STRIDE // SEL_LEN
    qt = _query_cols(q_ref)
    s = jnp.dot(kc_ref[:n_cmp], qt, preferred_element_type=F32)
    tiles = []
    for cb in range(n_tiles):
        idx = jnp.clip(ib - (CMP_TILE * CMP_STRIDE // Q_BLOCK) * cb + 1, 0, CMP_NEAR + 1)
        tiles.append(s[cb * CMP_TILE:(cb + 1) * CMP_TILE] + tc_ref[idx])
    s = jnp.concatenate(tiles, axis=0)
    m = jnp.max(s, axis=0, keepdims=True)
    p = jnp.exp2(s - m)
    inv = jnp.where(m > 0.5 * NEG, 1.0 / jnp.maximum(jnp.sum(p, axis=0, keepdims=True), 1e-30), 0.0)
    p = p * inv
    oc_ref[...] = jnp.dot(vct_ref[:, :n_cmp], p.astype(BF16), preferred_element_type=F32)
    ov = ov_ref[:n_sel, :n_cmp]

    psum = p[:, 0:Q_BLOCK]
    for g in range(1, GROUP):
        psum = psum + p[:, g * Q_BLOCK:(g + 1) * Q_BLOCK]
    p_hi = psum.astype(BF16)
    p_lo = (psum - p_hi.astype(F32)).astype(BF16)
    imp = (jnp.dot(ov, p_hi, preferred_element_type=F32)
           + jnp.dot(ov, p_lo, preferred_element_type=F32))
    blk = lax.broadcasted_iota(jnp.int32, (n_sel, Q_BLOCK), 0)
    t = ib * Q_BLOCK + lax.broadcasted_iota(jnp.int32, (n_sel, Q_BLOCK), 1)
    cur = jnp.right_shift(t, int(math.log2(SEL_LEN)))
    ok = blk * SEL_LEN <= t
    forced = ok & ((blk == 0) | (blk == cur) | (blk == cur - 1))
    score = jnp.where(forced, -jnp.inf, jnp.where(ok, imp, NEG))
    taken = forced
    for _ in range(SEL_TOP - 3):
        top = jnp.max(score, axis=0, keepdims=True)
        first = jnp.min(jnp.where(score == top, blk, n_sel), axis=0, keepdims=True)
        pick = blk == first
        taken = taken | pick
        score = jnp.where(pick, -jnp.inf, score)
    mb_ref[:n_sel] = jnp.where(taken & ok, 0.0, NEG).astype(BF16)
    if n_sel < mb_ref.shape[0]:
        mb_ref[n_sel:] = jnp.full((mb_ref.shape[0] - n_sel, Q_BLOCK), NEG, BF16)


def _cmp_kernel(*refs, n_tiles):
    ib = pl.program_id(2)
    per_tile = CMP_TILE * CMP_STRIDE // Q_BLOCK
    lo = 0
    for nt in range(1, n_tiles + 1):
        hi = nt * per_tile

        @pl.when((ib >= lo) & (ib < hi))
        def _():
            _cmp_body(ib, nt, *refs)
        lo = hi


def _cmp_attention(qt, kc, vct, tc, overlap_t, *, B, S):
    nq = S // Q_BLOCK
    n_cmp = S // CMP_STRIDE
    n_sel = S // SEL_LEN
    n_tiles = n_cmp // CMP_TILE
    return pl.pallas_call(
        functools.partial(_cmp_kernel, n_tiles=n_tiles),
        grid=(N_KV, B, nq),
        in_specs=[pl.BlockSpec((GROUP * HEAD_DIM, Q_BLOCK), lambda h, b, i: (h, b * nq + i)),
                  pl.BlockSpec((None, None, n_cmp, HEAD_DIM), lambda h, b, i: (h, b, 0, 0)),
                  pl.BlockSpec((None, None, HEAD_DIM, n_cmp), lambda h, b, i: (h, b, 0, 0)),
                  pl.BlockSpec((None, CMP_NEAR + 2, CMP_TILE, QG), lambda h, b, i: (h, 0, 0, 0)),
                  pl.BlockSpec((n_sel, n_cmp), lambda h, b, i: (0, 0))],
        out_specs=[pl.BlockSpec((None, None, HEAD_DIM, QG), lambda h, b, i: (h, b * nq + i, 0, 0)),
                   pl.BlockSpec((None, None, n_sel, Q_BLOCK), lambda h, b, i: (h, b, 0, i))],
        out_shape=[jax.ShapeDtypeStruct((N_KV, B * nq, HEAD_DIM, QG), F32),
                   jax.ShapeDtypeStruct((N_KV, B, n_sel, S), BF16)],
        compiler_params=_cparams(("parallel", "parallel", "parallel")),
        name="nsa_cmp_select",
    )(qt, kc, vct, tc, overlap_t)


def _sel_kernel(q_ref, mb_ref, k_ref, oh_ref, vt_ref, ts_ref, o_ref, m_ref, acc_ref, sa_ref, sb_ref):
    ib = pl.program_id(2)
    per = SEL_CHUNK // Q_BLOCK
    cq = ib // per
    sub = ib % per
    qt = _query_cols(q_ref)
    blocks_per_chunk = SEL_CHUNK // SEL_LEN
    chunks_per_group = LANE // blocks_per_chunk

    m_ref[...] = jnp.full(m_ref.shape, NEG, F32)
    acc_ref[...] = jnp.zeros_like(acc_ref)

    qq = cq // SEL_FAR
    rows = SEL_FAR * SEL_CHUNK

    def raw_scores(quad):
        c0 = pl.multiple_of(quad * SEL_FAR, SEL_FAR)
        grp = c0 // chunks_per_group
        mb = mb_ref[pl.ds(pl.multiple_of(grp * LANE, LANE), LANE), :]
        qa = jnp.concatenate([qt, jnp.concatenate([mb] * GROUP, axis=1)], axis=0)
        o0 = pl.multiple_of(c0 % oh_ref.shape[0], SEL_FAR)
        keys = jnp.concatenate([k_ref[pl.ds(c0, SEL_FAR)].reshape(rows, HEAD_DIM),
                                oh_ref[pl.ds(o0, SEL_FAR)].reshape(rows, HEAD_DIM)], axis=1)
        return jnp.dot(keys, qa, preferred_element_type=F32)

    def update(s, quad):
        c0 = quad * SEL_FAR
        v = jnp.concatenate([vt_ref[c0 + k] for k in range(SEL_FAR)], axis=1)
        m_old = m_ref[0:1, :]
        m_new = jnp.maximum(m_old, jnp.max(s, axis=0, keepdims=True))
        p = jnp.exp2((s - m_new).astype(BF16))
        acc_ref[...] = (jnp.exp2(m_old - m_new) * acc_ref[...]
                        + jnp.dot(v, p, preferred_element_type=F32))
        m_ref[...] = jnp.broadcast_to(m_new, m_ref.shape)

    def bias(quad, live):
        tiles = []
        for k in range(SEL_FAR):
            near = jnp.clip(quad * SEL_FAR + k - cq + SEL_NEAR, 0, SEL_NEAR + 1)
            tiles.append(ts_ref[sub, jnp.where(live, near, SEL_NEAR + 1)])
        return jnp.concatenate(tiles, axis=0)

    def finish(near0_ref):
        s1 = raw_scores(qq)
        update(near0_ref[...] + bias(qq - 1, qq >= 1), jnp.maximum(qq - 1, 0))
        update(s1 + bias(qq, True), qq)
        o_ref[...] = acc_ref[:HEAD_DIM] / jnp.maximum(acc_ref[HEAD_DIM:HEAD_DIM + 1], 1e-30)

    n_far = jnp.maximum(qq - 1, 0)
    sa_ref[...] = raw_scores(0)

    def far_pair(i, carry):
        sb_ref[...] = raw_scores(2 * i + 1)
        update(sa_ref[...], 2 * i)
        sa_ref[...] = raw_scores(2 * i + 2)
        update(sb_ref[...], 2 * i + 1)
        return carry

    lax.fori_loop(0, n_far // 2, far_pair, 0)

    @pl.when(n_far % 2 == 1)
    def _():
        sb_ref[...] = raw_scores(n_far)
        update(sa_ref[...], n_far - 1)
        finish(sb_ref)

    @pl.when(n_far % 2 == 0)
    def _():
        finish(sa_ref)


def _sel_attention(qt, mbt, k_sel, onehot, vt, ts, *, B, S):
    nq = S // Q_BLOCK
    n_sel = mbt.shape[2]
    nc = S // SEL_CHUNK
    return pl.pallas_call(
        _sel_kernel,
        grid=(N_KV, B, nq),
        in_specs=[pl.BlockSpec((GROUP * HEAD_DIM, Q_BLOCK), lambda h, b, i: (h, b * nq + i)),
                  pl.BlockSpec((None, None, n_sel, Q_BLOCK), lambda h, b, i: (h, b, 0, i)),
                  pl.BlockSpec((None, None, nc, SEL_CHUNK, HEAD_DIM), lambda h, b, i: (h, b, 0, 0, 0)),
                  pl.BlockSpec(onehot.shape, lambda h, b, i: (0, 0, 0)),
                  pl.BlockSpec((None, None, nc, V_ROWS, SEL_CHUNK), lambda h, b, i: (h, b, 0, 0, 0)),
                  pl.BlockSpec((None, SEL_CHUNK // Q_BLOCK, SEL_NEAR + 2, SEL_CHUNK, QG),
                               lambda h, b, i: (h, 0, 0, 0, 0))],
        out_specs=pl.BlockSpec((None, None, HEAD_DIM, QG), lambda h, b, i: (h, b * nq + i, 0, 0)),
        out_shape=jax.ShapeDtypeStruct((N_KV, B * nq, HEAD_DIM, QG), F32),
        scratch_shapes=[pltpu.VMEM((8, QG), F32), pltpu.VMEM((V_ROWS, QG), F32),
                        pltpu.VMEM((SEL_FAR * SEL_CHUNK, QG), F32),
                        pltpu.VMEM((SEL_FAR * SEL_CHUNK, QG), F32)],
        compiler_params=_cparams(("parallel", "parallel", "arbitrary")),
        name="nsa_selected",
    )(qt, mbt, k_sel, onehot, vt, ts)


def _win_kernel(*refs):
    nb = WINDOW // Q_BLOCK + 1
    q_ref = refs[0]
    k_refs = refs[1:1 + nb]
    v_refs = refs[1 + nb:1 + 2 * nb]
    tw_ref, gate_ref, oc_ref, os_ref, o_ref = refs[1 + 2 * nb:]
    ib = pl.program_id(2)
    qt = _query_cols(q_ref)
    parts = []
    for i in range(nb):
        s = jnp.dot(k_refs[i][...], qt, preferred_element_type=F32)
        s = s + tw_ref[i * Q_BLOCK:(i + 1) * Q_BLOCK, :]
        parts.append(jnp.where(ib - (nb - 1) + i >= 0, s, NEG))
    s = jnp.concatenate(parts, axis=0)
    m = jnp.max(s, axis=0, keepdims=True)
    pb = jnp.exp2((s - m).astype(BF16))
    ow = jnp.dot(v_refs[0][...], pb[0:Q_BLOCK], preferred_element_type=F32)
    for i in range(1, nb):
        ow = ow + jnp.dot(v_refs[i][...], pb[i * Q_BLOCK:(i + 1) * Q_BLOCK], preferred_element_type=F32)
    ow = ow[:HEAD_DIM] / jnp.maximum(ow[HEAD_DIM:HEAD_DIM + 1], 1e-30)

    def gate(j):
        return jnp.concatenate([gate_ref[j, g:g + 1, :] for g in range(GROUP)], axis=1)

    o = gate(0) * oc_ref[...] + gate(1) * os_ref[...] + gate(2) * ow
    for g in range(GROUP):
        o_ref[:, g * HEAD_DIM:(g + 1) * HEAD_DIM] = o[:, g * Q_BLOCK:(g + 1) * Q_BLOCK].T.astype(BF16)


def _win_attention(qt, kw, vwt, tw, gates_t, oc, osel, *, B, S):
    nq = S // Q_BLOCK
    nb = WINDOW // Q_BLOCK + 1

    def kmap(i_off):
        return lambda h, b, i: (N_KV + h, b * nq + jnp.maximum(i - (nb - 1) + i_off, 0), 0)

    def vmap_(i_off):
        return lambda h, b, i: (h, 0, b * nq + jnp.maximum(i - (nb - 1) + i_off, 0))

    blk = pl.BlockSpec((None, None, HEAD_DIM, QG), lambda h, b, i: (h, b * nq + i, 0, 0))
    return pl.pallas_call(
        _win_kernel,
        grid=(N_KV, B, nq),
        in_specs=([pl.BlockSpec((GROUP * HEAD_DIM, Q_BLOCK), lambda h, b, i: (h, b * nq + i))]
                  + [pl.BlockSpec((None, Q_BLOCK, HEAD_DIM), kmap(j)) for j in range(nb)]
                  + [pl.BlockSpec((None, V_ROWS, Q_BLOCK), vmap_(j)) for j in range(nb)]
                  + [pl.BlockSpec((None, WINDOW + Q_BLOCK, QG), lambda h, b, i: (h, 0, 0)),
                     pl.BlockSpec((3, None, GROUP, Q_BLOCK), lambda h, b, i: (0, h, 0, b * nq + i)),
                     blk, blk]),
        out_specs=pl.BlockSpec((Q_BLOCK, GROUP * HEAD_DIM), lambda h, b, i: (b * nq + i, h)),
        out_shape=jax.ShapeDtypeStruct((B * S, N_HEADS * HEAD_DIM), BF16),
        compiler_params=_cparams(("parallel", "parallel", "parallel")),
        name="nsa_window_merge",
    )(qt, *([kw] * nb), *([vwt] * nb), tw, gates_t, oc, osel)


def _nsa_layer(x32, xb, p, tables, consts, ln_g, ln_b, *, B, S):
    T = B * S
    tc, ts, tw = tables
    overlap_t, sel_onehot = consts
    kd = N_KV * HEAD_DIM
    qv = _proj_nt(p["w_qv_t"], xb, out_dtype=BF16, scale=HEAD_DIM ** -0.5 * LOG2E,
                  n_scaled=N_HEADS * HEAD_DIM // 512)
    qt = qv
    k2 = _proj_nn(xb, p["w_k"], out_dtype=BF16, head_major=True)
    cmp = _proj_nn(xb, p["w_cmp"], out_dtype=F32, head_major=True)
    gates_t = _proj_nt(p["w_gate_t"], xb, out_dtype=F32, act="sigmoid")
    gates_t = gates_t.reshape(3, N_KV, GROUP, T)

    cmp = cmp.reshape(2 * N_KV, B, S, HEAD_DIM)
    kc = _compress(cmp, 0, p["pe_k"], p["w1_k"], p["w2_k"], B=B, S=S, transposed=False)
    vct = _compress(cmp, N_KV, p["pe_v"], p["w1_v"], p["w2_v"], B=B, S=S, transposed=True)
    oc, mbt = _cmp_attention(qt, kc, vct, tc, overlap_t, B=B, S=S)
    if mbt.shape[2] % LANE:
        mbt = jnp.pad(mbt, ((0, 0), (0, 0), (0, -mbt.shape[2] % LANE), (0, 0)), constant_values=NEG)

    nc = S // SEL_CHUNK
    k_sel = k2.reshape(2 * N_KV, B, nc, SEL_CHUNK, HEAD_DIM)
    v_sel_t = qv[N_HEADS * HEAD_DIM:N_HEADS * HEAD_DIM + kd].reshape(N_KV, HEAD_DIM, B, nc, SEL_CHUNK)
    v_sel_t = v_sel_t.transpose(0, 2, 3, 1, 4)
    ones_rows = (jnp.arange(V_ROWS - HEAD_DIM) == 0).astype(BF16)[:, None]
    v_sel_t = jnp.concatenate(
        [v_sel_t, jnp.broadcast_to(ones_rows, v_sel_t.shape[:3] + (V_ROWS - HEAD_DIM, SEL_CHUNK))], axis=3)
    osel = _sel_attention(qt, mbt, k_sel, sel_onehot, v_sel_t, ts, B=B, S=S)

    v_win_t = qv[N_HEADS * HEAD_DIM + kd:].reshape(N_KV, HEAD_DIM, T)
    v_win_t = jnp.concatenate(
        [v_win_t, jnp.broadcast_to(ones_rows, (N_KV, V_ROWS - HEAD_DIM, T))], axis=1)
    attn = _win_attention(qt, k2, v_win_t, tw, gates_t, oc, osel, B=B, S=S)
    return _proj_ln(attn, p["w_out"], x32, ln_g, ln_b, *p["router"])


def _nsa_params(w_in, pe_k, w1_k, w2_k, pe_v, w1_v, w2_v, w_out):
    D = w_in.shape[0]
    qd, kd = N_HEADS * HEAD_DIM, N_KV * HEAD_DIM
    w_kv = w_in[:, qd:qd + 6 * kd].reshape(D, 6, kd)
    gcol = np.array([[[(kv * GROUP + g) * 3 + j for g in range(GROUP)] for kv in range(N_KV)]
                     for j in range(3)]).reshape(-1)
    return dict(
        w_qv_t=jnp.concatenate([w_in[:, :qd], w_kv[:, 3], w_kv[:, 5]], axis=1).T.astype(BF16),
        w_k=jnp.concatenate([w_kv[:, 2], w_kv[:, 4]], axis=1).astype(BF16),
        w_cmp=jnp.concatenate([w_kv[:, 0], w_kv[:, 1]], axis=1).astype(BF16),
        w_gate_t=w_in[:, qd + 6 * kd:][:, gcol].T.astype(BF16),
        pe_k=pe_k, w1_k=w1_k.astype(BF16), w2_k=w2_k.astype(BF16),
        pe_v=pe_v, w1_v=w1_v.astype(BF16), w2_v=w2_v.astype(BF16),
        w_out=w_out.astype(BF16))


def _nsa_constants(S):
    n_cmp, n_sel = S // CMP_STRIDE, S // SEL_LEN
    cj = np.arange(n_cmp) * CMP_STRIDE
    sb = np.arange(n_sel) * SEL_LEN
    ov = np.maximum(np.minimum(cj[None, :] + CMP_LEN, sb[:, None] + SEL_LEN)
                    - np.maximum(cj[None, :], sb[:, None]), 0).astype(np.float32) / CMP_LEN
    ov[:, n_cmp - 1] = 0.0
    key = np.arange(SEL_CHUNK)
    chunk = np.arange(min(S // SEL_CHUNK, LANE * SEL_LEN // SEL_CHUNK))
    blk = (chunk[:, None] * (SEL_CHUNK // SEL_LEN) + key[None, :] // SEL_LEN) % LANE
    onehot = (blk[:, :, None] == np.arange(LANE)[None, None, :]).astype(np.float32)
    return jnp.asarray(ov, BF16), jnp.asarray(onehot, BF16)


def kernel(x, rel_bias, router_w, router_b, ln1_g, ln1_b, ln2_g, ln2_b, lru_w_in, lru_conv_w,
           lru_conv_b, lru_w_a, lru_b_a, lru_w_i, lru_b_i, lru_lambda, lru_w_out, nsa_w_in,
           nsa_pe_k, nsa_w1_k, nsa_w2_k, nsa_pe_v, nsa_w1_v, nsa_w2_v, nsa_w_out,
           moe_w_gate, moe_w_up, moe_w_down):
    B, S, D = x.shape
    T = B * S
    assert D == D_MODEL and S % (CMP_TILE * CMP_STRIDE) == 0 and S // SEL_LEN >= SEL_TOP
    qd, kd = N_HEADS * HEAD_DIM, N_KV * HEAD_DIM
    tables = _bias_tables(rel_bias, S)
    consts = _nsa_constants(S)
    router = _router_params(router_w, router_b)

    x32 = x.reshape(T, D)
    xb = x32.astype(BF16)
    for layer in range(DEPTH):
        j = layer // 2
        if layer % 2 == 0:
            w_in = lru_w_in[j].astype(BF16)
            p = dict(w_gate=w_in[:, :D], w_y=w_in[:, D:], conv_w=lru_conv_w[j], conv_b=lru_conv_b[j],
                     w_a=lru_w_a[j].astype(BF16), b_a=lru_b_a[j], w_i=lru_w_i[j].astype(BF16),
                     b_i=lru_b_i[j], lam=lru_lambda[j], w_out=lru_w_out[j].astype(BF16), router=router)
            x32, xb, logits = _rglru_layer(x32, xb, p, ln1_g[layer], ln1_b[layer], B=B, S=S)
        else:
            p = _nsa_params(nsa_w_in[j], nsa_pe_k[j], nsa_w1_k[j], nsa_w2_k[j], nsa_pe_v[j],
                            nsa_w1_v[j], nsa_w2_v[j], nsa_w_out[j])
            p["router"] = router
            x32, xb, logits = _nsa_layer(x32, xb, p, tables, consts, ln1_g[layer], ln1_b[layer], B=B, S=S)
        x32, xb = _moe_layer(x32, logits, moe_w_gate[layer].astype(BF16),
                             moe_w_up[layer].astype(BF16), moe_w_down[layer].astype(BF16),
                             ln2_g[layer], ln2_b[layer])
    return x32.reshape(B, S, D)
```

```python
import functools
import math

import numpy as np
import jax
import jax.numpy as jnp
from jax import lax
from jax.experimental import pallas as pl
from jax.experimental.pallas import tpu as pltpu

F32 = jnp.float32
BF16 = jnp.bfloat16

D_MODEL = 2048
DEPTH = 4
LRU_BLOCK = 256
LRU_NBLK = D_MODEL // LRU_BLOCK
CONV_W = 4
LRU_C = 8.0
HEAD_DIM = 128
N_HEADS = 16
N_KV = 4
GROUP = 4
CMP_LEN = 32
CMP_STRIDE = 16
SEL_LEN = 64
SEL_TOP = 16
WINDOW = 512
Q_BLOCK = 128
NUM_BUCKETS = 32
MAX_DISTANCE = 1024
N_EXPERTS = 16
N_GROUPS = 4
EXPERTS_PER_GROUP = 4
D_EXPERT = 1024
ALPHA = (2 * DEPTH) ** 0.25
LN_EPS = 1e-5
NEG = -1e30
BIG = 1e9

LANE = 128
QG = GROUP * Q_BLOCK
EXPERT_ROWS = 256
SEL_CHUNK = 256
SEL_NEAR = 5
SEL_FAR = 4
V_ROWS = HEAD_DIM + 16
LOG2E = math.log2(math.e)
CMP_TILE = 128
CMP_NEAR = 23
VMEM_LIMIT = 56 << 20


def _cparams(sem, vmem=VMEM_LIMIT):
    return pltpu.CompilerParams(dimension_semantics=sem, vmem_limit_bytes=vmem)


def _act(x, act):
    if act == "gelu":
        return jax.nn.gelu(x)
    if act == "sigmoid":
        return jax.nn.sigmoid(x)
    return x


def _proj_nn_kernel(x_ref, w_ref, o_ref, *, act, head_major):
    acc = jnp.dot(x_ref[...], w_ref[...], preferred_element_type=F32)
    acc = _act(acc, act)
    if head_major:
        for h in range(o_ref.shape[0]):
            o_ref[h] = acc[:, h * LANE:(h + 1) * LANE].astype(o_ref.dtype)
    else:
        o_ref[...] = acc.astype(o_ref.dtype)


def _proj_nn(x, w, *, out_dtype, act=None, head_major=False, tm=1024, tn=1024):
    T, K = x.shape
    N = w.shape[1]
    tm, tn = min(tm, T), min(tn, N)
    if head_major:
        out_shape = (N // LANE, T, LANE)
        out_spec = pl.BlockSpec((tn // LANE, tm, LANE), lambda j, i: (j, i, 0))
    else:
        out_shape = (T, N)
        out_spec = pl.BlockSpec((tm, tn), lambda j, i: (i, j))
    return pl.pallas_call(
        functools.partial(_proj_nn_kernel, act=act, head_major=head_major),
        grid=(N // tn, T // tm),
        in_specs=[pl.BlockSpec((tm, K), lambda j, i: (i, 0)),
                  pl.BlockSpec((K, tn), lambda j, i: (0, j))],
        out_specs=out_spec,
        out_shape=jax.ShapeDtypeStruct(out_shape, out_dtype),
        compiler_params=_cparams(("parallel", "parallel")),
        name="proj_nn",
    )(x, w)


def _proj_nt_kernel(wt_ref, x_ref, o_ref, *, act, scale, n_scaled):
    acc = lax.dot_general(wt_ref[...], x_ref[...], (((1,), (1,)), ((), ())),
                          preferred_element_type=F32)
    if n_scaled:
        acc = acc * jnp.where(pl.program_id(0) < n_scaled, scale, 1.0).astype(F32)
    o_ref[...] = _act(acc, act).astype(o_ref.dtype)


def _proj_nt(wt, x, *, out_dtype, act=None, scale=1.0, n_scaled=0, tm=1024, tn=512):
    N, K = wt.shape
    T = x.shape[0]
    tm, tn = min(tm, T), min(tn, N)
    return pl.pallas_call(
        functools.partial(_proj_nt_kernel, act=act, scale=scale, n_scaled=n_scaled),
        grid=(N // tn, T // tm),
        in_specs=[pl.BlockSpec((tn, K), lambda j, i: (j, 0)),
                  pl.BlockSpec((tm, K), lambda j, i: (i, 0))],
        out_specs=pl.BlockSpec((tn, tm), lambda j, i: (j, i)),
        out_shape=jax.ShapeDtypeStruct((N, T), out_dtype),
        compiler_params=_cparams(("parallel", "parallel")),
        name="proj_nt",
    )(wt, x)


def _layer_norm_rows(z, g, b):
    mu = jnp.mean(z, axis=-1, keepdims=True)
    zc = z - mu
    var = jnp.mean(zc * zc, axis=-1, keepdims=True)
    return zc * lax.rsqrt(var + LN_EPS) * g + b


def _proj_ln_kernel(a_ref, w_ref, res_ref, g_ref, b_ref, rw_ref, rb_ref, of_ref, ob_ref, lg_ref):
    acc = jnp.dot(a_ref[...], w_ref[...], preferred_element_type=F32)
    y = _layer_norm_rows(ALPHA * res_ref[...] + acc, g_ref[...], b_ref[...])
    of_ref[...] = y
    y_hi = y.astype(BF16)
    ob_ref[...] = y_hi
    y_lo = (y - y_hi.astype(F32)).astype(BF16)
    rows = y.shape[0]
    both = jnp.dot(jnp.concatenate([y_hi, y_lo], axis=0), rw_ref[0], preferred_element_type=F32)
    lg_ref[...] = (both[:rows] + both[rows:] + jnp.dot(y_hi, rw_ref[1], preferred_element_type=F32)
                   + rb_ref[...])


def _proj_ln(a, w, res, g, b, router_w, router_b, *, tm=256):
    T, K = a.shape
    D = w.shape[1]
    tm = min(tm, T)
    row = pl.BlockSpec((tm, D), lambda i: (i, 0))
    vec = pl.BlockSpec((1, D), lambda i: (0, 0))
    lgt = pl.BlockSpec((tm, LANE), lambda i: (i, 0))
    return pl.pallas_call(
        _proj_ln_kernel,
        grid=(T // tm,),
        in_specs=[pl.BlockSpec((tm, K), lambda i: (i, 0)),
                  pl.BlockSpec((K, D), lambda i: (0, 0)),
                  row, vec, vec,
                  pl.BlockSpec((2, D, LANE), lambda i: (0, 0, 0)),
                  pl.BlockSpec((1, LANE), lambda i: (0, 0))],
        out_specs=[row, row, lgt],
        out_shape=[jax.ShapeDtypeStruct((T, D), F32), jax.ShapeDtypeStruct((T, D), BF16),
                   jax.ShapeDtypeStruct((T, LANE), F32)],
        compiler_params=_cparams(("parallel",)),
        name="proj_ln",
    )(a, w, res, g.reshape(1, D), b.reshape(1, D), router_w, router_b)


def _lru_kernel(y_ref, gate_ref, cw_ref, cb_ref, wa_ref, ba_ref, wi_ref, bi_ref, lam_ref,
                o_ref, h_ref, tail_ref, *, rows):
    @pl.when(pl.program_id(2) == 0)
    def _():
        h_ref[...] = jnp.zeros_like(h_ref)
        tail_ref[...] = jnp.zeros_like(tail_ref)

    y = y_ref[...]
    tail = tail_ref[...]
    cw = cw_ref[...]
    row8 = lax.broadcasted_iota(jnp.int32, (8, LRU_BLOCK), 0)
    yc = cb_ref[...] + y * cw[CONV_W - 1:CONV_W, :]
    for d in range(1, CONV_W):
        ys = pltpu.roll(y, d, 0)
        first = jnp.where(row8 < d, pltpu.roll(tail, d, 0), ys[:8])
        ys = jnp.concatenate([first, ys[8:]], axis=0)
        yc = yc + ys * cw[CONV_W - 1 - d:CONV_W - d, :]
    tail_ref[...] = y[rows - 8:]

    yb = yc.astype(BF16)
    r = jax.nn.sigmoid(jnp.dot(yb, wa_ref[...], preferred_element_type=F32) + ba_ref[...])
    ig = jax.nn.sigmoid(jnp.dot(yb, wi_ref[...], preferred_element_type=F32) + bi_ref[...])
    z = -lam_ref[...]
    softplus = jnp.maximum(z, 0.0) + jnp.log1p(jnp.exp(-jnp.abs(z)))
    log_a = (-LRU_C * softplus) * r
    a = jnp.exp(log_a)
    h = jnp.sqrt(1.0 - a * a) * (ig * yc)

    rowi = lax.broadcasted_iota(jnp.int32, (rows, LRU_BLOCK), 0)
    s = 1
    while s < rows:
        if s < 8:
            keep = rowi >= s
            h = jnp.where(keep, h + a * pltpu.roll(h, s, 0), h)
            a = jnp.where(keep, a * pltpu.roll(a, s, 0), a)
        else:
            h = jnp.concatenate([h[:s], h[s:] + a[s:] * h[:rows - s]], axis=0)
            a = jnp.concatenate([a[:s], a[s:] * a[:rows - s]], axis=0)
        s *= 2
    h = h + a * h_ref[0:1, :]
    h_ref[...] = jnp.broadcast_to(h[rows - 1:rows, :], h_ref.shape)
    o_ref[...] = (h * gate_ref[...].astype(F32)).astype(BF16)


def _lru_core(y, gate, conv_w, conv_b, w_a, b_a, w_i, b_i, lam, *, B, S, rows=512):
    T, W = y.shape
    rows = min(rows, S)
    ns = S // rows
    blk = pl.BlockSpec((rows, LRU_BLOCK), lambda n, b, s: (b * ns + s, n))
    vec = pl.BlockSpec((1, LRU_BLOCK), lambda n, b, s: (0, n))
    wblk = pl.BlockSpec((None, LRU_BLOCK, LRU_BLOCK), lambda n, b, s: (n, 0, 0))
    return pl.pallas_call(
        functools.partial(_lru_kernel, rows=rows),
        grid=(LRU_NBLK, B, ns),
        in_specs=[blk, blk,
                  pl.BlockSpec((CONV_W, LRU_BLOCK), lambda n, b, s: (0, n)), vec,
                  wblk, vec, wblk, vec, vec],
        out_specs=blk,
        out_shape=jax.ShapeDtypeStruct((T, W), BF16),
        scratch_shapes=[pltpu.VMEM((8, LRU_BLOCK), F32), pltpu.VMEM((8, LRU_BLOCK), F32)],
        compiler_params=_cparams(("parallel", "parallel", "arbitrary")),
        name="lru_core",
    )(y, gate, conv_w, conv_b.reshape(1, W), w_a, b_a.reshape(1, W), w_i, b_i.reshape(1, W),
      lam.reshape(1, W))


def _rglru_layer(x32, xb, p, ln_g, ln_b, *, B, S):
    gate = _proj_nn(xb, p["w_gate"], out_dtype=BF16, act="gelu")
    y = _proj_nn(xb, p["w_y"], out_dtype=F32)
    hg = _lru_core(y, gate, p["conv_w"], p["conv_b"], p["w_a"], p["b_a"], p["w_i"], p["b_i"],
                   p["lam"], B=B, S=S)
    return _proj_ln(hg, p["w_out"], x32, ln_g, ln_b, *p["router"])


def _router_kernel(lg_ref, ids_ref, wts_ref, pos_ref, cnt_ref, carry_ref, *, tm):
    @pl.when(pl.program_id(0) == 0)
    def _():
        carry_ref[...] = jnp.zeros_like(carry_ref)

    logits = lg_ref[...].T[:N_EXPERTS]
    m = jnp.max(logits, axis=0, keepdims=True)
    e = jnp.exp(logits - m)
    p = e / jnp.sum(e, axis=0, keepdims=True)
    rows = [p[k:k + 1, :] for k in range(N_EXPERTS)]

    scores = []
    for g in range(N_GROUPS):
        a, b, c, d = rows[4 * g:4 * g + 4]
        hi1, lo1, hi2, lo2 = jnp.maximum(a, b), jnp.minimum(a, b), jnp.maximum(c, d), jnp.minimum(c, d)
        scores.append(jnp.maximum(hi1, hi2) + jnp.maximum(jnp.minimum(hi1, hi2), jnp.maximum(lo1, lo2)))
    best, gsel = scores[0], jnp.zeros((1, tm), jnp.int32)
    for g in range(1, N_GROUPS):
        upd = scores[g] > best
        gsel = jnp.where(upd, g, gsel)
        best = jnp.where(upd, scores[g], best)

    cand = []
    for k in range(EXPERTS_PER_GROUP):
        c = rows[k]
        for g in range(1, N_GROUPS):
            c = jnp.where(gsel == g, rows[4 * g + k], c)
        cand.append(c)
    v1, i1 = cand[0], jnp.zeros((1, tm), jnp.int32)
    for k in range(1, EXPERTS_PER_GROUP):
        upd = cand[k] > v1
        i1 = jnp.where(upd, k, i1)
        v1 = jnp.where(upd, cand[k], v1)
    v2, i2 = jnp.full((1, tm), -1.0, F32), jnp.zeros((1, tm), jnp.int32)
    for k in range(EXPERTS_PER_GROUP):
        c = jnp.where(i1 == k, -1.0, cand[k])
        upd = c > v2
        i2 = jnp.where(upd, k, i2)
        v2 = jnp.where(upd, c, v2)
    e0 = gsel * EXPERTS_PER_GROUP + i1
    e1 = gsel * EXPERTS_PER_GROUP + i2
    tot = v1 + v2

    eio = lax.broadcasted_iota(jnp.int32, (N_EXPERTS, tm), 0)
    hit0, hit1 = eio == e0, eio == e1
    onehot = jnp.where(hit0 | hit1, 1.0, 0.0)
    before = (lax.broadcasted_iota(jnp.int32, (tm, tm), 0)
              < lax.broadcasted_iota(jnp.int32, (tm, tm), 1))
    rank = jnp.dot(onehot.astype(BF16), jnp.where(before, 1.0, 0.0).astype(BF16),
                   preferred_element_type=F32) + carry_ref[:, 0:1]
    pos0 = jnp.sum(jnp.where(hit0, rank, 0.0), axis=0, keepdims=True)
    pos1 = jnp.sum(jnp.where(hit1, rank, 0.0), axis=0, keepdims=True)
    carry_ref[...] = carry_ref[...] + jnp.sum(onehot, axis=1, keepdims=True)
    cnt_ref[...] = carry_ref[...]

    zi = jnp.zeros((6, tm), jnp.int32)
    ids_ref[...] = jnp.concatenate([e0, e1, zi], axis=0)
    pos_ref[...] = jnp.concatenate([pos0.astype(jnp.int32), pos1.astype(jnp.int32), zi], axis=0)
    wts_ref[...] = jnp.concatenate([v1 / tot, v2 / tot, jnp.zeros((6, tm), F32)], axis=0)


def _router(logits, *, tm=512):
    T = logits.shape[0]
    tm = min(tm, T)
    tok = pl.BlockSpec((8, tm), lambda i: (0, i))
    return pl.pallas_call(
        functools.partial(_router_kernel, tm=tm),
        grid=(T // tm,),
        in_specs=[pl.BlockSpec((tm, LANE), lambda i: (i, 0))],
        out_specs=[tok, tok, tok, pl.BlockSpec((N_EXPERTS, LANE), lambda i: (0, 0))],
        out_shape=[jax.ShapeDtypeStruct((8, T), jnp.int32), jax.ShapeDtypeStruct((8, T), F32),
                   jax.ShapeDtypeStruct((8, T), jnp.int32),
                   jax.ShapeDtypeStruct((N_EXPERTS, LANE), F32)],
        scratch_shapes=[pltpu.VMEM((N_EXPERTS, LANE), F32)],
        compiler_params=_cparams(("arbitrary",)),
        name="moe_router",
    )(logits)


def _router_params(router_w, router_b):
    pad = LANE - N_EXPERTS
    w = jnp.pad(router_w, ((0, 0), (0, pad)))
    w_hi = w.astype(BF16)
    w_lo = (w - w_hi.astype(F32)).astype(BF16)
    return (jnp.stack([w_hi, w_lo]), jnp.pad(router_b, (0, pad)).reshape(1, LANE))


def _row_copy(src, src_row, dst, dst_row, sem, n=1):
    return pltpu.make_async_copy(src.at[pl.ds(src_row, n)], dst.at[pl.ds(dst_row, n)], sem)


def _row_tokens_kernel(n_ref, lo_ref, hi_ref, d0_ref, d1_ref, tok_ref):
    step = 8

    def clear(r, c):
        tok_ref[r] = 0
        return c

    def put(i, c):
        for k in range(step):
            t = i * step + k
            tok_ref[d0_ref[t]] = t
            tok_ref[d1_ref[t]] = t
        return c

    for e in range(lo_ref.shape[0]):
        lax.fori_loop(lo_ref[e], hi_ref[e], clear, 0)
    lax.fori_loop(0, n_ref[0] // step, put, 0)


def _row_tokens(d0, d1, free_lo, free_hi, n_rows):
    n_tok = d0.shape[0]
    assert n_tok % 8 == 0
    return pl.pallas_call(
        _row_tokens_kernel,
        grid_spec=pltpu.PrefetchScalarGridSpec(
            num_scalar_prefetch=5, grid=(1,), in_specs=[],
            out_specs=pl.BlockSpec(memory_space=pltpu.SMEM)),
        out_shape=jax.ShapeDtypeStruct((n_rows,), jnp.int32),
        compiler_params=pltpu.CompilerParams(dimension_semantics=("arbitrary",)),
        name="moe_row_tokens",
    )(jnp.array([n_tok], jnp.int32), free_lo, free_hi, d0, d1)


def _ffn_kernel(be_ref, nact_ref, tok_ref, x_hbm, wg_ref, wu_ref, wd_ref, y_ref, xbuf, sems):
    i = pl.program_id(0)
    n_act = nact_ref[0]
    R = EXPERT_ROWS

    def gather(block, slot):
        def issue(r, c):
            _row_copy(x_hbm, tok_ref[block * R + r], xbuf.at[slot], r, sems.at[slot]).start()
            return c
        lax.fori_loop(0, R, issue, 0, unroll=8)

    def run(slot):
        @pl.when(i + 1 < n_act)
        def _():
            gather(i + 1, 1 - slot)

        def drain(r, c):
            _row_copy(x_hbm, 0, xbuf.at[slot], 0, sems.at[slot]).wait()
            return c
        lax.fori_loop(0, R, drain, 0, unroll=8)
        xb = xbuf[slot].astype(BF16)
        g = jnp.dot(xb, wg_ref[...], preferred_element_type=F32)
        u = jnp.dot(xb, wu_ref[...], preferred_element_type=F32)
        h = (jax.nn.silu(g) * u).astype(BF16)
        y_ref[...] = jnp.dot(h, wd_ref[...], preferred_element_type=F32)

    @pl.when(i == 0)
    def _():
        gather(0, 0)

    for slot in range(2):
        @pl.when((i < n_act) & (i % 2 == slot))
        def _():
            run(slot)

    @pl.when(i >= n_act)
    def _():
        y_ref[...] = jnp.zeros_like(y_ref)


def _expert_ffn(x32, row_tok, blk_e, n_act, w_gate, w_up, w_down, layer):
    D = x32.shape[1]
    R = EXPERT_ROWS
    n_rows = row_tok.shape[0]
    wmap = lambda i, be, na, tok: (layer, be[i], 0, 0)
    return pl.pallas_call(
        _ffn_kernel,
        grid_spec=pltpu.PrefetchScalarGridSpec(
            num_scalar_prefetch=3, grid=(n_rows // R,),
            in_specs=[pl.BlockSpec(memory_space=pl.ANY),
                      pl.BlockSpec((None, None, D, D_EXPERT), wmap),
                      pl.BlockSpec((None, None, D, D_EXPERT), wmap),
                      pl.BlockSpec((None, None, D_EXPERT, D), wmap)],
            out_specs=pl.BlockSpec((R, D), lambda i, be, na, tok: (i, 0)),
            scratch_shapes=[pltpu.VMEM((2, R, D), F32), pltpu.SemaphoreType.DMA((2,))]),
        out_shape=jax.ShapeDtypeStruct((n_rows, D), F32),
        compiler_params=_cparams(("arbitrary",)),
        name="moe_ffn",
    )(blk_e, n_act, row_tok, x32, w_gate, w_up, w_down)


def _combine_kernel(d0_ref, d1_ref, res_ref, w_ref, g_ref, b_ref, y_hbm, of_ref, ob_ref,
                    buf, sem, *, tm):
    i = pl.program_id(0)

    def issue(r, c):
        t = i * tm + r
        _row_copy(y_hbm, d0_ref[t], buf.at[0], r, sem).start()
        _row_copy(y_hbm, d1_ref[t], buf.at[1], r, sem).start()
        return c

    def drain(r, c):
        _row_copy(y_hbm, 0, buf.at[0], 0, sem).wait()
        _row_copy(y_hbm, 0, buf.at[1], 0, sem).wait()
        return c

    lax.fori_loop(0, tm, issue, 0)
    lax.fori_loop(0, tm, drain, 0)
    w = w_ref[...]
    z = ALPHA * res_ref[...] + (w[:, 0:1] * buf[0] + w[:, 1:2] * buf[1])
    y = _layer_norm_rows(z, g_ref[...], b_ref[...])
    of_ref[...] = y
    ob_ref[...] = y.astype(BF16)


def _combine_ln(y, d0, d1, wts, res, g, b, *, tm=256):
    T, D = res.shape
    tm = min(tm, T)
    row = pl.BlockSpec((tm, D), lambda i, a, c: (i, 0))
    vec = pl.BlockSpec((1, D), lambda i, a, c: (0, 0))
    return pl.pallas_call(
        functools.partial(_combine_kernel, tm=tm),
        grid_spec=pltpu.PrefetchScalarGridSpec(
            num_scalar_prefetch=2, grid=(T // tm,),
            in_specs=[row, pl.BlockSpec((tm, 2), lambda i, a, c: (i, 0)), vec, vec,
                      pl.BlockSpec(memory_space=pl.ANY)],
            out_specs=[row, row],
            scratch_shapes=[pltpu.VMEM((2, tm, D), F32), pltpu.SemaphoreType.DMA(())]),
        out_shape=[jax.ShapeDtypeStruct((T, D), F32), jax.ShapeDtypeStruct((T, D), BF16)],
        compiler_params=_cparams(("arbitrary",)),
        name="moe_combine_ln",
    )(d0, d1, res, wts, g.reshape(1, D), b.reshape(1, D), y)


def _moe_layer(x32, logits, w_gate, w_up, w_down, layer, ln_g, ln_b):
    T = x32.shape[0]
    R = EXPERT_ROWS
    ids, wts, pos, cnt = _router(logits)
    counts = cnt[:, 0].astype(jnp.int32)
    padded = (counts + R - 1) // R * R
    p_end = jnp.cumsum(padded)
    offs = p_end - padded
    d0 = offs[ids[0]] + pos[0]
    d1 = offs[ids[1]] + pos[1]
    n_blocks = 2 * T // R + N_EXPERTS
    n_act = p_end[-1] // R
    step = jnp.arange(n_blocks, dtype=jnp.int32)
    live = step < n_act
    src = jnp.where(live, step, n_act - 1)
    blk_e = jnp.minimum(jnp.sum(src[:, None] >= (p_end // R)[None, :], axis=1),
                        N_EXPERTS - 1).astype(jnp.int32)
    n_rows = n_blocks * R
    free_lo = jnp.concatenate([offs + counts, p_end[-1:]]).astype(jnp.int32)
    free_hi = jnp.concatenate([p_end, jnp.array([n_rows], p_end.dtype)]).astype(jnp.int32)
    row_tok = _row_tokens(d0, d1, free_lo, free_hi, n_rows)
    y = _expert_ffn(x32, row_tok, blk_e, n_act.reshape(1).astype(jnp.int32), w_gate, w_up, w_down, layer)
    return _combine_ln(y, d0, d1, wts[:2].T, x32, ln_g, ln_b)


def _t5_bucket(dist):
    n = jnp.maximum(dist, 0)
    max_exact = NUM_BUCKETS // 2
    nf = jnp.maximum(n, 1).astype(F32)
    large = max_exact + (jnp.log(nf / max_exact) / math.log(MAX_DISTANCE / max_exact)
                         * (NUM_BUCKETS - max_exact)).astype(jnp.int32)
    large = jnp.minimum(large, NUM_BUCKETS - 1)
    return jnp.where(n < max_exact, n, large)


def _stack_heads(tbl):
    parts = tbl.reshape((N_KV, GROUP) + tbl.shape[1:])
    return jnp.concatenate([parts[:, g] for g in range(GROUP)], axis=-1)


def _bias_tables(rel_bias, S):
    off = CMP_STRIDE * CMP_TILE + Q_BLOCK
    n_dist = CMP_NEAR * Q_BLOCK
    bvec = rel_bias[_t5_bucket(jnp.arange(n_dist))].T
    last = rel_bias[NUM_BUCKETS - 1]
    masked = jnp.full((N_HEADS, off), NEG, F32)
    w_abs = jnp.concatenate([masked, bvec * LOG2E], axis=1)
    w_rel = jnp.concatenate([masked, (bvec - last[:, None]) * LOG2E], axis=1)
    w_win = jnp.where(jnp.arange(-off, n_dist) < WINDOW, w_abs, NEG)

    def toeplitz(w, first, step, n):
        lo = first - step * (n - 1) + off
        span = step * (n - 1) + Q_BLOCK
        u = w[:, lo:lo + span]
        width = span + step
        tiled = jnp.tile(u, (1, -(-n * width // span)))[:, :n * width]
        return tiled.reshape(N_HEADS, n, width)[:, ::-1, :Q_BLOCK]

    def hankel(u, n, m, step):
        span = step * (n - 1) + m
        width = span + step
        u = u[..., :span, :]
        tiled = jnp.tile(u, (1,) * (u.ndim - 2) + (-(-n * width // span), 1))[..., :n * width, :]
        return tiled.reshape(u.shape[:-2] + (n, width, u.shape[-1]))[..., :m, :]

    per = Q_BLOCK // CMP_STRIDE
    n_seq = per * (CMP_NEAR - 1) + (per - 1) + CMP_TILE
    lo = off - (CMP_LEN - 1) - CMP_STRIDE * (CMP_TILE - 1)
    seq = w_abs[:, lo:lo + CMP_STRIDE * n_seq].reshape(N_HEADS, n_seq, CMP_STRIDE)
    by_tile = hankel(seq, CMP_NEAR, per - 1 + CMP_TILE, per)
    near_c = hankel(by_tile, per, CMP_TILE, 1)
    near_c = near_c[:, :, :, ::-1, :].transpose(0, 1, 3, 2, 4).reshape(N_HEADS, CMP_NEAR, CMP_TILE, Q_BLOCK)
    const = lambda v: jnp.broadcast_to(v, (N_HEADS, 1, CMP_TILE, Q_BLOCK)).astype(F32)
    tc = jnp.concatenate([const(NEG), near_c, const(last[:, None, None, None] * LOG2E)], axis=1)
    zero = jnp.zeros((N_HEADS, 1, SEL_CHUNK, Q_BLOCK), F32)
    ts = jnp.stack([
        jnp.concatenate(
            [zero] + [toeplitz(w_rel, Q_BLOCK * sub + SEL_CHUNK * (SEL_NEAR - 1 - i), 1, SEL_CHUNK)[:, None]
                      for i in range(SEL_NEAR)] + [zero + NEG], axis=1)
        for sub in range(SEL_CHUNK // Q_BLOCK)], axis=1)
    tw = toeplitz(w_win, WINDOW, 1, WINDOW + Q_BLOCK)
    return _stack_heads(tc), _stack_heads(ts), _stack_heads(tw)


def _compress_kernel(x_ref, pe_ref, w1_ref, w2_ref, o_ref, *, transposed):
    n = x_ref.shape[0] // CMP_STRIDE
    a = b = None
    for tok in range(CMP_STRIDE):
        xt = x_ref[pl.ds(tok, n, stride=CMP_STRIDE), :]
        as_first = (xt + pe_ref[tok:tok + 1, :]).astype(BF16)
        as_second = (xt + pe_ref[CMP_STRIDE + tok:CMP_STRIDE + tok + 1, :]).astype(BF16)
        da = jnp.dot(as_first, w1_ref[tok * HEAD_DIM:(tok + 1) * HEAD_DIM], preferred_element_type=F32)
        db = jnp.dot(as_second, w1_ref[(CMP_STRIDE + tok) * HEAD_DIM:(CMP_STRIDE + tok + 1) * HEAD_DIM],
                     preferred_element_type=F32)
        a = da if a is None else a + da
        b = db if b is None else b + db
    pre = a + pltpu.roll(b, n - 1, 0)
    rowi = lax.broadcasted_iota(jnp.int32, pre.shape, 0)
    hid = jnp.where(rowi < n - 1, jax.nn.gelu(pre), 0.0).astype(BF16)
    if transposed:
        out = lax.dot_general(w2_ref[...], hid, (((1,), (1,)), ((), ())), preferred_element_type=F32)
    else:
        out = jnp.dot(hid, w2_ref[...], preferred_element_type=F32)
    o_ref[...] = out.astype(BF16)


def _compress(slabs, first, pe, w1, w2, *, B, S, transposed):
    n = S // CMP_STRIDE
    wide = CMP_STRIDE * HEAD_DIM
    out_shape = (N_KV, B, HEAD_DIM, n) if transposed else (N_KV, B, n, HEAD_DIM)
    w2_in = w2.T if transposed else w2
    return pl.pallas_call(
        functools.partial(_compress_kernel, transposed=transposed),
        grid=(N_KV, B),
        in_specs=[pl.BlockSpec((None, None, S, HEAD_DIM), lambda h, b: (first + h, b, 0, 0)),
                  pl.BlockSpec((CMP_LEN, HEAD_DIM), lambda h, b: (0, 0)),
                  pl.BlockSpec((2 * wide, HEAD_DIM), lambda h, b: (0, 0)),
                  pl.BlockSpec((HEAD_DIM, HEAD_DIM), lambda h, b: (0, 0))],
        out_specs=pl.BlockSpec((None, None) + out_shape[2:], lambda h, b: (h, b, 0, 0)),
        out_shape=jax.ShapeDtypeStruct(out_shape, BF16),
        compiler_params=_cparams(("parallel", "parallel")),
        name="nsa_compress",
    )(slabs, pe, w1, w2_in)


def _query_cols(q_ref):
    return jnp.concatenate([q_ref[g * HEAD_DIM:(g + 1) * HEAD_DIM, :] for g in range(GROUP)], axis=1)


def _cmp_body(ib, n_tiles, q_ref, kc_ref, vct_ref, tc_ref, ov_ref, oc_ref, mb_ref):
    n_cmp = n_tiles * CMP_TILE
    n_sel = n_cmp * CMP_STRIDE // SEL_LEN
    qt = _query_cols(q_ref)
    s = jnp.dot(kc_ref[:n_cmp], qt, preferred_element_type=F32)
    tiles = []
    for cb in range(n_tiles):
        idx = jnp.clip(ib - (CMP_TILE * CMP_STRIDE // Q_BLOCK) * cb + 1, 0, CMP_NEAR + 1)
        tiles.append(s[cb * CMP_TILE:(cb + 1) * CMP_TILE] + tc_ref[idx])
    s = jnp.concatenate(tiles, axis=0)
    m = jnp.max(s, axis=0, keepdims=True)
    p = jnp.exp2(s - m)
    inv = jnp.where(m > 0.5 * NEG, 1.0 / jnp.maximum(jnp.sum(p, axis=0, keepdims=True), 1e-30), 0.0)
    p = p * inv
    oc_ref[...] = jnp.dot(vct_ref[:, :n_cmp], p.astype(BF16), preferred_element_type=F32)
    ov = ov_ref[:n_sel, :n_cmp]

    psum = p[:, 0:Q_BLOCK]
    for g in range(1, GROUP):
        psum = psum + p[:, g * Q_BLOCK:(g + 1) * Q_BLOCK]
    p_hi = psum.astype(BF16)
    p_lo = (psum - p_hi.astype(F32)).astype(BF16)
    imp = (jnp.dot(ov, p_hi, preferred_element_type=F32)
           + jnp.dot(ov, p_lo, preferred_element_type=F32))
    blk = lax.broadcasted_iota(jnp.int32, (n_sel, Q_BLOCK), 0)
    t = ib * Q_BLOCK + lax.broadcasted_iota(jnp.int32, (n_sel, Q_BLOCK), 1)
    cur = jnp.right_shift(t, int(math.log2(SEL_LEN)))
    ok = blk * SEL_LEN <= t
    forced = ok & ((blk == 0) | (blk == cur) | (blk == cur - 1))
    score = jnp.where(forced, -jnp.inf, jnp.where(ok, imp, NEG))
    taken = forced
    for _ in range(SEL_TOP - 3):
        top = jnp.max(score, axis=0, keepdims=True)
        first = jnp.min(jnp.where(score == top, blk, n_sel), axis=0, keepdims=True)
        pick = blk == first
        taken = taken | pick
        score = jnp.where(pick, -jnp.inf, score)
    mb_ref[:n_sel] = jnp.where(taken & ok, 0.0, NEG).astype(BF16)
    if n_sel < mb_ref.shape[0]:
        mb_ref[n_sel:] = jnp.full((mb_ref.shape[0] - n_sel, Q_BLOCK), NEG, BF16)


def _cmp_kernel(*refs, n_tiles):
    ib = pl.program_id(2)
    per_tile = CMP_TILE * CMP_STRIDE // Q_BLOCK
    lo = 0
    for nt in range(1, n_tiles + 1):
        hi = nt * per_tile

        @pl.when((ib >= lo) & (ib < hi))
        def _():
            _cmp_body(ib, nt, *refs)
        lo = hi


def _cmp_attention(qt, kc, vct, tc, overlap_t, *, B, S):
    nq = S // Q_BLOCK
    n_cmp = S // CMP_STRIDE
    n_sel = S // SEL_LEN
    n_tiles = n_cmp // CMP_TILE
    return pl.pallas_call(
        functools.partial(_cmp_kernel, n_tiles=n_tiles),
        grid=(N_KV, B, nq),
        in_specs=[pl.BlockSpec((GROUP * HEAD_DIM, Q_BLOCK), lambda h, b, i: (h, b * nq + i)),
                  pl.BlockSpec((None, None, n_cmp, HEAD_DIM), lambda h, b, i: (h, b, 0, 0)),
                  pl.BlockSpec((None, None, HEAD_DIM, n_cmp), lambda h, b, i: (h, b, 0, 0)),
                  pl.BlockSpec((None, CMP_NEAR + 2, CMP_TILE, QG), lambda h, b, i: (h, 0, 0, 0)),
                  pl.BlockSpec((n_sel, n_cmp), lambda h, b, i: (0, 0))],
        out_specs=[pl.BlockSpec((None, None, HEAD_DIM, QG), lambda h, b, i: (h, b * nq + i, 0, 0)),
                   pl.BlockSpec((None, None, n_sel, Q_BLOCK), lambda h, b, i: (h, b, 0, i))],
        out_shape=[jax.ShapeDtypeStruct((N_KV, B * nq, HEAD_DIM, QG), F32),
                   jax.ShapeDtypeStruct((N_KV, B, n_sel, S), BF16)],
        compiler_params=_cparams(("parallel", "parallel", "parallel")),
        name="nsa_cmp_select",
    )(qt, kc, vct, tc, overlap_t)


def _sel_kernel(q_ref, mb_ref, k_ref, oh_ref, vt_ref, ts_ref, o_ref, m_ref, acc_ref, sa_ref, sb_ref):
    ib = pl.program_id(2)
    per = SEL_CHUNK // Q_BLOCK
    cq = ib // per
    sub = ib % per
    qt = _query_cols(q_ref)
    blocks_per_chunk = SEL_CHUNK // SEL_LEN
    chunks_per_group = LANE // blocks_per_chunk

    m_ref[...] = jnp.full(m_ref.shape, NEG, F32)
    acc_ref[...] = jnp.zeros_like(acc_ref)

    qq = cq // SEL_FAR
    rows = SEL_FAR * SEL_CHUNK

    def raw_scores(quad):
        c0 = pl.multiple_of(quad * SEL_FAR, SEL_FAR)
        grp = c0 // chunks_per_group
        mb = mb_ref[pl.ds(pl.multiple_of(grp * LANE, LANE), LANE), :]
        qa = jnp.concatenate([qt, jnp.concatenate([mb] * GROUP, axis=1)], axis=0)
        o0 = pl.multiple_of(c0 % oh_ref.shape[0], SEL_FAR)
        keys = jnp.concatenate([k_ref[pl.ds(c0, SEL_FAR)].reshape(rows, HEAD_DIM),
                                oh_ref[pl.ds(o0, SEL_FAR)].reshape(rows, HEAD_DIM)], axis=1)
        return jnp.dot(keys, qa, preferred_element_type=F32)

    def update(s, quad):
        c0 = quad * SEL_FAR
        v = jnp.concatenate([vt_ref[c0 + k] for k in range(SEL_FAR)], axis=1)
        m_old = m_ref[0:1, :]
        m_new = jnp.maximum(m_old, jnp.max(s, axis=0, keepdims=True))
        p = jnp.exp2((s - m_new).astype(BF16))
        acc_ref[...] = (jnp.exp2(m_old - m_new) * acc_ref[...]
                        + jnp.dot(v, p, preferred_element_type=F32))
        m_ref[...] = jnp.broadcast_to(m_new, m_ref.shape)

    def bias(quad, live):
        tiles = []
        for k in range(SEL_FAR):
            near = jnp.clip(quad * SEL_FAR + k - cq + SEL_NEAR, 0, SEL_NEAR + 1)
            tiles.append(ts_ref[sub, jnp.where(live, near, SEL_NEAR + 1)])
        return jnp.concatenate(tiles, axis=0)

    def finish(near0_ref):
        s1 = raw_scores(qq)
        update(near0_ref[...] + bias(qq - 1, qq >= 1), jnp.maximum(qq - 1, 0))
        update(s1 + bias(qq, True), qq)
        o_ref[...] = acc_ref[:HEAD_DIM] / jnp.maximum(acc_ref[HEAD_DIM:HEAD_DIM + 1], 1e-30)

    n_far = jnp.maximum(qq - 1, 0)
    sa_ref[...] = raw_scores(0)

    def far_pair(i, carry):
        sb_ref[...] = raw_scores(2 * i + 1)
        update(sa_ref[...], 2 * i)
        sa_ref[...] = raw_scores(2 * i + 2)
        update(sb_ref[...], 2 * i + 1)
        return carry

    lax.fori_loop(0, n_far // 2, far_pair, 0)

    @pl.when(n_far % 2 == 1)
    def _():
        sb_ref[...] = raw_scores(n_far)
        update(sa_ref[...], n_far - 1)
        finish(sb_ref)

    @pl.when(n_far % 2 == 0)
    def _():
        finish(sa_ref)


def _sel_attention(qt, mbt, k_sel, onehot, vt, ts, *, B, S):
    nq = S // Q_BLOCK
    n_sel = mbt.shape[2]
    nc = S // SEL_CHUNK
    return pl.pallas_call(
        _sel_kernel,
        grid=(N_KV, B, nq),
        in_specs=[pl.BlockSpec((GROUP * HEAD_DIM, Q_BLOCK), lambda h, b, i: (h, b * nq + i)),
                  pl.BlockSpec((None, None, n_sel, Q_BLOCK), lambda h, b, i: (h, b, 0, i)),
                  pl.BlockSpec((None, None, nc, SEL_CHUNK, HEAD_DIM), lambda h, b, i: (h, b, 0, 0, 0)),
                  pl.BlockSpec(onehot.shape, lambda h, b, i: (0, 0, 0)),
                  pl.BlockSpec((None, None, nc, V_ROWS, SEL_CHUNK), lambda h, b, i: (h, b, 0, 0, 0)),
                  pl.BlockSpec((None, SEL_CHUNK // Q_BLOCK, SEL_NEAR + 2, SEL_CHUNK, QG),
                               lambda h, b, i: (h, 0, 0, 0, 0))],
        out_specs=pl.BlockSpec((None, None, HEAD_DIM, QG), lambda h, b, i: (h, b * nq + i, 0, 0)),
        out_shape=jax.ShapeDtypeStruct((N_KV, B * nq, HEAD_DIM, QG), F32),
        scratch_shapes=[pltpu.VMEM((8, QG), F32), pltpu.VMEM((V_ROWS, QG), F32),
                        pltpu.VMEM((SEL_FAR * SEL_CHUNK, QG), F32),
                        pltpu.VMEM((SEL_FAR * SEL_CHUNK, QG), F32)],
        compiler_params=_cparams(("parallel", "parallel", "arbitrary")),
        name="nsa_selected",
    )(qt, mbt, k_sel, onehot, vt, ts)


def _win_kernel(*refs):
    nb = WINDOW // Q_BLOCK + 1
    q_ref = refs[0]
    k_refs = refs[1:1 + nb]
    v_refs = refs[1 + nb:1 + 2 * nb]
    tw_ref, gate_ref, oc_ref, os_ref, o_ref = refs[1 + 2 * nb:]
    ib = pl.program_id(2)
    qt = _query_cols(q_ref)
    parts = []
    for i in range(nb):
        s = jnp.dot(k_refs[i][...], qt, preferred_element_type=F32)
        s = s + tw_ref[i * Q_BLOCK:(i + 1) * Q_BLOCK, :]
        parts.append(jnp.where(ib - (nb - 1) + i >= 0, s, NEG))
    s = jnp.concatenate(parts, axis=0)
    m = jnp.max(s, axis=0, keepdims=True)
    pb = jnp.exp2((s - m).astype(BF16))
    ow = jnp.dot(v_refs[0][...], pb[0:Q_BLOCK], preferred_element_type=F32)
    for i in range(1, nb):
        ow = ow + jnp.dot(v_refs[i][...], pb[i * Q_BLOCK:(i + 1) * Q_BLOCK], preferred_element_type=F32)
    ow = ow[:HEAD_DIM] / jnp.maximum(ow[HEAD_DIM:HEAD_DIM + 1], 1e-30)

    def gate(j):
        return jnp.concatenate([gate_ref[j, g:g + 1, :] for g in range(GROUP)], axis=1)

    o = gate(0) * oc_ref[...] + gate(1) * os_ref[...] + gate(2) * ow
    for g in range(GROUP):
        o_ref[:, g * HEAD_DIM:(g + 1) * HEAD_DIM] = o[:, g * Q_BLOCK:(g + 1) * Q_BLOCK].T.astype(BF16)


def _win_attention(qt, kw, vwt, tw, gates_t, oc, osel, *, B, S):
    nq = S // Q_BLOCK
    nb = WINDOW // Q_BLOCK + 1

    def kmap(i_off):
        return lambda h, b, i: (N_KV + h, b * nq + jnp.maximum(i - (nb - 1) + i_off, 0), 0)

    def vmap_(i_off):
        return lambda h, b, i: (h, 0, b * nq + jnp.maximum(i - (nb - 1) + i_off, 0))

    blk = pl.BlockSpec((None, None, HEAD_DIM, QG), lambda h, b, i: (h, b * nq + i, 0, 0))
    return pl.pallas_call(
        _win_kernel,
        grid=(N_KV, B, nq),
        in_specs=([pl.BlockSpec((GROUP * HEAD_DIM, Q_BLOCK), lambda h, b, i: (h, b * nq + i))]
                  + [pl.BlockSpec((None, Q_BLOCK, HEAD_DIM), kmap(j)) for j in range(nb)]
                  + [pl.BlockSpec((None, V_ROWS, Q_BLOCK), vmap_(j)) for j in range(nb)]
                  + [pl.BlockSpec((None, WINDOW + Q_BLOCK, QG), lambda h, b, i: (h, 0, 0)),
                     pl.BlockSpec((3, None, GROUP, Q_BLOCK), lambda h, b, i: (0, h, 0, b * nq + i)),
                     blk, blk]),
        out_specs=pl.BlockSpec((Q_BLOCK, GROUP * HEAD_DIM), lambda h, b, i: (b * nq + i, h)),
        out_shape=jax.ShapeDtypeStruct((B * S, N_HEADS * HEAD_DIM), BF16),
        compiler_params=_cparams(("parallel", "parallel", "parallel")),
        name="nsa_window_merge",
    )(qt, *([kw] * nb), *([vwt] * nb), tw, gates_t, oc, osel)


def _nsa_layer(x32, xb, p, tables, consts, ln_g, ln_b, *, B, S):
    T = B * S
    tc, ts, tw = tables
    overlap_t, sel_onehot = consts
    kd = N_KV * HEAD_DIM
    qv = _proj_nt(p["w_qv_t"], xb, out_dtype=BF16, scale=HEAD_DIM ** -0.5 * LOG2E,
                  n_scaled=N_HEADS * HEAD_DIM // 512)
    qt = qv
    k2 = _proj_nn(xb, p["w_k"], out_dtype=BF16, head_major=True)
    cmp = _proj_nn(xb, p["w_cmp"], out_dtype=F32, head_major=True)
    gates_t = _proj_nt(p["w_gate_t"], xb, out_dtype=F32, act="sigmoid")
    gates_t = gates_t.reshape(3, N_KV, GROUP, T)

    cmp = cmp.reshape(2 * N_KV, B, S, HEAD_DIM)
    kc = _compress(cmp, 0, p["pe_k"], p["w1_k"], p["w2_k"], B=B, S=S, transposed=False)
    vct = _compress(cmp, N_KV, p["pe_v"], p["w1_v"], p["w2_v"], B=B, S=S, transposed=True)
    oc, mbt = _cmp_attention(qt, kc, vct, tc, overlap_t, B=B, S=S)
    if mbt.shape[2] % LANE:
        mbt = jnp.pad(mbt, ((0, 0), (0, 0), (0, -mbt.shape[2] % LANE), (0, 0)), constant_values=NEG)

    nc = S // SEL_CHUNK
    k_sel = k2.reshape(2 * N_KV, B, nc, SEL_CHUNK, HEAD_DIM)
    v_sel_t = qv[N_HEADS * HEAD_DIM:N_HEADS * HEAD_DIM + kd].reshape(N_KV, HEAD_DIM, B, nc, SEL_CHUNK)
    v_sel_t = v_sel_t.transpose(0, 2, 3, 1, 4)
    ones_rows = (jnp.arange(V_ROWS - HEAD_DIM) == 0).astype(BF16)[:, None]
    v_sel_t = jnp.concatenate(
        [v_sel_t, jnp.broadcast_to(ones_rows, v_sel_t.shape[:3] + (V_ROWS - HEAD_DIM, SEL_CHUNK))], axis=3)
    osel = _sel_attention(qt, mbt, k_sel, sel_onehot, v_sel_t, ts, B=B, S=S)

    v_win_t = qv[N_HEADS * HEAD_DIM + kd:].reshape(N_KV, HEAD_DIM, T)
    v_win_t = jnp.concatenate(
        [v_win_t, jnp.broadcast_to(ones_rows, (N_KV, V_ROWS - HEAD_DIM, T))], axis=1)
    attn = _win_attention(qt, k2, v_win_t, tw, gates_t, oc, osel, B=B, S=S)
    return _proj_ln(attn, p["w_out"], x32, ln_g, ln_b, *p["router"])


def _nsa_params(w_in, pe_k, w1_k, w2_k, pe_v, w1_v, w2_v, w_out):
    D = w_in.shape[0]
    qd, kd = N_HEADS * HEAD_DIM, N_KV * HEAD_DIM
    w_kv = w_in[:, qd:qd + 6 * kd].reshape(D, 6, kd)
    gcol = np.array([[[(kv * GROUP + g) * 3 + j for g in range(GROUP)] for kv in range(N_KV)]
                     for j in range(3)]).reshape(-1)
    return dict(
        w_qv_t=jnp.concatenate([w_in[:, :qd], w_kv[:, 3], w_kv[:, 5]], axis=1).T.astype(BF16),
        w_k=jnp.concatenate([w_kv[:, 2], w_kv[:, 4]], axis=1).astype(BF16),
        w_cmp=jnp.concatenate([w_kv[:, 0], w_kv[:, 1]], axis=1).astype(BF16),
        w_gate_t=w_in[:, qd + 6 * kd:][:, gcol].T.astype(BF16),
        pe_k=pe_k, w1_k=w1_k.astype(BF16), w2_k=w2_k.astype(BF16),
        pe_v=pe_v, w1_v=w1_v.astype(BF16), w2_v=w2_v.astype(BF16),
        w_out=w_out.astype(BF16))


def _nsa_constants(S):
    n_cmp, n_sel = S // CMP_STRIDE, S // SEL_LEN
    cj = np.arange(n_cmp) * CMP_STRIDE
    sb = np.arange(n_sel) * SEL_LEN
    ov = np.maximum(np.minimum(cj[None, :] + CMP_LEN, sb[:, None] + SEL_LEN)
                    - np.maximum(cj[None, :], sb[:, None]), 0).astype(np.float32) / CMP_LEN
    ov[:, n_cmp - 1] = 0.0
    key = np.arange(SEL_CHUNK)
    chunk = np.arange(min(S // SEL_CHUNK, LANE * SEL_LEN // SEL_CHUNK))
    blk = (chunk[:, None] * (SEL_CHUNK // SEL_LEN) + key[None, :] // SEL_LEN) % LANE
    onehot = (blk[:, :, None] == np.arange(LANE)[None, None, :]).astype(np.float32)
    return jnp.asarray(ov, BF16), jnp.asarray(onehot, BF16)


def kernel(x, rel_bias, router_w, router_b, ln1_g, ln1_b, ln2_g, ln2_b, lru_w_in, lru_conv_w,
           lru_conv_b, lru_w_a, lru_b_a, lru_w_i, lru_b_i, lru_lambda, lru_w_out, nsa_w_in,
           nsa_pe_k, nsa_w1_k, nsa_w2_k, nsa_pe_v, nsa_w1_v, nsa_w2_v, nsa_w_out,
           moe_w_gate, moe_w_up, moe_w_down):
    B, S, D = x.shape
    T = B * S
    assert D == D_MODEL and S % (CMP_TILE * CMP_STRIDE) == 0 and S // SEL_LEN >= SEL_TOP
    qd, kd = N_HEADS * HEAD_DIM, N_KV * HEAD_DIM
    tables = _bias_tables(rel_bias, S)
    consts = _nsa_constants(S)
    router = _router_params(router_w, router_b)
    moe_bf16 = [w.astype(BF16) for w in (moe_w_gate, moe_w_up, moe_w_down)]

    x32 = x.reshape(T, D)
    xb = x32.astype(BF16)
    for layer in range(DEPTH):
        j = layer // 2
        if layer % 2 == 0:
            w_in = lru_w_in[j].astype(BF16)
            p = dict(w_gate=w_in[:, :D], w_y=w_in[:, D:], conv_w=lru_conv_w[j], conv_b=lru_conv_b[j],
                     w_a=lru_w_a[j].astype(BF16), b_a=lru_b_a[j], w_i=lru_w_i[j].astype(BF16),
                     b_i=lru_b_i[j], lam=lru_lambda[j], w_out=lru_w_out[j].astype(BF16), router=router)
            x32, xb, logits = _rglru_layer(x32, xb, p, ln1_g[layer], ln1_b[layer], B=B, S=S)
        else:
            p = _nsa_params(nsa_w_in[j], nsa_pe_k[j], nsa_w1_k[j], nsa_w2_k[j], nsa_pe_v[j],
                            nsa_w1_v[j], nsa_w2_v[j], nsa_w_out[j])
            p["router"] = router
            x32, xb, logits = _nsa_layer(x32, xb, p, tables, consts, ln1_g[layer], ln1_b[layer], B=B, S=S)
        x32, xb = _moe_layer(x32, logits, moe_bf16[0], moe_bf16[1], moe_bf16[2], layer,
                             ln2_g[layer], ln2_b[layer])
    return x32.reshape(B, S, D)
```

```python
import functools
import math

import numpy as np
import jax
import jax.numpy as jnp
from jax import lax
from jax.experimental import pallas as pl
from jax.experimental.pallas import tpu as pltpu

F32 = jnp.float32
BF16 = jnp.bfloat16

D_MODEL = 2048
DEPTH = 4
LRU_BLOCK = 256
LRU_NBLK = D_MODEL // LRU_BLOCK
CONV_W = 4
LRU_C = 8.0
HEAD_DIM = 128
N_HEADS = 16
N_KV = 4
GROUP = 4
CMP_LEN = 32
CMP_STRIDE = 16
SEL_LEN = 64
SEL_TOP = 16
WINDOW = 512
Q_BLOCK = 128
NUM_BUCKETS = 32
MAX_DISTANCE = 1024
N_EXPERTS = 16
N_GROUPS = 4
EXPERTS_PER_GROUP = 4
D_EXPERT = 1024
ALPHA = (2 * DEPTH) ** 0.25
LN_EPS = 1e-5
NEG = -1e30
BIG = 1e9

LANE = 128
QG = GROUP * Q_BLOCK
EXPERT_ROWS = 256
SEL_CHUNK = 256
SEL_NEAR = 5
SEL_FAR = 4
V_ROWS = HEAD_DIM + 16
WIN_PAIR = 2
CMP_PAIR = 2
LOG2E = math.log2(math.e)
CMP_TILE = 128
CMP_NEAR = 23
VMEM_LIMIT = 56 << 20


def _cparams(sem, vmem=VMEM_LIMIT):
    return pltpu.CompilerParams(dimension_semantics=sem, vmem_limit_bytes=vmem)


def _act(x, act):
    if act == "gelu":
        return jax.nn.gelu(x)
    if act == "sigmoid":
        return jax.nn.sigmoid(x)
    return x


def _proj_nn_kernel(x_ref, w_ref, o_ref, *, act, head_major):
    acc = jnp.dot(x_ref[...], w_ref[...], preferred_element_type=F32)
    acc = _act(acc, act)
    if head_major:
        for h in range(o_ref.shape[0]):
            o_ref[h] = acc[:, h * LANE:(h + 1) * LANE].astype(o_ref.dtype)
    else:
        o_ref[...] = acc.astype(o_ref.dtype)


def _proj_nn(x, w, *, out_dtype, act=None, head_major=False, tm=1024, tn=1024):
    T, K = x.shape
    N = w.shape[1]
    tm, tn = min(tm, T), min(tn, N)
    if head_major:
        out_shape = (N // LANE, T, LANE)
        out_spec = pl.BlockSpec((tn // LANE, tm, LANE), lambda j, i: (j, i, 0))
    else:
        out_shape = (T, N)
        out_spec = pl.BlockSpec((tm, tn), lambda j, i: (i, j))
    return pl.pallas_call(
        functools.partial(_proj_nn_kernel, act=act, head_major=head_major),
        grid=(N // tn, T // tm),
        in_specs=[pl.BlockSpec((tm, K), lambda j, i: (i, 0)),
                  pl.BlockSpec((K, tn), lambda j, i: (0, j))],
        out_specs=out_spec,
        out_shape=jax.ShapeDtypeStruct(out_shape, out_dtype),
        compiler_params=_cparams(("parallel", "parallel")),
        name="proj_nn",
    )(x, w)


def _proj_nt_kernel(wt_ref, x_ref, o_ref, *, act, scale, n_scaled):
    acc = lax.dot_general(wt_ref[...], x_ref[...], (((1,), (1,)), ((), ())),
                          preferred_element_type=F32)
    if n_scaled:
        acc = acc * jnp.where(pl.program_id(0) < n_scaled, scale, 1.0).astype(F32)
    o_ref[...] = _act(acc, act).astype(o_ref.dtype)


def _proj_nt(wt, x, *, out_dtype, act=None, scale=1.0, n_scaled=0, tm=1024, tn=512):
    N, K = wt.shape
    T = x.shape[0]
    tm, tn = min(tm, T), min(tn, N)
    return pl.pallas_call(
        functools.partial(_proj_nt_kernel, act=act, scale=scale, n_scaled=n_scaled),
        grid=(N // tn, T // tm),
        in_specs=[pl.BlockSpec((tn, K), lambda j, i: (j, 0)),
                  pl.BlockSpec((tm, K), lambda j, i: (i, 0))],
        out_specs=pl.BlockSpec((tn, tm), lambda j, i: (j, i)),
        out_shape=jax.ShapeDtypeStruct((N, T), out_dtype),
        compiler_params=_cparams(("parallel", "parallel")),
        name="proj_nt",
    )(wt, x)


def _layer_norm_rows(z, g, b):
    mu = jnp.mean(z, axis=-1, keepdims=True)
    zc = z - mu
    var = jnp.mean(zc * zc, axis=-1, keepdims=True)
    return zc * lax.rsqrt(var + LN_EPS) * g + b


def _proj_ln_kernel(a_ref, w_ref, res_ref, g_ref, b_ref, rw_ref, rb_ref, of_ref, ob_ref, lg_ref):
    acc = jnp.dot(a_ref[...], w_ref[...], preferred_element_type=F32)
    y = _layer_norm_rows(ALPHA * res_ref[...] + acc, g_ref[...], b_ref[...])
    of_ref[...] = y
    y_hi = y.astype(BF16)
    ob_ref[...] = y_hi
    y_lo = (y - y_hi.astype(F32)).astype(BF16)
    rows = y.shape[0]
    both = jnp.dot(jnp.concatenate([y_hi, y_lo], axis=0), rw_ref[0], preferred_element_type=F32)
    lg_ref[...] = (both[:rows] + both[rows:] + jnp.dot(y_hi, rw_ref[1], preferred_element_type=F32)
                   + rb_ref[...])


def _proj_ln(a, w, res, g, b, router_w, router_b, *, tm=256):
    T, K = a.shape
    D = w.shape[1]
    tm = min(tm, T)
    row = pl.BlockSpec((tm, D), lambda i: (i, 0))
    vec = pl.BlockSpec((1, D), lambda i: (0, 0))
    lgt = pl.BlockSpec((tm, LANE), lambda i: (i, 0))
    return pl.pallas_call(
        _proj_ln_kernel,
        grid=(T // tm,),
        in_specs=[pl.BlockSpec((tm, K), lambda i: (i, 0)),
                  pl.BlockSpec((K, D), lambda i: (0, 0)),
                  row, vec, vec,
                  pl.BlockSpec((2, D, LANE), lambda i: (0, 0, 0)),
                  pl.BlockSpec((1, LANE), lambda i: (0, 0))],
        out_specs=[row, row, lgt],
        out_shape=[jax.ShapeDtypeStruct((T, D), F32), jax.ShapeDtypeStruct((T, D), BF16),
                   jax.ShapeDtypeStruct((T, LANE), F32)],
        compiler_params=_cparams(("parallel",)),
        name="proj_ln",
    )(a, w, res, g.reshape(1, D), b.reshape(1, D), router_w, router_b)


def _lru_kernel(y_ref, gate_ref, cw_ref, cb_ref, wa_ref, ba_ref, wi_ref, bi_ref, lam_ref,
                o_ref, h_ref, tail_ref, *, rows):
    @pl.when(pl.program_id(2) == 0)
    def _():
        h_ref[...] = jnp.zeros_like(h_ref)
        tail_ref[...] = jnp.zeros_like(tail_ref)

    y = y_ref[...]
    tail = tail_ref[...]
    cw = cw_ref[...]
    row8 = lax.broadcasted_iota(jnp.int32, (8, LRU_BLOCK), 0)
    yc = cb_ref[...] + y * cw[CONV_W - 1:CONV_W, :]
    for d in range(1, CONV_W):
        ys = pltpu.roll(y, d, 0)
        first = jnp.where(row8 < d, pltpu.roll(tail, d, 0), ys[:8])
        ys = jnp.concatenate([first, ys[8:]], axis=0)
        yc = yc + ys * cw[CONV_W - 1 - d:CONV_W - d, :]
    tail_ref[...] = y[rows - 8:]

    yb = yc.astype(BF16)
    r = jax.nn.sigmoid(jnp.dot(yb, wa_ref[...], preferred_element_type=F32) + ba_ref[...])
    ig = jax.nn.sigmoid(jnp.dot(yb, wi_ref[...], preferred_element_type=F32) + bi_ref[...])
    z = -lam_ref[...]
    softplus = jnp.maximum(z, 0.0) + jnp.log1p(jnp.exp(-jnp.abs(z)))
    log_a = (-LRU_C * softplus) * r
    a = jnp.exp(log_a)
    h = jnp.sqrt(1.0 - a * a) * (ig * yc)

    rowi = lax.broadcasted_iota(jnp.int32, (rows, LRU_BLOCK), 0)
    s = 1
    while s < rows:
        if s < 8:
            keep = rowi >= s
            h = jnp.where(keep, h + a * pltpu.roll(h, s, 0), h)
            a = jnp.where(keep, a * pltpu.roll(a, s, 0), a)
        else:
            h = jnp.concatenate([h[:s], h[s:] + a[s:] * h[:rows - s]], axis=0)
            a = jnp.concatenate([a[:s], a[s:] * a[:rows - s]], axis=0)
        s *= 2
    h = h + a * h_ref[0:1, :]
    h_ref[...] = jnp.broadcast_to(h[rows - 1:rows, :], h_ref.shape)
    o_ref[...] = (h * gate_ref[...].astype(F32)).astype(BF16)


def _lru_core(y, gate, conv_w, conv_b, w_a, b_a, w_i, b_i, lam, *, B, S, rows=512):
    T, W = y.shape
    rows = min(rows, S)
    ns = S // rows
    blk = pl.BlockSpec((rows, LRU_BLOCK), lambda n, b, s: (b * ns + s, n))
    vec = pl.BlockSpec((1, LRU_BLOCK), lambda n, b, s: (0, n))
    wblk = pl.BlockSpec((None, LRU_BLOCK, LRU_BLOCK), lambda n, b, s: (n, 0, 0))
    return pl.pallas_call(
        functools.partial(_lru_kernel, rows=rows),
        grid=(LRU_NBLK, B, ns),
        in_specs=[blk, blk,
                  pl.BlockSpec((CONV_W, LRU_BLOCK), lambda n, b, s: (0, n)), vec,
                  wblk, vec, wblk, vec, vec],
        out_specs=blk,
        out_shape=jax.ShapeDtypeStruct((T, W), BF16),
        scratch_shapes=[pltpu.VMEM((8, LRU_BLOCK), F32), pltpu.VMEM((8, LRU_BLOCK), F32)],
        compiler_params=_cparams(("parallel", "parallel", "arbitrary")),
        name="lru_core",
    )(y, gate, conv_w, conv_b.reshape(1, W), w_a, b_a.reshape(1, W), w_i, b_i.reshape(1, W),
      lam.reshape(1, W))


def _rglru_layer(x32, xb, p, ln_g, ln_b, *, B, S):
    gate = _proj_nn(xb, p["w_gate"], out_dtype=BF16, act="gelu")
    y = _proj_nn(xb, p["w_y"], out_dtype=F32)
    hg = _lru_core(y, gate, p["conv_w"], p["conv_b"], p["w_a"], p["b_a"], p["w_i"], p["b_i"],
                   p["lam"], B=B, S=S)
    return _proj_ln(hg, p["w_out"], x32, ln_g, ln_b, *p["router"])


def _router_kernel(lg_ref, ids_ref, wts_ref, pos_ref, cnt_ref, carry_ref, *, tm):
    @pl.when(pl.program_id(0) == 0)
    def _():
        carry_ref[...] = jnp.zeros_like(carry_ref)

    logits = lg_ref[...].T[:N_EXPERTS]
    m = jnp.max(logits, axis=0, keepdims=True)
    e = jnp.exp(logits - m)
    p = e / jnp.sum(e, axis=0, keepdims=True)
    rows = [p[k:k + 1, :] for k in range(N_EXPERTS)]

    scores = []
    for g in range(N_GROUPS):
        a, b, c, d = rows[4 * g:4 * g + 4]
        hi1, lo1, hi2, lo2 = jnp.maximum(a, b), jnp.minimum(a, b), jnp.maximum(c, d), jnp.minimum(c, d)
        scores.append(jnp.maximum(hi1, hi2) + jnp.maximum(jnp.minimum(hi1, hi2), jnp.maximum(lo1, lo2)))
    best, gsel = scores[0], jnp.zeros((1, tm), jnp.int32)
    for g in range(1, N_GROUPS):
        upd = scores[g] > best
        gsel = jnp.where(upd, g, gsel)
        best = jnp.where(upd, scores[g], best)

    cand = []
    for k in range(EXPERTS_PER_GROUP):
        c = rows[k]
        for g in range(1, N_GROUPS):
            c = jnp.where(gsel == g, rows[4 * g + k], c)
        cand.append(c)
    v1, i1 = cand[0], jnp.zeros((1, tm), jnp.int32)
    for k in range(1, EXPERTS_PER_GROUP):
        upd = cand[k] > v1
        i1 = jnp.where(upd, k, i1)
        v1 = jnp.where(upd, cand[k], v1)
    v2, i2 = jnp.full((1, tm), -1.0, F32), jnp.zeros((1, tm), jnp.int32)
    for k in range(EXPERTS_PER_GROUP):
        c = jnp.where(i1 == k, -1.0, cand[k])
        upd = c > v2
        i2 = jnp.where(upd, k, i2)
        v2 = jnp.where(upd, c, v2)
    e0 = gsel * EXPERTS_PER_GROUP + i1
    e1 = gsel * EXPERTS_PER_GROUP + i2
    tot = v1 + v2

    eio = lax.broadcasted_iota(jnp.int32, (N_EXPERTS, tm), 0)
    hit0, hit1 = eio == e0, eio == e1
    onehot = jnp.where(hit0 | hit1, 1.0, 0.0)
    before = (lax.broadcasted_iota(jnp.int32, (tm, tm), 0)
              < lax.broadcasted_iota(jnp.int32, (tm, tm), 1))
    rank = jnp.dot(onehot.astype(BF16), jnp.where(before, 1.0, 0.0).astype(BF16),
                   preferred_element_type=F32) + carry_ref[:, 0:1]
    pos0 = jnp.sum(jnp.where(hit0, rank, 0.0), axis=0, keepdims=True)
    pos1 = jnp.sum(jnp.where(hit1, rank, 0.0), axis=0, keepdims=True)
    carry_ref[...] = carry_ref[...] + jnp.sum(onehot, axis=1, keepdims=True)
    cnt_ref[...] = carry_ref[...]

    zi = jnp.zeros((6, tm), jnp.int32)
    ids_ref[...] = jnp.concatenate([e0, e1, zi], axis=0)
    pos_ref[...] = jnp.concatenate([pos0.astype(jnp.int32), pos1.astype(jnp.int32), zi], axis=0)
    wts_ref[...] = jnp.concatenate([v1 / tot, v2 / tot, jnp.zeros((6, tm), F32)], axis=0)


def _router(logits, *, tm=512):
    T = logits.shape[0]
    tm = min(tm, T)
    tok = pl.BlockSpec((8, tm), lambda i: (0, i))
    return pl.pallas_call(
        functools.partial(_router_kernel, tm=tm),
        grid=(T // tm,),
        in_specs=[pl.BlockSpec((tm, LANE), lambda i: (i, 0))],
        out_specs=[tok, tok, tok, pl.BlockSpec((N_EXPERTS, LANE), lambda i: (0, 0))],
        out_shape=[jax.ShapeDtypeStruct((8, T), jnp.int32), jax.ShapeDtypeStruct((8, T), F32),
                   jax.ShapeDtypeStruct((8, T), jnp.int32),
                   jax.ShapeDtypeStruct((N_EXPERTS, LANE), F32)],
        scratch_shapes=[pltpu.VMEM((N_EXPERTS, LANE), F32)],
        compiler_params=_cparams(("arbitrary",)),
        name="moe_router",
    )(logits)


def _router_params(router_w, router_b):
    pad = LANE - N_EXPERTS
    w = jnp.pad(router_w, ((0, 0), (0, pad)))
    w_hi = w.astype(BF16)
    w_lo = (w - w_hi.astype(F32)).astype(BF16)
    return (jnp.stack([w_hi, w_lo]), jnp.pad(router_b, (0, pad)).reshape(1, LANE))


def _row_copy(src, src_row, dst, dst_row, sem, n=1):
    return pltpu.make_async_copy(src.at[pl.ds(src_row, n)], dst.at[pl.ds(dst_row, n)], sem)


def _row_tokens_kernel(n_ref, lo_ref, hi_ref, d0_ref, d1_ref, tok_ref):
    step = 8

    def clear(r, c):
        tok_ref[r] = 0
        return c

    def put(i, c):
        for k in range(step):
            t = i * step + k
            tok_ref[d0_ref[t]] = t
            tok_ref[d1_ref[t]] = t
        return c

    for e in range(lo_ref.shape[0]):
        lax.fori_loop(lo_ref[e], hi_ref[e], clear, 0)
    lax.fori_loop(0, n_ref[0] // step, put, 0)


def _row_tokens(d0, d1, free_lo, free_hi, n_rows):
    n_tok = d0.shape[0]
    assert n_tok % 8 == 0
    return pl.pallas_call(
        _row_tokens_kernel,
        grid_spec=pltpu.PrefetchScalarGridSpec(
            num_scalar_prefetch=5, grid=(1,), in_specs=[],
            out_specs=pl.BlockSpec(memory_space=pltpu.SMEM)),
        out_shape=jax.ShapeDtypeStruct((n_rows,), jnp.int32),
        compiler_params=pltpu.CompilerParams(dimension_semantics=("arbitrary",)),
        name="moe_row_tokens",
    )(jnp.array([n_tok], jnp.int32), free_lo, free_hi, d0, d1)


def _ffn_kernel(be_ref, nact_ref, tok_ref, x_hbm, wg_ref, wu_ref, wd_ref, y_ref, xbuf, sems):
    i = pl.program_id(0)
    n_act = nact_ref[0]
    R = EXPERT_ROWS

    def gather(block, slot):
        def issue(r, c):
            _row_copy(x_hbm, tok_ref[block * R + r], xbuf.at[slot], r, sems.at[slot]).start()
            return c
        lax.fori_loop(0, R, issue, 0, unroll=8)

    def run(slot):
        @pl.when(i + 1 < n_act)
        def _():
            gather(i + 1, 1 - slot)

        def drain(r, c):
            _row_copy(x_hbm, 0, xbuf.at[slot], 0, sems.at[slot]).wait()
            return c
        lax.fori_loop(0, R, drain, 0, unroll=8)
        xb = xbuf[slot].astype(BF16)
        g = jnp.dot(xb, wg_ref[...], preferred_element_type=F32)
        u = jnp.dot(xb, wu_ref[...], preferred_element_type=F32)
        h = (jax.nn.silu(g) * u).astype(BF16)
        y_ref[...] = jnp.dot(h, wd_ref[...], preferred_element_type=F32)

    @pl.when(i == 0)
    def _():
        gather(0, 0)

    for slot in range(2):
        @pl.when((i < n_act) & (i % 2 == slot))
        def _():
            run(slot)

    @pl.when(i >= n_act)
    def _():
        y_ref[...] = jnp.zeros_like(y_ref)


def _expert_ffn(x32, row_tok, blk_e, n_act, w_gate, w_up, w_down, layer):
    D = x32.shape[1]
    R = EXPERT_ROWS
    n_rows = row_tok.shape[0]
    wmap = lambda i, be, na, tok: (layer, be[i], 0, 0)
    return pl.pallas_call(
        _ffn_kernel,
        grid_spec=pltpu.PrefetchScalarGridSpec(
            num_scalar_prefetch=3, grid=(n_rows // R,),
            in_specs=[pl.BlockSpec(memory_space=pl.ANY),
                      pl.BlockSpec((None, None, D, D_EXPERT), wmap),
                      pl.BlockSpec((None, None, D, D_EXPERT), wmap),
                      pl.BlockSpec((None, None, D_EXPERT, D), wmap)],
            out_specs=pl.BlockSpec((R, D), lambda i, be, na, tok: (i, 0)),
            scratch_shapes=[pltpu.VMEM((2, R, D), F32), pltpu.SemaphoreType.DMA((2,))]),
        out_shape=jax.ShapeDtypeStruct((n_rows, D), F32),
        compiler_params=_cparams(("arbitrary",)),
        name="moe_ffn",
    )(blk_e, n_act, row_tok, x32, w_gate, w_up, w_down)


def _combine_kernel(d0_ref, d1_ref, res_ref, w_ref, g_ref, b_ref, y_hbm, of_ref, ob_ref,
                    buf, sem, *, tm):
    i = pl.program_id(0)

    def issue(r, c):
        t = i * tm + r
        _row_copy(y_hbm, d0_ref[t], buf.at[0], r, sem).start()
        _row_copy(y_hbm, d1_ref[t], buf.at[1], r, sem).start()
        return c

    def drain(r, c):
        _row_copy(y_hbm, 0, buf.at[0], 0, sem).wait()
        _row_copy(y_hbm, 0, buf.at[1], 0, sem).wait()
        return c

    lax.fori_loop(0, tm, issue, 0)
    lax.fori_loop(0, tm, drain, 0)
    w = w_ref[...]
    z = ALPHA * res_ref[...] + (w[:, 0:1] * buf[0] + w[:, 1:2] * buf[1])
    y = _layer_norm_rows(z, g_ref[...], b_ref[...])
    of_ref[...] = y
    ob_ref[...] = y.astype(BF16)


def _combine_ln(y, d0, d1, wts, res, g, b, *, tm=256):
    T, D = res.shape
    tm = min(tm, T)
    row = pl.BlockSpec((tm, D), lambda i, a, c: (i, 0))
    vec = pl.BlockSpec((1, D), lambda i, a, c: (0, 0))
    return pl.pallas_call(
        functools.partial(_combine_kernel, tm=tm),
        grid_spec=pltpu.PrefetchScalarGridSpec(
            num_scalar_prefetch=2, grid=(T // tm,),
            in_specs=[row, pl.BlockSpec((tm, 2), lambda i, a, c: (i, 0)), vec, vec,
                      pl.BlockSpec(memory_space=pl.ANY)],
            out_specs=[row, row],
            scratch_shapes=[pltpu.VMEM((2, tm, D), F32), pltpu.SemaphoreType.DMA(())]),
        out_shape=[jax.ShapeDtypeStruct((T, D), F32), jax.ShapeDtypeStruct((T, D), BF16)],
        compiler_params=_cparams(("arbitrary",)),
        name="moe_combine_ln",
    )(d0, d1, res, wts, g.reshape(1, D), b.reshape(1, D), y)


def _moe_layer(x32, logits, w_gate, w_up, w_down, layer, ln_g, ln_b):
    T = x32.shape[0]
    R = EXPERT_ROWS
    ids, wts, pos, cnt = _router(logits)
    counts = cnt[:, 0].astype(jnp.int32)
    padded = (counts + R - 1) // R * R
    p_end = jnp.cumsum(padded)
    offs = p_end - padded
    d0 = offs[ids[0]] + pos[0]
    d1 = offs[ids[1]] + pos[1]
    n_blocks = 2 * T // R + N_EXPERTS
    n_act = p_end[-1] // R
    step = jnp.arange(n_blocks, dtype=jnp.int32)
    live = step < n_act
    src = jnp.where(live, step, n_act - 1)
    blk_e = jnp.minimum(jnp.sum(src[:, None] >= (p_end // R)[None, :], axis=1),
                        N_EXPERTS - 1).astype(jnp.int32)
    n_rows = n_blocks * R
    free_lo = jnp.concatenate([offs + counts, p_end[-1:]]).astype(jnp.int32)
    free_hi = jnp.concatenate([p_end, jnp.array([n_rows], p_end.dtype)]).astype(jnp.int32)
    row_tok = _row_tokens(d0, d1, free_lo, free_hi, n_rows)
    y = _expert_ffn(x32, row_tok, blk_e, n_act.reshape(1).astype(jnp.int32), w_gate, w_up, w_down, layer)
    return _combine_ln(y, d0, d1, wts[:2].T, x32, ln_g, ln_b)


def _t5_bucket(dist):
    n = jnp.maximum(dist, 0)
    max_exact = NUM_BUCKETS // 2
    nf = jnp.maximum(n, 1).astype(F32)
    large = max_exact + (jnp.log(nf / max_exact) / math.log(MAX_DISTANCE / max_exact)
                         * (NUM_BUCKETS - max_exact)).astype(jnp.int32)
    large = jnp.minimum(large, NUM_BUCKETS - 1)
    return jnp.where(n < max_exact, n, large)


def _stack_heads(tbl):
    parts = tbl.reshape((N_KV, GROUP) + tbl.shape[1:])
    return jnp.concatenate([parts[:, g] for g in range(GROUP)], axis=-1)


def _bias_tables(rel_bias, S):
    off = CMP_STRIDE * CMP_TILE + Q_BLOCK
    n_dist = CMP_NEAR * Q_BLOCK
    bvec = rel_bias[_t5_bucket(jnp.arange(n_dist))].T
    last = rel_bias[NUM_BUCKETS - 1]
    masked = jnp.full((N_HEADS, off), NEG, F32)
    w_abs = jnp.concatenate([masked, bvec * LOG2E], axis=1)
    w_rel = jnp.concatenate([masked, (bvec - last[:, None]) * LOG2E], axis=1)
    w_win = jnp.where(jnp.arange(-off, n_dist) < WINDOW, w_abs, NEG)

    def toeplitz(w, first, step, n):
        lo = first - step * (n - 1) + off
        span = step * (n - 1) + Q_BLOCK
        u = w[:, lo:lo + span]
        width = span + step
        tiled = jnp.tile(u, (1, -(-n * width // span)))[:, :n * width]
        return tiled.reshape(N_HEADS, n, width)[:, ::-1, :Q_BLOCK]

    def hankel(u, n, m, step):
        span = step * (n - 1) + m
        width = span + step
        u = u[..., :span, :]
        tiled = jnp.tile(u, (1,) * (u.ndim - 2) + (-(-n * width // span), 1))[..., :n * width, :]
        return tiled.reshape(u.shape[:-2] + (n, width, u.shape[-1]))[..., :m, :]

    per = Q_BLOCK // CMP_STRIDE
    n_seq = per * (CMP_NEAR - 1) + (per - 1) + CMP_TILE
    lo = off - (CMP_LEN - 1) - CMP_STRIDE * (CMP_TILE - 1)
    seq = w_abs[:, lo:lo + CMP_STRIDE * n_seq].reshape(N_HEADS, n_seq, CMP_STRIDE)
    by_tile = hankel(seq, CMP_NEAR, per - 1 + CMP_TILE, per)
    near_c = hankel(by_tile, per, CMP_TILE, 1)
    near_c = near_c[:, :, :, ::-1, :].transpose(0, 1, 3, 2, 4).reshape(N_HEADS, CMP_NEAR, CMP_TILE, Q_BLOCK)
    const = lambda v: jnp.broadcast_to(v, (N_HEADS, 1, CMP_TILE, Q_BLOCK)).astype(F32)
    tc = jnp.concatenate([const(NEG), near_c, const(last[:, None, None, None] * LOG2E)], axis=1)
    zero = jnp.zeros((N_HEADS, 1, SEL_CHUNK, Q_BLOCK), F32)
    ts = jnp.stack([
        jnp.concatenate(
            [zero] + [toeplitz(w_rel, Q_BLOCK * sub + SEL_CHUNK * (SEL_NEAR - 1 - i), 1, SEL_CHUNK)[:, None]
                      for i in range(SEL_NEAR)] + [zero + NEG], axis=1)
        for sub in range(SEL_CHUNK // Q_BLOCK)], axis=1)
    n_keys = WINDOW + WIN_PAIR * Q_BLOCK
    tw = jnp.concatenate([_stack_heads(toeplitz(w_win, WINDOW + Q_BLOCK * c, 1, n_keys))
                          for c in range(WIN_PAIR)], axis=-1)
    return _stack_heads(tc), _stack_heads(ts), tw


def _compress_kernel(x_ref, pe_ref, w1_ref, w2_ref, o_ref, *, transposed):
    n = x_ref.shape[0] // CMP_STRIDE
    a = b = None
    for tok in range(CMP_STRIDE):
        xt = x_ref[pl.ds(tok, n, stride=CMP_STRIDE), :]
        as_first = (xt + pe_ref[tok:tok + 1, :]).astype(BF16)
        as_second = (xt + pe_ref[CMP_STRIDE + tok:CMP_STRIDE + tok + 1, :]).astype(BF16)
        da = jnp.dot(as_first, w1_ref[tok * HEAD_DIM:(tok + 1) * HEAD_DIM], preferred_element_type=F32)
        db = jnp.dot(as_second, w1_ref[(CMP_STRIDE + tok) * HEAD_DIM:(CMP_STRIDE + tok + 1) * HEAD_DIM],
                     preferred_element_type=F32)
        a = da if a is None else a + da
        b = db if b is None else b + db
    pre = a + pltpu.roll(b, n - 1, 0)
    rowi = lax.broadcasted_iota(jnp.int32, pre.shape, 0)
    hid = jnp.where(rowi < n - 1, jax.nn.gelu(pre), 0.0).astype(BF16)
    if transposed:
        out = lax.dot_general(w2_ref[...], hid, (((1,), (1,)), ((), ())), preferred_element_type=F32)
    else:
        out = jnp.dot(hid, w2_ref[...], preferred_element_type=F32)
    o_ref[...] = out.astype(BF16)


def _compress(slabs, first, pe, w1, w2, *, B, S, transposed):
    n = S // CMP_STRIDE
    wide = CMP_STRIDE * HEAD_DIM
    out_shape = (N_KV, B, HEAD_DIM, n) if transposed else (N_KV, B, n, HEAD_DIM)
    w2_in = w2.T if transposed else w2
    return pl.pallas_call(
        functools.partial(_compress_kernel, transposed=transposed),
        grid=(N_KV, B),
        in_specs=[pl.BlockSpec((None, None, S, HEAD_DIM), lambda h, b: (first + h, b, 0, 0)),
                  pl.BlockSpec((CMP_LEN, HEAD_DIM), lambda h, b: (0, 0)),
                  pl.BlockSpec((2 * wide, HEAD_DIM), lambda h, b: (0, 0)),
                  pl.BlockSpec((HEAD_DIM, HEAD_DIM), lambda h, b: (0, 0))],
        out_specs=pl.BlockSpec((None, None) + out_shape[2:], lambda h, b: (h, b, 0, 0)),
        out_shape=jax.ShapeDtypeStruct(out_shape, BF16),
        compiler_params=_cparams(("parallel", "parallel")),
        name="nsa_compress",
    )(slabs, pe, w1, w2_in)


def _query_cols(q_ref):
    return jnp.concatenate([q_ref[g * HEAD_DIM:(g + 1) * HEAD_DIM, :] for g in range(GROUP)], axis=1)


def _cmp_body(ib, n_tiles, q_ref, kc_ref, vct_ref, tc_ref, ov_ref, oc_ref, mb_ref):
    n_cmp = n_tiles * CMP_TILE
    n_sel = n_cmp * CMP_STRIDE // SEL_LEN
    nq = CMP_PAIR * Q_BLOCK
    qt = jnp.concatenate([q_ref[g * HEAD_DIM:(g + 1) * HEAD_DIM, c * Q_BLOCK:(c + 1) * Q_BLOCK]
                          for c in range(CMP_PAIR) for g in range(GROUP)], axis=1)
    s = jnp.dot(kc_ref[:n_cmp], qt, preferred_element_type=F32)
    tiles = []
    for cb in range(n_tiles):
        bias = [tc_ref[jnp.clip(ib + c - (CMP_TILE * CMP_STRIDE // Q_BLOCK) * cb + 1, 0, CMP_NEAR + 1)]
                for c in range(CMP_PAIR)]
        tiles.append(s[cb * CMP_TILE:(cb + 1) * CMP_TILE] + jnp.concatenate(bias, axis=1))
    s = jnp.concatenate(tiles, axis=0)
    m = jnp.max(s, axis=0, keepdims=True)
    p = jnp.exp2(s - m)
    inv = jnp.where(m > 0.5 * NEG, 1.0 / jnp.maximum(jnp.sum(p, axis=0, keepdims=True), 1e-30), 0.0)
    p = p * inv
    oc = jnp.dot(vct_ref[:, :n_cmp], p.astype(BF16), preferred_element_type=F32)
    for c in range(CMP_PAIR):
        oc_ref[c] = oc[:, c * QG:(c + 1) * QG]
    ov = ov_ref[:n_sel, :n_cmp]

    sums = []
    for c in range(CMP_PAIR):
        acc = p[:, c * QG:c * QG + Q_BLOCK]
        for g in range(1, GROUP):
            acc = acc + p[:, c * QG + g * Q_BLOCK:c * QG + (g + 1) * Q_BLOCK]
        sums.append(acc)
    psum = jnp.concatenate(sums, axis=1)
    p_hi = psum.astype(BF16)
    p_lo = (psum - p_hi.astype(F32)).astype(BF16)
    imp = (jnp.dot(ov, p_hi, preferred_element_type=F32)
           + jnp.dot(ov, p_lo, preferred_element_type=F32))
    blk = lax.broadcasted_iota(jnp.int32, (n_sel, nq), 0)
    t = ib * Q_BLOCK + lax.broadcasted_iota(jnp.int32, (n_sel, nq), 1)
    cur = jnp.right_shift(t, int(math.log2(SEL_LEN)))
    ok = blk * SEL_LEN <= t
    forced = ok & ((blk == 0) | (blk == cur) | (blk == cur - 1))
    score = jnp.where(forced, -jnp.inf, jnp.where(ok, imp, NEG))
    taken = forced
    for _ in range(SEL_TOP - 3):
        top = jnp.max(score, axis=0, keepdims=True)
        first = jnp.min(jnp.where(score == top, blk, n_sel), axis=0, keepdims=True)
        pick = blk == first
        taken = taken | pick
        score = jnp.where(pick, -jnp.inf, score)
    mb_ref[:n_sel] = jnp.where(taken & ok, 0.0, NEG).astype(BF16)
    if n_sel < mb_ref.shape[0]:
        mb_ref[n_sel:] = jnp.full((mb_ref.shape[0] - n_sel, nq), NEG, BF16)


def _cmp_kernel(*refs, n_tiles):
    ib = CMP_PAIR * pl.program_id(2)
    per_tile = CMP_TILE * CMP_STRIDE // Q_BLOCK
    lo = 0
    for nt in range(1, n_tiles + 1):
        hi = nt * per_tile

        @pl.when((ib >= lo) & (ib < hi))
        def _():
            _cmp_body(ib, nt, *refs)
        lo = hi


def _cmp_attention(qt, kc, vct, tc, overlap_t, *, B, S):
    nq = S // Q_BLOCK
    npair = nq // CMP_PAIR
    n_cmp = S // CMP_STRIDE
    n_sel = S // SEL_LEN
    n_tiles = n_cmp // CMP_TILE
    return pl.pallas_call(
        functools.partial(_cmp_kernel, n_tiles=n_tiles),
        grid=(N_KV, B, npair),
        in_specs=[pl.BlockSpec((GROUP * HEAD_DIM, CMP_PAIR * Q_BLOCK), lambda h, b, i: (h, b * npair + i)),
                  pl.BlockSpec((None, None, n_cmp, HEAD_DIM), lambda h, b, i: (h, b, 0, 0)),
                  pl.BlockSpec((None, None, HEAD_DIM, n_cmp), lambda h, b, i: (h, b, 0, 0)),
                  pl.BlockSpec((None, CMP_NEAR + 2, CMP_TILE, QG), lambda h, b, i: (h, 0, 0, 0)),
                  pl.BlockSpec((n_sel, n_cmp), lambda h, b, i: (0, 0))],
        out_specs=[pl.BlockSpec((None, CMP_PAIR, HEAD_DIM, QG), lambda h, b, i: (h, b * npair + i, 0, 0)),
                   pl.BlockSpec((None, None, n_sel, CMP_PAIR * Q_BLOCK), lambda h, b, i: (h, b, 0, i))],
        out_shape=[jax.ShapeDtypeStruct((N_KV, B * nq, HEAD_DIM, QG), F32),
                   jax.ShapeDtypeStruct((N_KV, B, n_sel, S), BF16)],
        compiler_params=_cparams(("parallel", "parallel", "parallel")),
        name="nsa_cmp_select",
    )(qt, kc, vct, tc, overlap_t)


def _sel_kernel(q_ref, mb_ref, k_ref, oh_ref, vt_ref, ts_ref, o_ref, m_ref, acc_ref, sa_ref, sb_ref):
    ib = pl.program_id(2)
    per = SEL_CHUNK // Q_BLOCK
    cq = ib // per
    sub = ib % per
    qt = _query_cols(q_ref)
    blocks_per_chunk = SEL_CHUNK // SEL_LEN
    chunks_per_group = LANE // blocks_per_chunk

    m_ref[...] = jnp.full(m_ref.shape, NEG, F32)
    acc_ref[...] = jnp.zeros_like(acc_ref)

    qq = cq // SEL_FAR
    rows = SEL_FAR * SEL_CHUNK

    def raw_scores(quad):
        c0 = pl.multiple_of(quad * SEL_FAR, SEL_FAR)
        grp = c0 // chunks_per_group
        mb = mb_ref[pl.ds(pl.multiple_of(grp * LANE, LANE), LANE), :]
        qa = jnp.concatenate([qt, jnp.concatenate([mb] * GROUP, axis=1)], axis=0)
        o0 = pl.multiple_of(c0 % oh_ref.shape[0], SEL_FAR)
        keys = jnp.concatenate([k_ref[pl.ds(c0, SEL_FAR)].reshape(rows, HEAD_DIM),
                                oh_ref[pl.ds(o0, SEL_FAR)].reshape(rows, HEAD_DIM)], axis=1)
        return jnp.dot(keys, qa, preferred_element_type=F32)

    def update(s, quad):
        c0 = quad * SEL_FAR
        v = jnp.concatenate([vt_ref[c0 + k] for k in range(SEL_FAR)], axis=1)
        m_old = m_ref[0:1, :]
        m_new = jnp.maximum(m_old, jnp.max(s, axis=0, keepdims=True))
        p = jnp.exp2((s - m_new).astype(BF16))
        acc_ref[...] = (jnp.exp2(m_old - m_new) * acc_ref[...]
                        + jnp.dot(v, p, preferred_element_type=F32))
        m_ref[...] = jnp.broadcast_to(m_new, m_ref.shape)

    def bias(quad, live):
        tiles = []
        for k in range(SEL_FAR):
            near = jnp.clip(quad * SEL_FAR + k - cq + SEL_NEAR, 0, SEL_NEAR + 1)
            tiles.append(ts_ref[sub, jnp.where(live, near, SEL_NEAR + 1)])
        return jnp.concatenate(tiles, axis=0)

    def finish(near0_ref):
        s1 = raw_scores(qq)
        update(near0_ref[...] + bias(qq - 1, qq >= 1), jnp.maximum(qq - 1, 0))
        update(s1 + bias(qq, True), qq)
        o_ref[...] = acc_ref[:HEAD_DIM] / jnp.maximum(acc_ref[HEAD_DIM:HEAD_DIM + 1], 1e-30)

    n_far = jnp.maximum(qq - 1, 0)
    sa_ref[...] = raw_scores(0)

    def far_pair(i, carry):
        sb_ref[...] = raw_scores(2 * i + 1)
        update(sa_ref[...], 2 * i)
        sa_ref[...] = raw_scores(2 * i + 2)
        update(sb_ref[...], 2 * i + 1)
        return carry

    lax.fori_loop(0, n_far // 2, far_pair, 0)

    @pl.when(n_far % 2 == 1)
    def _():
        sb_ref[...] = raw_scores(n_far)
        update(sa_ref[...], n_far - 1)
        finish(sb_ref)

    @pl.when(n_far % 2 == 0)
    def _():
        finish(sa_ref)


def _sel_attention(qt, mbt, k_sel, onehot, vt, ts, *, B, S):
    nq = S // Q_BLOCK
    n_sel = mbt.shape[2]
    nc = S // SEL_CHUNK
    return pl.pallas_call(
        _sel_kernel,
        grid=(N_KV, B, nq),
        in_specs=[pl.BlockSpec((GROUP * HEAD_DIM, Q_BLOCK), lambda h, b, i: (h, b * nq + i)),
                  pl.BlockSpec((None, None, n_sel, Q_BLOCK), lambda h, b, i: (h, b, 0, i)),
                  pl.BlockSpec((None, None, nc, SEL_CHUNK, HEAD_DIM), lambda h, b, i: (h, b, 0, 0, 0)),
                  pl.BlockSpec(onehot.shape, lambda h, b, i: (0, 0, 0)),
                  pl.BlockSpec((None, None, nc, V_ROWS, SEL_CHUNK), lambda h, b, i: (h, b, 0, 0, 0)),
                  pl.BlockSpec((None, SEL_CHUNK // Q_BLOCK, SEL_NEAR + 2, SEL_CHUNK, QG),
                               lambda h, b, i: (h, 0, 0, 0, 0))],
        out_specs=pl.BlockSpec((None, None, HEAD_DIM, QG), lambda h, b, i: (h, b * nq + i, 0, 0)),
        out_shape=jax.ShapeDtypeStruct((N_KV, B * nq, HEAD_DIM, QG), F32),
        scratch_shapes=[pltpu.VMEM((8, QG), F32), pltpu.VMEM((V_ROWS, QG), F32),
                        pltpu.VMEM((SEL_FAR * SEL_CHUNK, QG), F32),
                        pltpu.VMEM((SEL_FAR * SEL_CHUNK, QG), F32)],
        compiler_params=_cparams(("parallel", "parallel", "arbitrary")),
        name="nsa_selected",
    )(qt, mbt, k_sel, onehot, vt, ts)


def _win_kernel(*refs):
    nk = WINDOW // Q_BLOCK + WIN_PAIR
    q_ref = refs[0]
    k_refs = refs[1:1 + nk]
    v_refs = refs[1 + nk:1 + 2 * nk]
    tw_ref, gate_ref, oc_ref, os_ref, o_ref = refs[1 + 2 * nk:]
    first = WIN_PAIR * pl.program_id(2) - WINDOW // Q_BLOCK
    qt = jnp.concatenate([q_ref[g * HEAD_DIM:(g + 1) * HEAD_DIM, c * Q_BLOCK:(c + 1) * Q_BLOCK]
                          for c in range(WIN_PAIR) for g in range(GROUP)], axis=1)
    parts = []
    for i in range(nk):
        s = jnp.dot(k_refs[i][...], qt, preferred_element_type=F32)
        s = s + tw_ref[i * Q_BLOCK:(i + 1) * Q_BLOCK, :]
        parts.append(jnp.where(first + i >= 0, s, NEG))
    s = jnp.concatenate(parts, axis=0)
    m = jnp.max(s, axis=0, keepdims=True)
    pb = jnp.exp2((s - m).astype(BF16))
    ow = jnp.dot(v_refs[0][...], pb[0:Q_BLOCK], preferred_element_type=F32)
    for i in range(1, nk):
        ow = ow + jnp.dot(v_refs[i][...], pb[i * Q_BLOCK:(i + 1) * Q_BLOCK], preferred_element_type=F32)
    ow = ow[:HEAD_DIM] / jnp.maximum(ow[HEAD_DIM:HEAD_DIM + 1], 1e-30)

    def gate(j):
        return jnp.concatenate([gate_ref[j, g:g + 1, c * Q_BLOCK:(c + 1) * Q_BLOCK]
                                for c in range(WIN_PAIR) for g in range(GROUP)], axis=1)

    def both(ref):
        return jnp.concatenate([ref[c] for c in range(WIN_PAIR)], axis=1)

    o = gate(0) * both(oc_ref) + gate(1) * both(os_ref) + gate(2) * ow
    for c in range(WIN_PAIR):
        for g in range(GROUP):
            col = c * QG + g * Q_BLOCK
            o_ref[c * Q_BLOCK:(c + 1) * Q_BLOCK, g * HEAD_DIM:(g + 1) * HEAD_DIM] = (
                o[:, col:col + Q_BLOCK].T.astype(BF16))


def _win_attention(qt, kw, vwt, tw, gates_t, oc, osel, *, B, S):
    nq = S // Q_BLOCK
    npair = nq // WIN_PAIR
    back = WINDOW // Q_BLOCK
    nk = back + WIN_PAIR

    def kmap(j):
        return lambda h, b, i: (N_KV + h, b * nq + jnp.maximum(WIN_PAIR * i - back + j, 0), 0)

    def vmap_(j):
        return lambda h, b, i: (h, 0, b * nq + jnp.maximum(WIN_PAIR * i - back + j, 0))

    blk = pl.BlockSpec((None, WIN_PAIR, HEAD_DIM, QG), lambda h, b, i: (h, b * npair + i, 0, 0))
    return pl.pallas_call(
        _win_kernel,
        grid=(N_KV, B, npair),
        in_specs=([pl.BlockSpec((GROUP * HEAD_DIM, WIN_PAIR * Q_BLOCK), lambda h, b, i: (h, b * npair + i))]
                  + [pl.BlockSpec((None, Q_BLOCK, HEAD_DIM), kmap(j)) for j in range(nk)]
                  + [pl.BlockSpec((None, V_ROWS, Q_BLOCK), vmap_(j)) for j in range(nk)]
                  + [pl.BlockSpec((None, nk * Q_BLOCK, WIN_PAIR * QG), lambda h, b, i: (h, 0, 0)),
                     pl.BlockSpec((3, None, GROUP, WIN_PAIR * Q_BLOCK),
                                  lambda h, b, i: (0, h, 0, b * npair + i)),
                     blk, blk]),
        out_specs=pl.BlockSpec((WIN_PAIR * Q_BLOCK, GROUP * HEAD_DIM), lambda h, b, i: (b * npair + i, h)),
        out_shape=jax.ShapeDtypeStruct((B * S, N_HEADS * HEAD_DIM), BF16),
        compiler_params=_cparams(("parallel", "parallel", "parallel")),
        name="nsa_window_merge",
    )(qt, *([kw] * nk), *([vwt] * nk), tw, gates_t, oc, osel)


def _nsa_layer(x32, xb, p, tables, consts, ln_g, ln_b, *, B, S):
    T = B * S
    tc, ts, tw = tables
    overlap_t, sel_onehot = consts
    kd = N_KV * HEAD_DIM
    qv = _proj_nt(p["w_qv_t"], xb, out_dtype=BF16, scale=HEAD_DIM ** -0.5 * LOG2E,
                  n_scaled=N_HEADS * HEAD_DIM // 512)
    qt = qv
    k2 = _proj_nn(xb, p["w_k"], out_dtype=BF16, head_major=True)
    cmp = _proj_nn(xb, p["w_cmp"], out_dtype=F32, head_major=True)
    gates_t = _proj_nt(p["w_gate_t"], xb, out_dtype=F32, act="sigmoid")
    gates_t = gates_t.reshape(3, N_KV, GROUP, T)

    cmp = cmp.reshape(2 * N_KV, B, S, HEAD_DIM)
    kc = _compress(cmp, 0, p["pe_k"], p["w1_k"], p["w2_k"], B=B, S=S, transposed=False)
    vct = _compress(cmp, N_KV, p["pe_v"], p["w1_v"], p["w2_v"], B=B, S=S, transposed=True)
    oc, mbt = _cmp_attention(qt, kc, vct, tc, overlap_t, B=B, S=S)
    if mbt.shape[2] % LANE:
        mbt = jnp.pad(mbt, ((0, 0), (0, 0), (0, -mbt.shape[2] % LANE), (0, 0)), constant_values=NEG)

    nc = S // SEL_CHUNK
    k_sel = k2.reshape(2 * N_KV, B, nc, SEL_CHUNK, HEAD_DIM)
    v_sel_t = qv[N_HEADS * HEAD_DIM:N_HEADS * HEAD_DIM + kd].reshape(N_KV, HEAD_DIM, B, nc, SEL_CHUNK)
    v_sel_t = v_sel_t.transpose(0, 2, 3, 1, 4)
    ones_rows = (jnp.arange(V_ROWS - HEAD_DIM) == 0).astype(BF16)[:, None]
    v_sel_t = jnp.concatenate(
        [v_sel_t, jnp.broadcast_to(ones_rows, v_sel_t.shape[:3] + (V_ROWS - HEAD_DIM, SEL_CHUNK))], axis=3)
    osel = _sel_attention(qt, mbt, k_sel, sel_onehot, v_sel_t, ts, B=B, S=S)

    v_win_t = qv[N_HEADS * HEAD_DIM + kd:].reshape(N_KV, HEAD_DIM, T)
    v_win_t = jnp.concatenate(
        [v_win_t, jnp.broadcast_to(ones_rows, (N_KV, V_ROWS - HEAD_DIM, T))], axis=1)
    attn = _win_attention(qt, k2, v_win_t, tw, gates_t, oc, osel, B=B, S=S)
    return _proj_ln(attn, p["w_out"], x32, ln_g, ln_b, *p["router"])


def _nsa_params(w_in, pe_k, w1_k, w2_k, pe_v, w1_v, w2_v, w_out):
    D = w_in.shape[0]
    qd, kd = N_HEADS * HEAD_DIM, N_KV * HEAD_DIM
    w_kv = w_in[:, qd:qd + 6 * kd].reshape(D, 6, kd)
    gcol = np.array([[[(kv * GROUP + g) * 3 + j for g in range(GROUP)] for kv in range(N_KV)]
                     for j in range(3)]).reshape(-1)
    return dict(
        w_qv_t=jnp.concatenate([w_in[:, :qd], w_kv[:, 3], w_kv[:, 5]], axis=1).T.astype(BF16),
        w_k=jnp.concatenate([w_kv[:, 2], w_kv[:, 4]], axis=1).astype(BF16),
        w_cmp=jnp.concatenate([w_kv[:, 0], w_kv[:, 1]], axis=1).astype(BF16),
        w_gate_t=w_in[:, qd + 6 * kd:][:, gcol].T.astype(BF16),
        pe_k=pe_k, w1_k=w1_k.astype(BF16), w2_k=w2_k.astype(BF16),
        pe_v=pe_v, w1_v=w1_v.astype(BF16), w2_v=w2_v.astype(BF16),
        w_out=w_out.astype(BF16))


def _nsa_constants(S):
    n_cmp, n_sel = S // CMP_STRIDE, S // SEL_LEN
    cj = np.arange(n_cmp) * CMP_STRIDE
    sb = np.arange(n_sel) * SEL_LEN
    ov = np.maximum(np.minimum(cj[None, :] + CMP_LEN, sb[:, None] + SEL_LEN)
                    - np.maximum(cj[None, :], sb[:, None]), 0).astype(np.float32) / CMP_LEN
    ov[:, n_cmp - 1] = 0.0
    key = np.arange(SEL_CHUNK)
    chunk = np.arange(min(S // SEL_CHUNK, LANE * SEL_LEN // SEL_CHUNK))
    blk = (chunk[:, None] * (SEL_CHUNK // SEL_LEN) + key[None, :] // SEL_LEN) % LANE
    onehot = (blk[:, :, None] == np.arange(LANE)[None, None, :]).astype(np.float32)
    return jnp.asarray(ov, BF16), jnp.asarray(onehot, BF16)


def kernel(x, rel_bias, router_w, router_b, ln1_g, ln1_b, ln2_g, ln2_b, lru_w_in, lru_conv_w,
           lru_conv_b, lru_w_a, lru_b_a, lru_w_i, lru_b_i, lru_lambda, lru_w_out, nsa_w_in,
           nsa_pe_k, nsa_w1_k, nsa_w2_k, nsa_pe_v, nsa_w1_v, nsa_w2_v, nsa_w_out,
           moe_w_gate, moe_w_up, moe_w_down):
    B, S, D = x.shape
    T = B * S
    assert D == D_MODEL and S % (CMP_TILE * CMP_STRIDE) == 0 and S // SEL_LEN >= SEL_TOP
    qd, kd = N_HEADS * HEAD_DIM, N_KV * HEAD_DIM
    tables = _bias_tables(rel_bias, S)
    consts = _nsa_constants(S)
    router = _router_params(router_w, router_b)
    moe_bf16 = [w.astype(BF16) for w in (moe_w_gate, moe_w_up, moe_w_down)]

    x32 = x.reshape(T, D)
    xb = x32.astype(BF16)
    for layer in range(DEPTH):
        j = layer // 2
        if layer % 2 == 0:
            w_in = lru_w_in[j].astype(BF16)
            p = dict(w_gate=w_in[:, :D], w_y=w_in[:, D:], conv_w=lru_conv_w[j], conv_b=lru_conv_b[j],
                     w_a=lru_w_a[j].astype(BF16), b_a=lru_b_a[j], w_i=lru_w_i[j].astype(BF16),
                     b_i=lru_b_i[j], lam=lru_lambda[j], w_out=lru_w_out[j].astype(BF16), router=router)
            x32, xb, logits = _rglru_layer(x32, xb, p, ln1_g[layer], ln1_b[layer], B=B, S=S)
        else:
            p = _nsa_params(nsa_w_in[j], nsa_pe_k[j], nsa_w1_k[j], nsa_w2_k[j], nsa_pe_v[j],
                            nsa_w1_v[j], nsa_w2_v[j], nsa_w_out[j])
            p["router"] = router
            x32, xb, logits = _nsa_layer(x32, xb, p, tables, consts, ln1_g[layer], ln1_b[layer], B=B, S=S)
        x32, xb = _moe_layer(x32, logits, moe_bf16[0], moe_bf16[1], moe_bf16[2], layer,
                             ln2_g[layer], ln2_b[layer])
    return x32.reshape(B, S, D)
```

```python
import functools
import math

import numpy as np
import jax
import jax.numpy as jnp
from jax import lax
from jax.experimental import pallas as pl
from jax.experimental.pallas import tpu as pltpu

F32 = jnp.float32
BF16 = jnp.bfloat16

D_MODEL = 2048
DEPTH = 4
LRU_BLOCK = 256
LRU_NBLK = D_MODEL // LRU_BLOCK
CONV_W = 4
LRU_C = 8.0
HEAD_DIM = 128
N_HEADS = 16
N_KV = 4
GROUP = 4
CMP_LEN = 32
CMP_STRIDE = 16
SEL_LEN = 64
SEL_TOP = 16
WINDOW = 512
Q_BLOCK = 128
NUM_BUCKETS = 32
MAX_DISTANCE = 1024
N_EXPERTS = 16
N_GROUPS = 4
EXPERTS_PER_GROUP = 4
D_EXPERT = 1024
ALPHA = (2 * DEPTH) ** 0.25
LN_EPS = 1e-5
NEG = -1e30
BIG = 1e9

LANE = 128
QG = GROUP * Q_BLOCK
EXPERT_ROWS = 256
SEL_CHUNK = 256
SEL_NEAR = 5
SEL_FAR = 4
V_ROWS = HEAD_DIM + 16
SEL_PAIR = SEL_CHUNK // Q_BLOCK
WIN_PAIR = 2
CMP_PAIR = 2
LOG2E = math.log2(math.e)
CMP_TILE = 128
CMP_NEAR = 23
VMEM_LIMIT = 56 << 20


def _cparams(sem, vmem=VMEM_LIMIT):
    return pltpu.CompilerParams(dimension_semantics=sem, vmem_limit_bytes=vmem)


def _act(x, act):
    if act == "gelu":
        return jax.nn.gelu(x)
    if act == "sigmoid":
        return jax.nn.sigmoid(x)
    return x


def _proj_nn_kernel(x_ref, w_ref, o_ref, *, act, head_major):
    acc = jnp.dot(x_ref[...], w_ref[...], preferred_element_type=F32)
    acc = _act(acc, act)
    if head_major:
        for h in range(o_ref.shape[0]):
            o_ref[h] = acc[:, h * LANE:(h + 1) * LANE].astype(o_ref.dtype)
    else:
        o_ref[...] = acc.astype(o_ref.dtype)


def _proj_nn(x, w, *, out_dtype, act=None, head_major=False, tm=1024, tn=1024):
    T, K = x.shape
    N = w.shape[1]
    tm, tn = min(tm, T), min(tn, N)
    if head_major:
        out_shape = (N // LANE, T, LANE)
        out_spec = pl.BlockSpec((tn // LANE, tm, LANE), lambda j, i: (j, i, 0))
    else:
        out_shape = (T, N)
        out_spec = pl.BlockSpec((tm, tn), lambda j, i: (i, j))
    return pl.pallas_call(
        functools.partial(_proj_nn_kernel, act=act, head_major=head_major),
        grid=(N // tn, T // tm),
        in_specs=[pl.BlockSpec((tm, K), lambda j, i: (i, 0)),
                  pl.BlockSpec((K, tn), lambda j, i: (0, j))],
        out_specs=out_spec,
        out_shape=jax.ShapeDtypeStruct(out_shape, out_dtype),
        compiler_params=_cparams(("parallel", "parallel")),
        name="proj_nn",
    )(x, w)


def _proj_nt_kernel(wt_ref, x_ref, o_ref, *, act, scale, n_scaled):
    acc = lax.dot_general(wt_ref[...], x_ref[...], (((1,), (1,)), ((), ())),
                          preferred_element_type=F32)
    if n_scaled:
        acc = acc * jnp.where(pl.program_id(0) < n_scaled, scale, 1.0).astype(F32)
    o_ref[...] = _act(acc, act).astype(o_ref.dtype)


def _proj_nt(wt, x, *, out_dtype, act=None, scale=1.0, n_scaled=0, tm=1024, tn=512):
    N, K = wt.shape
    T = x.shape[0]
    tm, tn = min(tm, T), min(tn, N)
    return pl.pallas_call(
        functools.partial(_proj_nt_kernel, act=act, scale=scale, n_scaled=n_scaled),
        grid=(N // tn, T // tm),
        in_specs=[pl.BlockSpec((tn, K), lambda j, i: (j, 0)),
                  pl.BlockSpec((tm, K), lambda j, i: (i, 0))],
        out_specs=pl.BlockSpec((tn, tm), lambda j, i: (j, i)),
        out_shape=jax.ShapeDtypeStruct((N, T), out_dtype),
        compiler_params=_cparams(("parallel", "parallel")),
        name="proj_nt",
    )(wt, x)


def _layer_norm_rows(z, g, b):
    mu = jnp.mean(z, axis=-1, keepdims=True)
    zc = z - mu
    var = jnp.mean(zc * zc, axis=-1, keepdims=True)
    return zc * lax.rsqrt(var + LN_EPS) * g + b


def _proj_ln_kernel(a_ref, w_ref, res_ref, g_ref, b_ref, rw_ref, rb_ref, of_ref, ob_ref, lg_ref):
    acc = jnp.dot(a_ref[...], w_ref[...], preferred_element_type=F32)
    y = _layer_norm_rows(ALPHA * res_ref[...] + acc, g_ref[...], b_ref[...])
    of_ref[...] = y
    y_hi = y.astype(BF16)
    ob_ref[...] = y_hi
    y_lo = (y - y_hi.astype(F32)).astype(BF16)
    rows = y.shape[0]
    both = jnp.dot(jnp.concatenate([y_hi, y_lo], axis=0), rw_ref[0], preferred_element_type=F32)
    lg_ref[...] = (both[:rows] + both[rows:] + jnp.dot(y_hi, rw_ref[1], preferred_element_type=F32)
                   + rb_ref[...])


def _proj_ln(a, w, res, g, b, router_w, router_b, *, tm=256):
    T, K = a.shape
    D = w.shape[1]
    tm = min(tm, T)
    row = pl.BlockSpec((tm, D), lambda i: (i, 0))
    vec = pl.BlockSpec((1, D), lambda i: (0, 0))
    lgt = pl.BlockSpec((tm, LANE), lambda i: (i, 0))
    return pl.pallas_call(
        _proj_ln_kernel,
        grid=(T // tm,),
        in_specs=[pl.BlockSpec((tm, K), lambda i: (i, 0)),
                  pl.BlockSpec((K, D), lambda i: (0, 0)),
                  row, vec, vec,
                  pl.BlockSpec((2, D, LANE), lambda i: (0, 0, 0)),
                  pl.BlockSpec((1, LANE), lambda i: (0, 0))],
        out_specs=[row, row, lgt],
        out_shape=[jax.ShapeDtypeStruct((T, D), F32), jax.ShapeDtypeStruct((T, D), BF16),
                   jax.ShapeDtypeStruct((T, LANE), F32)],
        compiler_params=_cparams(("parallel",)),
        name="proj_ln",
    )(a, w, res, g.reshape(1, D), b.reshape(1, D), router_w, router_b)


def _lru_kernel(y_ref, gate_ref, cw_ref, cb_ref, wa_ref, ba_ref, wi_ref, bi_ref, lam_ref,
                o_ref, h_ref, tail_ref, *, rows):
    @pl.when(pl.program_id(2) == 0)
    def _():
        h_ref[...] = jnp.zeros_like(h_ref)
        tail_ref[...] = jnp.zeros_like(tail_ref)

    y = y_ref[...]
    tail = tail_ref[...]
    cw = cw_ref[...]
    row8 = lax.broadcasted_iota(jnp.int32, (8, LRU_BLOCK), 0)
    yc = cb_ref[...] + y * cw[CONV_W - 1:CONV_W, :]
    for d in range(1, CONV_W):
        ys = pltpu.roll(y, d, 0)
        first = jnp.where(row8 < d, pltpu.roll(tail, d, 0), ys[:8])
        ys = jnp.concatenate([first, ys[8:]], axis=0)
        yc = yc + ys * cw[CONV_W - 1 - d:CONV_W - d, :]
    tail_ref[...] = y[rows - 8:]

    yb = yc.astype(BF16)
    r = jax.nn.sigmoid(jnp.dot(yb, wa_ref[...], preferred_element_type=F32) + ba_ref[...])
    ig = jax.nn.sigmoid(jnp.dot(yb, wi_ref[...], preferred_element_type=F32) + bi_ref[...])
    z = -lam_ref[...]
    softplus = jnp.maximum(z, 0.0) + jnp.log1p(jnp.exp(-jnp.abs(z)))
    log_a = (-LRU_C * softplus) * r
    a = jnp.exp(log_a)
    h = jnp.sqrt(1.0 - a * a) * (ig * yc)

    rowi = lax.broadcasted_iota(jnp.int32, (rows, LRU_BLOCK), 0)
    s = 1
    while s < rows:
        if s < 8:
            keep = rowi >= s
            h = jnp.where(keep, h + a * pltpu.roll(h, s, 0), h)
            a = jnp.where(keep, a * pltpu.roll(a, s, 0), a)
        else:
            h = jnp.concatenate([h[:s], h[s:] + a[s:] * h[:rows - s]], axis=0)
            a = jnp.concatenate([a[:s], a[s:] * a[:rows - s]], axis=0)
        s *= 2
    h = h + a * h_ref[0:1, :]
    h_ref[...] = jnp.broadcast_to(h[rows - 1:rows, :], h_ref.shape)
    o_ref[...] = (h * gate_ref[...].astype(F32)).astype(BF16)


def _lru_core(y, gate, conv_w, conv_b, w_a, b_a, w_i, b_i, lam, *, B, S, rows=512):
    T, W = y.shape
    rows = min(rows, S)
    ns = S // rows
    blk = pl.BlockSpec((rows, LRU_BLOCK), lambda n, b, s: (b * ns + s, n))
    vec = pl.BlockSpec((1, LRU_BLOCK), lambda n, b, s: (0, n))
    wblk = pl.BlockSpec((None, LRU_BLOCK, LRU_BLOCK), lambda n, b, s: (n, 0, 0))
    return pl.pallas_call(
        functools.partial(_lru_kernel, rows=rows),
        grid=(LRU_NBLK, B, ns),
        in_specs=[blk, blk,
                  pl.BlockSpec((CONV_W, LRU_BLOCK), lambda n, b, s: (0, n)), vec,
                  wblk, vec, wblk, vec, vec],
        out_specs=blk,
        out_shape=jax.ShapeDtypeStruct((T, W), BF16),
        scratch_shapes=[pltpu.VMEM((8, LRU_BLOCK), F32), pltpu.VMEM((8, LRU_BLOCK), F32)],
        compiler_params=_cparams(("parallel", "parallel", "arbitrary")),
        name="lru_core",
    )(y, gate, conv_w, conv_b.reshape(1, W), w_a, b_a.reshape(1, W), w_i, b_i.reshape(1, W),
      lam.reshape(1, W))


def _rglru_layer(x32, xb, p, ln_g, ln_b, *, B, S):
    gate = _proj_nn(xb, p["w_gate"], out_dtype=BF16, act="gelu")
    y = _proj_nn(xb, p["w_y"], out_dtype=F32)
    hg = _lru_core(y, gate, p["conv_w"], p["conv_b"], p["w_a"], p["b_a"], p["w_i"], p["b_i"],
                   p["lam"], B=B, S=S)
    return _proj_ln(hg, p["w_out"], x32, ln_g, ln_b, *p["router"])


def _router_kernel(lg_ref, ids_ref, wts_ref, pos_ref, cnt_ref, carry_ref, *, tm):
    @pl.when(pl.program_id(0) == 0)
    def _():
        carry_ref[...] = jnp.zeros_like(carry_ref)

    logits = lg_ref[...].T[:N_EXPERTS]
    m = jnp.max(logits, axis=0, keepdims=True)
    e = jnp.exp(logits - m)
    p = e / jnp.sum(e, axis=0, keepdims=True)
    rows = [p[k:k + 1, :] for k in range(N_EXPERTS)]

    scores = []
    for g in range(N_GROUPS):
        a, b, c, d = rows[4 * g:4 * g + 4]
        hi1, lo1, hi2, lo2 = jnp.maximum(a, b), jnp.minimum(a, b), jnp.maximum(c, d), jnp.minimum(c, d)
        scores.append(jnp.maximum(hi1, hi2) + jnp.maximum(jnp.minimum(hi1, hi2), jnp.maximum(lo1, lo2)))
    best, gsel = scores[0], jnp.zeros((1, tm), jnp.int32)
    for g in range(1, N_GROUPS):
        upd = scores[g] > best
        gsel = jnp.where(upd, g, gsel)
        best = jnp.where(upd, scores[g], best)

    cand = []
    for k in range(EXPERTS_PER_GROUP):
        c = rows[k]
        for g in range(1, N_GROUPS):
            c = jnp.where(gsel == g, rows[4 * g + k], c)
        cand.append(c)
    v1, i1 = cand[0], jnp.zeros((1, tm), jnp.int32)
    for k in range(1, EXPERTS_PER_GROUP):
        upd = cand[k] > v1
        i1 = jnp.where(upd, k, i1)
        v1 = jnp.where(upd, cand[k], v1)
    v2, i2 = jnp.full((1, tm), -1.0, F32), jnp.zeros((1, tm), jnp.int32)
    for k in range(EXPERTS_PER_GROUP):
        c = jnp.where(i1 == k, -1.0, cand[k])
        upd = c > v2
        i2 = jnp.where(upd, k, i2)
        v2 = jnp.where(upd, c, v2)
    e0 = gsel * EXPERTS_PER_GROUP + i1
    e1 = gsel * EXPERTS_PER_GROUP + i2
    tot = v1 + v2

    eio = lax.broadcasted_iota(jnp.int32, (N_EXPERTS, tm), 0)
    hit0, hit1 = eio == e0, eio == e1
    onehot = jnp.where(hit0 | hit1, 1.0, 0.0)
    before = (lax.broadcasted_iota(jnp.int32, (tm, tm), 0)
              < lax.broadcasted_iota(jnp.int32, (tm, tm), 1))
    rank = jnp.dot(onehot.astype(BF16), jnp.where(before, 1.0, 0.0).astype(BF16),
                   preferred_element_type=F32) + carry_ref[:, 0:1]
    pos0 = jnp.sum(jnp.where(hit0, rank, 0.0), axis=0, keepdims=True)
    pos1 = jnp.sum(jnp.where(hit1, rank, 0.0), axis=0, keepdims=True)
    carry_ref[...] = carry_ref[...] + jnp.sum(onehot, axis=1, keepdims=True)
    cnt_ref[...] = carry_ref[...]

    zi = jnp.zeros((6, tm), jnp.int32)
    ids_ref[...] = jnp.concatenate([e0, e1, zi], axis=0)
    pos_ref[...] = jnp.concatenate([pos0.astype(jnp.int32), pos1.astype(jnp.int32), zi], axis=0)
    wts_ref[...] = jnp.concatenate([v1 / tot, v2 / tot, jnp.zeros((6, tm), F32)], axis=0)


def _router(logits, *, tm=512):
    T = logits.shape[0]
    tm = min(tm, T)
    tok = pl.BlockSpec((8, tm), lambda i: (0, i))
    return pl.pallas_call(
        functools.partial(_router_kernel, tm=tm),
        grid=(T // tm,),
        in_specs=[pl.BlockSpec((tm, LANE), lambda i: (i, 0))],
        out_specs=[tok, tok, tok, pl.BlockSpec((N_EXPERTS, LANE), lambda i: (0, 0))],
        out_shape=[jax.ShapeDtypeStruct((8, T), jnp.int32), jax.ShapeDtypeStruct((8, T), F32),
                   jax.ShapeDtypeStruct((8, T), jnp.int32),
                   jax.ShapeDtypeStruct((N_EXPERTS, LANE), F32)],
        scratch_shapes=[pltpu.VMEM((N_EXPERTS, LANE), F32)],
        compiler_params=_cparams(("arbitrary",)),
        name="moe_router",
    )(logits)


def _router_params(router_w, router_b):
    pad = LANE - N_EXPERTS
    w = jnp.pad(router_w, ((0, 0), (0, pad)))
    w_hi = w.astype(BF16)
    w_lo = (w - w_hi.astype(F32)).astype(BF16)
    return (jnp.stack([w_hi, w_lo]), jnp.pad(router_b, (0, pad)).reshape(1, LANE))


def _row_copy(src, src_row, dst, dst_row, sem, n=1):
    return pltpu.make_async_copy(src.at[pl.ds(src_row, n)], dst.at[pl.ds(dst_row, n)], sem)


def _row_tokens_kernel(n_ref, lo_ref, hi_ref, d0_ref, d1_ref, tok_ref):
    step = 8

    def clear(r, c):
        tok_ref[r] = 0
        return c

    def put(i, c):
        for k in range(step):
            t = i * step + k
            tok_ref[d0_ref[t]] = t
            tok_ref[d1_ref[t]] = t
        return c

    for e in range(lo_ref.shape[0]):
        lax.fori_loop(lo_ref[e], hi_ref[e], clear, 0)
    lax.fori_loop(0, n_ref[0] // step, put, 0)


def _row_tokens(d0, d1, free_lo, free_hi, n_rows):
    n_tok = d0.shape[0]
    assert n_tok % 8 == 0
    return pl.pallas_call(
        _row_tokens_kernel,
        grid_spec=pltpu.PrefetchScalarGridSpec(
            num_scalar_prefetch=5, grid=(1,), in_specs=[],
            out_specs=pl.BlockSpec(memory_space=pltpu.SMEM)),
        out_shape=jax.ShapeDtypeStruct((n_rows,), jnp.int32),
        compiler_params=pltpu.CompilerParams(dimension_semantics=("arbitrary",)),
        name="moe_row_tokens",
    )(jnp.array([n_tok], jnp.int32), free_lo, free_hi, d0, d1)


def _ffn_kernel(be_ref, nact_ref, tok_ref, x_hbm, wg_ref, wu_ref, wd_ref, y_ref, xbuf, sems):
    i = pl.program_id(0)
    n_act = nact_ref[0]
    R = EXPERT_ROWS

    def gather(block, slot):
        def issue(r, c):
            _row_copy(x_hbm, tok_ref[block * R + r], xbuf.at[slot], r, sems.at[slot]).start()
            return c
        lax.fori_loop(0, R, issue, 0, unroll=8)

    def run(slot):
        @pl.when(i + 1 < n_act)
        def _():
            gather(i + 1, 1 - slot)

        def drain(r, c):
            _row_copy(x_hbm, 0, xbuf.at[slot], 0, sems.at[slot]).wait()
            return c
        lax.fori_loop(0, R, drain, 0, unroll=8)
        xb = xbuf[slot].astype(BF16)
        g = jnp.dot(xb, wg_ref[...], preferred_element_type=F32)
        u = jnp.dot(xb, wu_ref[...], preferred_element_type=F32)
        h = (jax.nn.silu(g) * u).astype(BF16)
        y_ref[...] = jnp.dot(h, wd_ref[...], preferred_element_type=F32)

    @pl.when(i == 0)
    def _():
        gather(0, 0)

    for slot in range(2):
        @pl.when((i < n_act) & (i % 2 == slot))
        def _():
            run(slot)

    @pl.when(i >= n_act)
    def _():
        y_ref[...] = jnp.zeros_like(y_ref)


def _expert_ffn(x32, row_tok, blk_e, n_act, w_gate, w_up, w_down, layer):
    D = x32.shape[1]
    R = EXPERT_ROWS
    n_rows = row_tok.shape[0]
    wmap = lambda i, be, na, tok: (layer, be[i], 0, 0)
    return pl.pallas_call(
        _ffn_kernel,
        grid_spec=pltpu.PrefetchScalarGridSpec(
            num_scalar_prefetch=3, grid=(n_rows // R,),
            in_specs=[pl.BlockSpec(memory_space=pl.ANY),
                      pl.BlockSpec((None, None, D, D_EXPERT), wmap),
                      pl.BlockSpec((None, None, D, D_EXPERT), wmap),
                      pl.BlockSpec((None, None, D_EXPERT, D), wmap)],
            out_specs=pl.BlockSpec((R, D), lambda i, be, na, tok: (i, 0)),
            scratch_shapes=[pltpu.VMEM((2, R, D), F32), pltpu.SemaphoreType.DMA((2,))]),
        out_shape=jax.ShapeDtypeStruct((n_rows, D), F32),
        compiler_params=_cparams(("arbitrary",)),
        name="moe_ffn",
    )(blk_e, n_act, row_tok, x32, w_gate, w_up, w_down)


def _combine_kernel(d0_ref, d1_ref, res_ref, w_ref, g_ref, b_ref, y_hbm, of_ref, ob_ref,
                    buf, sem, *, tm):
    i = pl.program_id(0)

    def issue(r, c):
        t = i * tm + r
        _row_copy(y_hbm, d0_ref[t], buf.at[0], r, sem).start()
        _row_copy(y_hbm, d1_ref[t], buf.at[1], r, sem).start()
        return c

    def drain(r, c):
        _row_copy(y_hbm, 0, buf.at[0], 0, sem).wait()
        _row_copy(y_hbm, 0, buf.at[1], 0, sem).wait()
        return c

    lax.fori_loop(0, tm, issue, 0)
    lax.fori_loop(0, tm, drain, 0)
    w = w_ref[...]
    z = ALPHA * res_ref[...] + (w[:, 0:1] * buf[0] + w[:, 1:2] * buf[1])
    y = _layer_norm_rows(z, g_ref[...], b_ref[...])
    of_ref[...] = y
    ob_ref[...] = y.astype(BF16)


def _combine_ln(y, d0, d1, wts, res, g, b, *, tm=256):
    T, D = res.shape
    tm = min(tm, T)
    row = pl.BlockSpec((tm, D), lambda i, a, c: (i, 0))
    vec = pl.BlockSpec((1, D), lambda i, a, c: (0, 0))
    return pl.pallas_call(
        functools.partial(_combine_kernel, tm=tm),
        grid_spec=pltpu.PrefetchScalarGridSpec(
            num_scalar_prefetch=2, grid=(T // tm,),
            in_specs=[row, pl.BlockSpec((tm, 2), lambda i, a, c: (i, 0)), vec, vec,
                      pl.BlockSpec(memory_space=pl.ANY)],
            out_specs=[row, row],
            scratch_shapes=[pltpu.VMEM((2, tm, D), F32), pltpu.SemaphoreType.DMA(())]),
        out_shape=[jax.ShapeDtypeStruct((T, D), F32), jax.ShapeDtypeStruct((T, D), BF16)],
        compiler_params=_cparams(("arbitrary",)),
        name="moe_combine_ln",
    )(d0, d1, res, wts, g.reshape(1, D), b.reshape(1, D), y)


def _moe_layer(x32, logits, w_gate, w_up, w_down, layer, ln_g, ln_b):
    T = x32.shape[0]
    R = EXPERT_ROWS
    ids, wts, pos, cnt = _router(logits)
    counts = cnt[:, 0].astype(jnp.int32)
    padded = (counts + R - 1) // R * R
    p_end = jnp.cumsum(padded)
    offs = p_end - padded
    d0 = offs[ids[0]] + pos[0]
    d1 = offs[ids[1]] + pos[1]
    n_blocks = 2 * T // R + N_EXPERTS
    n_act = p_end[-1] // R
    step = jnp.arange(n_blocks, dtype=jnp.int32)
    live = step < n_act
    src = jnp.where(live, step, n_act - 1)
    blk_e = jnp.minimum(jnp.sum(src[:, None] >= (p_end // R)[None, :], axis=1),
                        N_EXPERTS - 1).astype(jnp.int32)
    n_rows = n_blocks * R
    free_lo = jnp.concatenate([offs + counts, p_end[-1:]]).astype(jnp.int32)
    free_hi = jnp.concatenate([p_end, jnp.array([n_rows], p_end.dtype)]).astype(jnp.int32)
    row_tok = _row_tokens(d0, d1, free_lo, free_hi, n_rows)
    y = _expert_ffn(x32, row_tok, blk_e, n_act.reshape(1).astype(jnp.int32), w_gate, w_up, w_down, layer)
    return _combine_ln(y, d0, d1, wts[:2].T, x32, ln_g, ln_b)


def _t5_bucket(dist):
    n = jnp.maximum(dist, 0)
    max_exact = NUM_BUCKETS // 2
    nf = jnp.maximum(n, 1).astype(F32)
    large = max_exact + (jnp.log(nf / max_exact) / math.log(MAX_DISTANCE / max_exact)
                         * (NUM_BUCKETS - max_exact)).astype(jnp.int32)
    large = jnp.minimum(large, NUM_BUCKETS - 1)
    return jnp.where(n < max_exact, n, large)


def _stack_heads(tbl):
    parts = tbl.reshape((N_KV, GROUP) + tbl.shape[1:])
    return jnp.concatenate([parts[:, g] for g in range(GROUP)], axis=-1)


def _bias_tables(rel_bias, S):
    off = CMP_STRIDE * CMP_TILE + Q_BLOCK
    n_dist = CMP_NEAR * Q_BLOCK
    bvec = rel_bias[_t5_bucket(jnp.arange(n_dist))].T
    last = rel_bias[NUM_BUCKETS - 1]
    masked = jnp.full((N_HEADS, off), NEG, F32)
    w_abs = jnp.concatenate([masked, bvec * LOG2E], axis=1)
    w_rel = jnp.concatenate([masked, (bvec - last[:, None]) * LOG2E], axis=1)
    w_win = jnp.where(jnp.arange(-off, n_dist) < WINDOW, w_abs, NEG)

    def toeplitz(w, first, step, n):
        lo = first - step * (n - 1) + off
        span = step * (n - 1) + Q_BLOCK
        u = w[:, lo:lo + span]
        width = span + step
        tiled = jnp.tile(u, (1, -(-n * width // span)))[:, :n * width]
        return tiled.reshape(N_HEADS, n, width)[:, ::-1, :Q_BLOCK]

    def hankel(u, n, m, step):
        span = step * (n - 1) + m
        width = span + step
        u = u[..., :span, :]
        tiled = jnp.tile(u, (1,) * (u.ndim - 2) + (-(-n * width // span), 1))[..., :n * width, :]
        return tiled.reshape(u.shape[:-2] + (n, width, u.shape[-1]))[..., :m, :]

    per = Q_BLOCK // CMP_STRIDE
    n_seq = per * (CMP_NEAR - 1) + (per - 1) + CMP_TILE
    lo = off - (CMP_LEN - 1) - CMP_STRIDE * (CMP_TILE - 1)
    seq = w_abs[:, lo:lo + CMP_STRIDE * n_seq].reshape(N_HEADS, n_seq, CMP_STRIDE)
    by_tile = hankel(seq, CMP_NEAR, per - 1 + CMP_TILE, per)
    near_c = hankel(by_tile, per, CMP_TILE, 1)
    near_c = near_c[:, :, :, ::-1, :].transpose(0, 1, 3, 2, 4).reshape(N_HEADS, CMP_NEAR, CMP_TILE, Q_BLOCK)
    const = lambda v: jnp.broadcast_to(v, (N_HEADS, 1, CMP_TILE, Q_BLOCK)).astype(F32)
    tc = jnp.concatenate([const(NEG), near_c, const(last[:, None, None, None] * LOG2E)], axis=1)
    zero = jnp.zeros((N_HEADS, 1, SEL_CHUNK, Q_BLOCK), F32)
    ts = jnp.stack([
        jnp.concatenate(
            [zero] + [toeplitz(w_rel, Q_BLOCK * sub + SEL_CHUNK * (SEL_NEAR - 1 - i), 1, SEL_CHUNK)[:, None]
                      for i in range(SEL_NEAR)] + [zero + NEG], axis=1)
        for sub in range(SEL_CHUNK // Q_BLOCK)], axis=1)
    n_keys = WINDOW + WIN_PAIR * Q_BLOCK
    tw = jnp.concatenate([_stack_heads(toeplitz(w_win, WINDOW + Q_BLOCK * c, 1, n_keys))
                          for c in range(WIN_PAIR)], axis=-1)
    ts = _stack_heads(ts)
    ts = jnp.concatenate([ts[:, sub] for sub in range(SEL_PAIR)], axis=-1)
    return _stack_heads(tc), ts, tw


def _compress_kernel(x_ref, pe_ref, w1_ref, w2_ref, o_ref, *, transposed):
    n = x_ref.shape[0] // CMP_STRIDE
    a = b = None
    for tok in range(CMP_STRIDE):
        xt = x_ref[pl.ds(tok, n, stride=CMP_STRIDE), :]
        as_first = (xt + pe_ref[tok:tok + 1, :]).astype(BF16)
        as_second = (xt + pe_ref[CMP_STRIDE + tok:CMP_STRIDE + tok + 1, :]).astype(BF16)
        da = jnp.dot(as_first, w1_ref[tok * HEAD_DIM:(tok + 1) * HEAD_DIM], preferred_element_type=F32)
        db = jnp.dot(as_second, w1_ref[(CMP_STRIDE + tok) * HEAD_DIM:(CMP_STRIDE + tok + 1) * HEAD_DIM],
                     preferred_element_type=F32)
        a = da if a is None else a + da
        b = db if b is None else b + db
    pre = a + pltpu.roll(b, n - 1, 0)
    rowi = lax.broadcasted_iota(jnp.int32, pre.shape, 0)
    hid = jnp.where(rowi < n - 1, jax.nn.gelu(pre), 0.0).astype(BF16)
    if transposed:
        out = lax.dot_general(w2_ref[...], hid, (((1,), (1,)), ((), ())), preferred_element_type=F32)
    else:
        out = jnp.dot(hid, w2_ref[...], preferred_element_type=F32)
    o_ref[...] = out.astype(BF16)


def _compress(slabs, first, pe, w1, w2, *, B, S, transposed):
    n = S // CMP_STRIDE
    wide = CMP_STRIDE * HEAD_DIM
    out_shape = (N_KV, B, HEAD_DIM, n) if transposed else (N_KV, B, n, HEAD_DIM)
    w2_in = w2.T if transposed else w2
    return pl.pallas_call(
        functools.partial(_compress_kernel, transposed=transposed),
        grid=(N_KV, B),
        in_specs=[pl.BlockSpec((None, None, S, HEAD_DIM), lambda h, b: (first + h, b, 0, 0)),
                  pl.BlockSpec((CMP_LEN, HEAD_DIM), lambda h, b: (0, 0)),
                  pl.BlockSpec((2 * wide, HEAD_DIM), lambda h, b: (0, 0)),
                  pl.BlockSpec((HEAD_DIM, HEAD_DIM), lambda h, b: (0, 0))],
        out_specs=pl.BlockSpec((None, None) + out_shape[2:], lambda h, b: (h, b, 0, 0)),
        out_shape=jax.ShapeDtypeStruct(out_shape, BF16),
        compiler_params=_cparams(("parallel", "parallel")),
        name="nsa_compress",
    )(slabs, pe, w1, w2_in)


def _query_cols(q_ref):
    return jnp.concatenate([q_ref[g * HEAD_DIM:(g + 1) * HEAD_DIM, :] for g in range(GROUP)], axis=1)


def _cmp_body(ib, n_tiles, q_ref, kc_ref, vct_ref, tc_ref, ov_ref, oc_ref, mb_ref):
    n_cmp = n_tiles * CMP_TILE
    n_sel = n_cmp * CMP_STRIDE // SEL_LEN
    nq = CMP_PAIR * Q_BLOCK
    qt = jnp.concatenate([q_ref[g * HEAD_DIM:(g + 1) * HEAD_DIM, c * Q_BLOCK:(c + 1) * Q_BLOCK]
                          for c in range(CMP_PAIR) for g in range(GROUP)], axis=1)
    s = jnp.dot(kc_ref[:n_cmp], qt, preferred_element_type=F32)
    tiles = []
    for cb in range(n_tiles):
        bias = [tc_ref[jnp.clip(ib + c - (CMP_TILE * CMP_STRIDE // Q_BLOCK) * cb + 1, 0, CMP_NEAR + 1)]
                for c in range(CMP_PAIR)]
        tiles.append(s[cb * CMP_TILE:(cb + 1) * CMP_TILE] + jnp.concatenate(bias, axis=1))
    s = jnp.concatenate(tiles, axis=0)
    m = jnp.max(s, axis=0, keepdims=True)
    p = jnp.exp2(s - m)
    inv = jnp.where(m > 0.5 * NEG, 1.0 / jnp.maximum(jnp.sum(p, axis=0, keepdims=True), 1e-30), 0.0)
    p = p * inv
    oc = jnp.dot(vct_ref[:, :n_cmp], p.astype(BF16), preferred_element_type=F32)
    for c in range(CMP_PAIR):
        oc_ref[c] = oc[:, c * QG:(c + 1) * QG]
    ov = ov_ref[:n_sel, :n_cmp]

    sums = []
    for c in range(CMP_PAIR):
        acc = p[:, c * QG:c * QG + Q_BLOCK]
        for g in range(1, GROUP):
            acc = acc + p[:, c * QG + g * Q_BLOCK:c * QG + (g + 1) * Q_BLOCK]
        sums.append(acc)
    psum = jnp.concatenate(sums, axis=1)
    p_hi = psum.astype(BF16)
    p_lo = (psum - p_hi.astype(F32)).astype(BF16)
    imp = (jnp.dot(ov, p_hi, preferred_element_type=F32)
           + jnp.dot(ov, p_lo, preferred_element_type=F32))
    blk = lax.broadcasted_iota(jnp.int32, (n_sel, nq), 0)
    t = ib * Q_BLOCK + lax.broadcasted_iota(jnp.int32, (n_sel, nq), 1)
    cur = jnp.right_shift(t, int(math.log2(SEL_LEN)))
    ok = blk * SEL_LEN <= t
    forced = ok & ((blk == 0) | (blk == cur) | (blk == cur - 1))
    score = jnp.where(forced, -jnp.inf, jnp.where(ok, imp, NEG))
    taken = forced
    for _ in range(SEL_TOP - 3):
        top = jnp.max(score, axis=0, keepdims=True)
        first = jnp.min(jnp.where(score == top, blk, n_sel), axis=0, keepdims=True)
        pick = blk == first
        taken = taken | pick
        score = jnp.where(pick, -jnp.inf, score)
    mb_ref[:n_sel] = jnp.where(taken & ok, 0.0, NEG).astype(BF16)
    if n_sel < mb_ref.shape[0]:
        mb_ref[n_sel:] = jnp.full((mb_ref.shape[0] - n_sel, nq), NEG, BF16)


def _cmp_kernel(*refs, n_tiles):
    ib = CMP_PAIR * pl.program_id(2)
    per_tile = CMP_TILE * CMP_STRIDE // Q_BLOCK
    lo = 0
    for nt in range(1, n_tiles + 1):
        hi = nt * per_tile

        @pl.when((ib >= lo) & (ib < hi))
        def _():
            _cmp_body(ib, nt, *refs)
        lo = hi


def _cmp_attention(qt, kc, vct, tc, overlap_t, *, B, S):
    nq = S // Q_BLOCK
    npair = nq // CMP_PAIR
    n_cmp = S // CMP_STRIDE
    n_sel = S // SEL_LEN
    n_tiles = n_cmp // CMP_TILE
    return pl.pallas_call(
        functools.partial(_cmp_kernel, n_tiles=n_tiles),
        grid=(N_KV, B, npair),
        in_specs=[pl.BlockSpec((GROUP * HEAD_DIM, CMP_PAIR * Q_BLOCK), lambda h, b, i: (h, b * npair + i)),
                  pl.BlockSpec((None, None, n_cmp, HEAD_DIM), lambda h, b, i: (h, b, 0, 0)),
                  pl.BlockSpec((None, None, HEAD_DIM, n_cmp), lambda h, b, i: (h, b, 0, 0)),
                  pl.BlockSpec((None, CMP_NEAR + 2, CMP_TILE, QG), lambda h, b, i: (h, 0, 0, 0)),
                  pl.BlockSpec((n_sel, n_cmp), lambda h, b, i: (0, 0))],
        out_specs=[pl.BlockSpec((None, CMP_PAIR, HEAD_DIM, QG), lambda h, b, i: (h, b * npair + i, 0, 0)),
                   pl.BlockSpec((None, None, n_sel, CMP_PAIR * Q_BLOCK), lambda h, b, i: (h, b, 0, i))],
        out_shape=[jax.ShapeDtypeStruct((N_KV, B * nq, HEAD_DIM, QG), F32),
                   jax.ShapeDtypeStruct((N_KV, B, n_sel, S), BF16)],
        compiler_params=_cparams(("parallel", "parallel", "parallel")),
        name="nsa_cmp_select",
    )(qt, kc, vct, tc, overlap_t)


def _sel_kernel(q_ref, mb_ref, k_ref, oh_ref, vt_ref, ts_ref, o_ref, m_ref, acc_ref, sa_ref, sb_ref):
    cq = pl.program_id(2)
    qt = jnp.concatenate([q_ref[g * HEAD_DIM:(g + 1) * HEAD_DIM, c * Q_BLOCK:(c + 1) * Q_BLOCK]
                          for c in range(SEL_PAIR) for g in range(GROUP)], axis=1)
    blocks_per_chunk = SEL_CHUNK // SEL_LEN
    chunks_per_group = LANE // blocks_per_chunk

    m_ref[...] = jnp.full(m_ref.shape, NEG, F32)
    acc_ref[...] = jnp.zeros_like(acc_ref)

    qq = cq // SEL_FAR
    rows = SEL_FAR * SEL_CHUNK

    def raw_scores(quad):
        c0 = pl.multiple_of(quad * SEL_FAR, SEL_FAR)
        grp = c0 // chunks_per_group
        mb = mb_ref[pl.ds(pl.multiple_of(grp * LANE, LANE), LANE), :]
        mask = jnp.concatenate([mb[:, c * Q_BLOCK:(c + 1) * Q_BLOCK]
                                for c in range(SEL_PAIR) for _ in range(GROUP)], axis=1)
        qa = jnp.concatenate([qt, mask], axis=0)
        o0 = pl.multiple_of(c0 % oh_ref.shape[0], SEL_FAR)
        keys = jnp.concatenate([k_ref[pl.ds(c0, SEL_FAR)].reshape(rows, HEAD_DIM),
                                oh_ref[pl.ds(o0, SEL_FAR)].reshape(rows, HEAD_DIM)], axis=1)
        return jnp.dot(keys, qa, preferred_element_type=F32)

    def update(s, quad):
        c0 = quad * SEL_FAR
        v = jnp.concatenate([vt_ref[c0 + k] for k in range(SEL_FAR)], axis=1)
        m_old = m_ref[0:1, :]
        m_new = jnp.maximum(m_old, jnp.max(s, axis=0, keepdims=True))
        p = jnp.exp2((s - m_new).astype(BF16))
        acc_ref[...] = (jnp.exp2(m_old - m_new) * acc_ref[...]
                        + jnp.dot(v, p, preferred_element_type=F32))
        m_ref[...] = jnp.broadcast_to(m_new, m_ref.shape)

    def bias(quad, live):
        tiles = []
        for k in range(SEL_FAR):
            near = jnp.clip(quad * SEL_FAR + k - cq + SEL_NEAR, 0, SEL_NEAR + 1)
            tiles.append(ts_ref[jnp.where(live, near, SEL_NEAR + 1)])
        return jnp.concatenate(tiles, axis=0)

    def finish(near0_ref):
        s1 = raw_scores(qq)
        update(near0_ref[...] + bias(qq - 1, qq >= 1), jnp.maximum(qq - 1, 0))
        update(s1 + bias(qq, True), qq)
        out = acc_ref[:HEAD_DIM] / jnp.maximum(acc_ref[HEAD_DIM:HEAD_DIM + 1], 1e-30)
        for c in range(SEL_PAIR):
            o_ref[c] = out[:, c * QG:(c + 1) * QG]

    n_far = jnp.maximum(qq - 1, 0)
    sa_ref[...] = raw_scores(0)

    def far_pair(i, carry):
        sb_ref[...] = raw_scores(2 * i + 1)
        update(sa_ref[...], 2 * i)
        sa_ref[...] = raw_scores(2 * i + 2)
        update(sb_ref[...], 2 * i + 1)
        return carry

    lax.fori_loop(0, n_far // 2, far_pair, 0)

    @pl.when(n_far % 2 == 1)
    def _():
        sb_ref[...] = raw_scores(n_far)
        update(sa_ref[...], n_far - 1)
        finish(sb_ref)

    @pl.when(n_far % 2 == 0)
    def _():
        finish(sa_ref)


def _sel_attention(qt, mbt, k_sel, onehot, vt, ts, *, B, S):
    nq = S // Q_BLOCK
    n_sel = mbt.shape[2]
    nc = S // SEL_CHUNK
    width = SEL_PAIR * QG
    once = pl.Buffered(1)
    return pl.pallas_call(
        _sel_kernel,
        grid=(N_KV, B, nc),
        in_specs=[pl.BlockSpec((GROUP * HEAD_DIM, SEL_CHUNK), lambda h, b, i: (h, b * nc + i)),
                  pl.BlockSpec((None, None, n_sel, SEL_CHUNK), lambda h, b, i: (h, b, 0, i)),
                  pl.BlockSpec((None, None, nc, SEL_CHUNK, HEAD_DIM), lambda h, b, i: (h, b, 0, 0, 0)),
                  pl.BlockSpec(onehot.shape, lambda h, b, i: (0, 0, 0), pipeline_mode=once),
                  pl.BlockSpec((None, None, nc, V_ROWS, SEL_CHUNK), lambda h, b, i: (h, b, 0, 0, 0)),
                  pl.BlockSpec((None, SEL_NEAR + 2, SEL_CHUNK, width), lambda h, b, i: (h, 0, 0, 0),
                               pipeline_mode=once)],
        out_specs=pl.BlockSpec((None, SEL_PAIR, HEAD_DIM, QG), lambda h, b, i: (h, b * nc + i, 0, 0)),
        out_shape=jax.ShapeDtypeStruct((N_KV, B * nq, HEAD_DIM, QG), F32),
        scratch_shapes=[pltpu.VMEM((8, width), F32), pltpu.VMEM((V_ROWS, width), F32),
                        pltpu.VMEM((SEL_FAR * SEL_CHUNK, width), F32),
                        pltpu.VMEM((SEL_FAR * SEL_CHUNK, width), F32)],
        compiler_params=_cparams(("parallel", "parallel", "arbitrary")),
        name="nsa_selected",
    )(qt, mbt, k_sel, onehot, vt, ts)


def _win_kernel(*refs):
    nk = WINDOW // Q_BLOCK + WIN_PAIR
    q_ref = refs[0]
    k_refs = refs[1:1 + nk]
    v_refs = refs[1 + nk:1 + 2 * nk]
    tw_ref, gate_ref, oc_ref, os_ref, o_ref = refs[1 + 2 * nk:]
    first = WIN_PAIR * pl.program_id(2) - WINDOW // Q_BLOCK
    qt = jnp.concatenate([q_ref[g * HEAD_DIM:(g + 1) * HEAD_DIM, c * Q_BLOCK:(c + 1) * Q_BLOCK]
                          for c in range(WIN_PAIR) for g in range(GROUP)], axis=1)
    parts = []
    for i in range(nk):
        s = jnp.dot(k_refs[i][...], qt, preferred_element_type=F32)
        s = s + tw_ref[i * Q_BLOCK:(i + 1) * Q_BLOCK, :]
        parts.append(jnp.where(first + i >= 0, s, NEG))
    s = jnp.concatenate(parts, axis=0)
    m = jnp.max(s, axis=0, keepdims=True)
    pb = jnp.exp2((s - m).astype(BF16))
    ow = jnp.dot(v_refs[0][...], pb[0:Q_BLOCK], preferred_element_type=F32)
    for i in range(1, nk):
        ow = ow + jnp.dot(v_refs[i][...], pb[i * Q_BLOCK:(i + 1) * Q_BLOCK], preferred_element_type=F32)
    ow = ow[:HEAD_DIM] / jnp.maximum(ow[HEAD_DIM:HEAD_DIM + 1], 1e-30)

    def gate(j):
        return jnp.concatenate([gate_ref[j, g:g + 1, c * Q_BLOCK:(c + 1) * Q_BLOCK]
                                for c in range(WIN_PAIR) for g in range(GROUP)], axis=1)

    def both(ref):
        return jnp.concatenate([ref[c] for c in range(WIN_PAIR)], axis=1)

    o = gate(0) * both(oc_ref) + gate(1) * both(os_ref) + gate(2) * ow
    for c in range(WIN_PAIR):
        for g in range(GROUP):
            col = c * QG + g * Q_BLOCK
            o_ref[c * Q_BLOCK:(c + 1) * Q_BLOCK, g * HEAD_DIM:(g + 1) * HEAD_DIM] = (
                o[:, col:col + Q_BLOCK].T.astype(BF16))


def _win_attention(qt, kw, vwt, tw, gates_t, oc, osel, *, B, S):
    nq = S // Q_BLOCK
    npair = nq // WIN_PAIR
    back = WINDOW // Q_BLOCK
    nk = back + WIN_PAIR

    def kmap(j):
        return lambda h, b, i: (N_KV + h, b * nq + jnp.maximum(WIN_PAIR * i - back + j, 0), 0)

    def vmap_(j):
        return lambda h, b, i: (h, 0, b * nq + jnp.maximum(WIN_PAIR * i - back + j, 0))

    blk = pl.BlockSpec((None, WIN_PAIR, HEAD_DIM, QG), lambda h, b, i: (h, b * npair + i, 0, 0))
    return pl.pallas_call(
        _win_kernel,
        grid=(N_KV, B, npair),
        in_specs=([pl.BlockSpec((GROUP * HEAD_DIM, WIN_PAIR * Q_BLOCK), lambda h, b, i: (h, b * npair + i))]
                  + [pl.BlockSpec((None, Q_BLOCK, HEAD_DIM), kmap(j)) for j in range(nk)]
                  + [pl.BlockSpec((None, V_ROWS, Q_BLOCK), vmap_(j)) for j in range(nk)]
                  + [pl.BlockSpec((None, nk * Q_BLOCK, WIN_PAIR * QG), lambda h, b, i: (h, 0, 0)),
                     pl.BlockSpec((3, None, GROUP, WIN_PAIR * Q_BLOCK),
                                  lambda h, b, i: (0, h, 0, b * npair + i)),
                     blk, blk]),
        out_specs=pl.BlockSpec((WIN_PAIR * Q_BLOCK, GROUP * HEAD_DIM), lambda h, b, i: (b * npair + i, h)),
        out_shape=jax.ShapeDtypeStruct((B * S, N_HEADS * HEAD_DIM), BF16),
        compiler_params=_cparams(("parallel", "parallel", "parallel")),
        name="nsa_window_merge",
    )(qt, *([kw] * nk), *([vwt] * nk), tw, gates_t, oc, osel)


def _nsa_layer(x32, xb, p, tables, consts, ln_g, ln_b, *, B, S):
    T = B * S
    tc, ts, tw = tables
    overlap_t, sel_onehot = consts
    kd = N_KV * HEAD_DIM
    qv = _proj_nt(p["w_qv_t"], xb, out_dtype=BF16, scale=HEAD_DIM ** -0.5 * LOG2E,
                  n_scaled=N_HEADS * HEAD_DIM // 512)
    qt = qv
    k2 = _proj_nn(xb, p["w_k"], out_dtype=BF16, head_major=True)
    cmp = _proj_nn(xb, p["w_cmp"], out_dtype=F32, head_major=True)
    gates_t = _proj_nt(p["w_gate_t"], xb, out_dtype=F32, act="sigmoid")
    gates_t = gates_t.reshape(3, N_KV, GROUP, T)

    cmp = cmp.reshape(2 * N_KV, B, S, HEAD_DIM)
    kc = _compress(cmp, 0, p["pe_k"], p["w1_k"], p["w2_k"], B=B, S=S, transposed=False)
    vct = _compress(cmp, N_KV, p["pe_v"], p["w1_v"], p["w2_v"], B=B, S=S, transposed=True)
    oc, mbt = _cmp_attention(qt, kc, vct, tc, overlap_t, B=B, S=S)
    if mbt.shape[2] % LANE:
        mbt = jnp.pad(mbt, ((0, 0), (0, 0), (0, -mbt.shape[2] % LANE), (0, 0)), constant_values=NEG)

    nc = S // SEL_CHUNK
    k_sel = k2.reshape(2 * N_KV, B, nc, SEL_CHUNK, HEAD_DIM)
    v_sel_t = qv[N_HEADS * HEAD_DIM:N_HEADS * HEAD_DIM + kd].reshape(N_KV, HEAD_DIM, B, nc, SEL_CHUNK)
    v_sel_t = v_sel_t.transpose(0, 2, 3, 1, 4)
    ones_rows = (jnp.arange(V_ROWS - HEAD_DIM) == 0).astype(BF16)[:, None]
    v_sel_t = jnp.concatenate(
        [v_sel_t, jnp.broadcast_to(ones_rows, v_sel_t.shape[:3] + (V_ROWS - HEAD_DIM, SEL_CHUNK))], axis=3)
    osel = _sel_attention(qt, mbt, k_sel, sel_onehot, v_sel_t, ts, B=B, S=S)

    v_win_t = qv[N_HEADS * HEAD_DIM + kd:].reshape(N_KV, HEAD_DIM, T)
    v_win_t = jnp.concatenate(
        [v_win_t, jnp.broadcast_to(ones_rows, (N_KV, V_ROWS - HEAD_DIM, T))], axis=1)
    attn = _win_attention(qt, k2, v_win_t, tw, gates_t, oc, osel, B=B, S=S)
    return _proj_ln(attn, p["w_out"], x32, ln_g, ln_b, *p["router"])


def _nsa_params(w_in, pe_k, w1_k, w2_k, pe_v, w1_v, w2_v, w_out):
    D = w_in.shape[0]
    qd, kd = N_HEADS * HEAD_DIM, N_KV * HEAD_DIM
    w_kv = w_in[:, qd:qd + 6 * kd].reshape(D, 6, kd)
    gcol = np.array([[[(kv * GROUP + g) * 3 + j for g in range(GROUP)] for kv in range(N_KV)]
                     for j in range(3)]).reshape(-1)
    return dict(
        w_qv_t=jnp.concatenate([w_in[:, :qd], w_kv[:, 3], w_kv[:, 5]], axis=1).T.astype(BF16),
        w_k=jnp.concatenate([w_kv[:, 2], w_kv[:, 4]], axis=1).astype(BF16),
        w_cmp=jnp.concatenate([w_kv[:, 0], w_kv[:, 1]], axis=1).astype(BF16),
        w_gate_t=w_in[:, qd + 6 * kd:][:, gcol].T.astype(BF16),
        pe_k=pe_k, w1_k=w1_k.astype(BF16), w2_k=w2_k.astype(BF16),
        pe_v=pe_v, w1_v=w1_v.astype(BF16), w2_v=w2_v.astype(BF16),
        w_out=w_out.astype(BF16))


def _nsa_constants(S):
    n_cmp, n_sel = S // CMP_STRIDE, S // SEL_LEN
    cj = np.arange(n_cmp) * CMP_STRIDE
    sb = np.arange(n_sel) * SEL_LEN
    ov = np.maximum(np.minimum(cj[None, :] + CMP_LEN, sb[:, None] + SEL_LEN)
                    - np.maximum(cj[None, :], sb[:, None]), 0).astype(np.float32) / CMP_LEN
    ov[:, n_cmp - 1] = 0.0
    key = np.arange(SEL_CHUNK)
    chunk = np.arange(min(S // SEL_CHUNK, LANE * SEL_LEN // SEL_CHUNK))
    blk = (chunk[:, None] * (SEL_CHUNK // SEL_LEN) + key[None, :] // SEL_LEN) % LANE
    onehot = (blk[:, :, None] == np.arange(LANE)[None, None, :]).astype(np.float32)
    return jnp.asarray(ov, BF16), jnp.asarray(onehot, BF16)


def kernel(x, rel_bias, router_w, router_b, ln1_g, ln1_b, ln2_g, ln2_b, lru_w_in, lru_conv_w,
           lru_conv_b, lru_w_a, lru_b_a, lru_w_i, lru_b_i, lru_lambda, lru_w_out, nsa_w_in,
           nsa_pe_k, nsa_w1_k, nsa_w2_k, nsa_pe_v, nsa_w1_v, nsa_w2_v, nsa_w_out,
           moe_w_gate, moe_w_up, moe_w_down):
    B, S, D = x.shape
    T = B * S
    assert D == D_MODEL and S % (CMP_TILE * CMP_STRIDE) == 0 and S // SEL_LEN >= SEL_TOP
    qd, kd = N_HEADS * HEAD_DIM, N_KV * HEAD_DIM
    tables = _bias_tables(rel_bias, S)
    consts = _nsa_constants(S)
    router = _router_params(router_w, router_b)
    moe_bf16 = [w.astype(BF16) for w in (moe_w_gate, moe_w_up, moe_w_down)]

    x32 = x.reshape(T, D)
    xb = x32.astype(BF16)
    for layer in range(DEPTH):
        j = layer // 2
        if layer % 2 == 0:
            w_in = lru_w_in[j].astype(BF16)
            p = dict(w_gate=w_in[:, :D], w_y=w_in[:, D:], conv_w=lru_conv_w[j], conv_b=lru_conv_b[j],
                     w_a=lru_w_a[j].astype(BF16), b_a=lru_b_a[j], w_i=lru_w_i[j].astype(BF16),
                     b_i=lru_b_i[j], lam=lru_lambda[j], w_out=lru_w_out[j].astype(BF16), router=router)
            x32, xb, logits = _rglru_layer(x32, xb, p, ln1_g[layer], ln1_b[layer], B=B, S=S)
        else:
            p = _nsa_params(nsa_w_in[j], nsa_pe_k[j], nsa_w1_k[j], nsa_w2_k[j], nsa_pe_v[j],
                            nsa_w1_v[j], nsa_w2_v[j], nsa_w_out[j])
            p["router"] = router
            x32, xb, logits = _nsa_layer(x32, xb, p, tables, consts, ln1_g[layer], ln1_b[layer], B=B, S=S)
        x32, xb = _moe_layer(x32, logits, moe_bf16[0], moe_bf16[1], moe_bf16[2], layer,
                             ln2_g[layer], ln2_b[layer])
    return x32.reshape(B, S, D)
```

```python
import functools
import math

import numpy as np
import jax
import jax.numpy as jnp
from jax import lax
from jax.experimental import pallas as pl
from jax.experimental.pallas import tpu as pltpu

F32 = jnp.float32
BF16 = jnp.bfloat16

D_MODEL = 2048
DEPTH = 4
LRU_BLOCK = 256
LRU_NBLK = D_MODEL // LRU_BLOCK
CONV_W = 4
LRU_C = 8.0
HEAD_DIM = 128
N_HEADS = 16
N_KV = 4
GROUP = 4
CMP_LEN = 32
CMP_STRIDE = 16
SEL_LEN = 64
SEL_TOP = 16
WINDOW = 512
Q_BLOCK = 128
NUM_BUCKETS = 32
MAX_DISTANCE = 1024
N_EXPERTS = 16
N_GROUPS = 4
EXPERTS_PER_GROUP = 4
D_EXPERT = 1024
ALPHA = (2 * DEPTH) ** 0.25
LN_EPS = 1e-5
NEG = -1e30
BIG = 1e9

LANE = 128
QG = GROUP * Q_BLOCK
EXPERT_ROWS = 256
SEL_CHUNK = 256
SEL_NEAR = 5
SEL_FAR = 4
V_ROWS = HEAD_DIM + 16
SEL_PAIR = SEL_CHUNK // Q_BLOCK
WIN_PAIR = 2
CMP_PAIR = 2
LOG2E = math.log2(math.e)
CMP_TILE = 128
CMP_NEAR = 23
VMEM_LIMIT = 56 << 20


def _cparams(sem, vmem=VMEM_LIMIT):
    return pltpu.CompilerParams(dimension_semantics=sem, vmem_limit_bytes=vmem)


def _act(x, act):
    if act == "gelu":
        return jax.nn.gelu(x)
    if act == "sigmoid":
        return jax.nn.sigmoid(x)
    return x


def _proj_nn_kernel(x_ref, w_ref, o_ref, *, act, head_major):
    acc = jnp.dot(x_ref[...], w_ref[...], preferred_element_type=F32)
    acc = _act(acc, act)
    if head_major:
        for h in range(o_ref.shape[0]):
            o_ref[h] = acc[:, h * LANE:(h + 1) * LANE].astype(o_ref.dtype)
    else:
        o_ref[...] = acc.astype(o_ref.dtype)


def _proj_nn(x, w, *, out_dtype, act=None, head_major=False, tm=1024, tn=1024):
    T, K = x.shape
    N = w.shape[1]
    tm, tn = min(tm, T), min(tn, N)
    if head_major:
        out_shape = (N // LANE, T, LANE)
        out_spec = pl.BlockSpec((tn // LANE, tm, LANE), lambda j, i: (j, i, 0))
    else:
        out_shape = (T, N)
        out_spec = pl.BlockSpec((tm, tn), lambda j, i: (i, j))
    return pl.pallas_call(
        functools.partial(_proj_nn_kernel, act=act, head_major=head_major),
        grid=(N // tn, T // tm),
        in_specs=[pl.BlockSpec((tm, K), lambda j, i: (i, 0)),
                  pl.BlockSpec((K, tn), lambda j, i: (0, j))],
        out_specs=out_spec,
        out_shape=jax.ShapeDtypeStruct(out_shape, out_dtype),
        compiler_params=_cparams(("parallel", "parallel")),
        name="proj_nn",
    )(x, w)


def _proj_nt_kernel(wt_ref, x_ref, o_ref, *, act, scale, n_scaled):
    acc = lax.dot_general(wt_ref[...], x_ref[...], (((1,), (1,)), ((), ())),
                          preferred_element_type=F32)
    if n_scaled:
        acc = acc * jnp.where(pl.program_id(0) < n_scaled, scale, 1.0).astype(F32)
    o_ref[...] = _act(acc, act).astype(o_ref.dtype)


def _proj_nt(wt, x, *, out_dtype, act=None, scale=1.0, n_scaled=0, tm=1024, tn=512):
    N, K = wt.shape
    T = x.shape[0]
    tm, tn = min(tm, T), min(tn, N)
    return pl.pallas_call(
        functools.partial(_proj_nt_kernel, act=act, scale=scale, n_scaled=n_scaled),
        grid=(N // tn, T // tm),
        in_specs=[pl.BlockSpec((tn, K), lambda j, i: (j, 0)),
                  pl.BlockSpec((tm, K), lambda j, i: (i, 0))],
        out_specs=pl.BlockSpec((tn, tm), lambda j, i: (j, i)),
        out_shape=jax.ShapeDtypeStruct((N, T), out_dtype),
        compiler_params=_cparams(("parallel", "parallel")),
        name="proj_nt",
    )(wt, x)


def _layer_norm_rows(z, g, b):
    mu = jnp.mean(z, axis=-1, keepdims=True)
    zc = z - mu
    var = jnp.mean(zc * zc, axis=-1, keepdims=True)
    return zc * lax.rsqrt(var + LN_EPS) * g + b


def _proj_ln_kernel(a_ref, w_ref, res_ref, g_ref, b_ref, rw_ref, rb_ref, of_ref, ob_ref, lg_ref):
    acc = jnp.dot(a_ref[...], w_ref[...], preferred_element_type=F32)
    y = _layer_norm_rows(ALPHA * res_ref[...] + acc, g_ref[...], b_ref[...])
    of_ref[...] = y
    y_hi = y.astype(BF16)
    ob_ref[...] = y_hi
    y_lo = (y - y_hi.astype(F32)).astype(BF16)
    rows = y.shape[0]
    both = jnp.dot(jnp.concatenate([y_hi, y_lo], axis=0), rw_ref[0], preferred_element_type=F32)
    lg_ref[...] = (both[:rows] + both[rows:] + jnp.dot(y_hi, rw_ref[1], preferred_element_type=F32)
                   + rb_ref[...])


def _proj_ln(a, w, res, g, b, router_w, router_b, *, tm=256):
    T, K = a.shape
    D = w.shape[1]
    tm = min(tm, T)
    row = pl.BlockSpec((tm, D), lambda i: (i, 0))
    vec = pl.BlockSpec((1, D), lambda i: (0, 0))
    lgt = pl.BlockSpec((tm, LANE), lambda i: (i, 0))
    return pl.pallas_call(
        _proj_ln_kernel,
        grid=(T // tm,),
        in_specs=[pl.BlockSpec((tm, K), lambda i: (i, 0)),
                  pl.BlockSpec((K, D), lambda i: (0, 0)),
                  row, vec, vec,
                  pl.BlockSpec((2, D, LANE), lambda i: (0, 0, 0)),
                  pl.BlockSpec((1, LANE), lambda i: (0, 0))],
        out_specs=[row, row, lgt],
        out_shape=[jax.ShapeDtypeStruct((T, D), F32), jax.ShapeDtypeStruct((T, D), BF16),
                   jax.ShapeDtypeStruct((T, LANE), F32)],
        compiler_params=_cparams(("parallel",)),
        name="proj_ln",
    )(a, w, res, g.reshape(1, D), b.reshape(1, D), router_w, router_b)


def _lru_kernel(y_ref, gate_ref, cw_ref, cb_ref, wa_ref, ba_ref, wi_ref, bi_ref, lam_ref,
                o_ref, h_ref, tail_ref, *, rows):
    @pl.when(pl.program_id(2) == 0)
    def _():
        h_ref[...] = jnp.zeros_like(h_ref)
        tail_ref[...] = jnp.zeros_like(tail_ref)

    y = y_ref[...]
    tail = tail_ref[...]
    cw = cw_ref[...]
    row8 = lax.broadcasted_iota(jnp.int32, (8, LRU_BLOCK), 0)
    yc = cb_ref[...] + y * cw[CONV_W - 1:CONV_W, :]
    for d in range(1, CONV_W):
        ys = pltpu.roll(y, d, 0)
        first = jnp.where(row8 < d, pltpu.roll(tail, d, 0), ys[:8])
        ys = jnp.concatenate([first, ys[8:]], axis=0)
        yc = yc + ys * cw[CONV_W - 1 - d:CONV_W - d, :]
    tail_ref[...] = y[rows - 8:]

    yb = yc.astype(BF16)
    r = jax.nn.sigmoid(jnp.dot(yb, wa_ref[...], preferred_element_type=F32) + ba_ref[...])
    ig = jax.nn.sigmoid(jnp.dot(yb, wi_ref[...], preferred_element_type=F32) + bi_ref[...])
    z = -lam_ref[...]
    softplus = jnp.maximum(z, 0.0) + jnp.log1p(jnp.exp(-jnp.abs(z)))
    log_a = (-LRU_C * softplus) * r
    a = jnp.exp(log_a)
    h = jnp.sqrt(1.0 - a * a) * (ig * yc)

    rowi = lax.broadcasted_iota(jnp.int32, (rows, LRU_BLOCK), 0)
    s = 1
    while s < rows:
        if s < 8:
            keep = rowi >= s
            h = jnp.where(keep, h + a * pltpu.roll(h, s, 0), h)
            a = jnp.where(keep, a * pltpu.roll(a, s, 0), a)
        else:
            h = jnp.concatenate([h[:s], h[s:] + a[s:] * h[:rows - s]], axis=0)
            a = jnp.concatenate([a[:s], a[s:] * a[:rows - s]], axis=0)
        s *= 2
    h = h + a * h_ref[0:1, :]
    h_ref[...] = jnp.broadcast_to(h[rows - 1:rows, :], h_ref.shape)
    o_ref[...] = (h * gate_ref[...].astype(F32)).astype(BF16)


def _lru_core(y, gate, conv_w, conv_b, w_a, b_a, w_i, b_i, lam, *, B, S, rows=512):
    T, W = y.shape
    rows = min(rows, S)
    ns = S // rows
    blk = pl.BlockSpec((rows, LRU_BLOCK), lambda n, b, s: (b * ns + s, n))
    vec = pl.BlockSpec((1, LRU_BLOCK), lambda n, b, s: (0, n))
    wblk = pl.BlockSpec((None, LRU_BLOCK, LRU_BLOCK), lambda n, b, s: (n, 0, 0))
    return pl.pallas_call(
        functools.partial(_lru_kernel, rows=rows),
        grid=(LRU_NBLK, B, ns),
        in_specs=[blk, blk,
                  pl.BlockSpec((CONV_W, LRU_BLOCK), lambda n, b, s: (0, n)), vec,
                  wblk, vec, wblk, vec, vec],
        out_specs=blk,
        out_shape=jax.ShapeDtypeStruct((T, W), BF16),
        scratch_shapes=[pltpu.VMEM((8, LRU_BLOCK), F32), pltpu.VMEM((8, LRU_BLOCK), F32)],
        compiler_params=_cparams(("parallel", "parallel", "arbitrary")),
        name="lru_core",
    )(y, gate, conv_w, conv_b.reshape(1, W), w_a, b_a.reshape(1, W), w_i, b_i.reshape(1, W),
      lam.reshape(1, W))


def _rglru_layer(x32, xb, p, ln_g, ln_b, *, B, S):
    gate = _proj_nn(xb, p["w_gate"], out_dtype=BF16, act="gelu")
    y = _proj_nn(xb, p["w_y"], out_dtype=F32)
    hg = _lru_core(y, gate, p["conv_w"], p["conv_b"], p["w_a"], p["b_a"], p["w_i"], p["b_i"],
                   p["lam"], B=B, S=S)
    return _proj_ln(hg, p["w_out"], x32, ln_g, ln_b, *p["router"])


def _router_kernel(lg_ref, ids_ref, wts_ref, pos_ref, cnt_ref, carry_ref, *, tm):
    @pl.when(pl.program_id(0) == 0)
    def _():
        carry_ref[...] = jnp.zeros_like(carry_ref)

    logits = lg_ref[...].T[:N_EXPERTS]
    m = jnp.max(logits, axis=0, keepdims=True)
    e = jnp.exp(logits - m)
    p = e / jnp.sum(e, axis=0, keepdims=True)
    rows = [p[k:k + 1, :] for k in range(N_EXPERTS)]

    scores = []
    for g in range(N_GROUPS):
        a, b, c, d = rows[4 * g:4 * g + 4]
        hi1, lo1, hi2, lo2 = jnp.maximum(a, b), jnp.minimum(a, b), jnp.maximum(c, d), jnp.minimum(c, d)
        scores.append(jnp.maximum(hi1, hi2) + jnp.maximum(jnp.minimum(hi1, hi2), jnp.maximum(lo1, lo2)))
    best, gsel = scores[0], jnp.zeros((1, tm), jnp.int32)
    for g in range(1, N_GROUPS):
        upd = scores[g] > best
        gsel = jnp.where(upd, g, gsel)
        best = jnp.where(upd, scores[g], best)

    cand = []
    for k in range(EXPERTS_PER_GROUP):
        c = rows[k]
        for g in range(1, N_GROUPS):
            c = jnp.where(gsel == g, rows[4 * g + k], c)
        cand.append(c)
    v1, i1 = cand[0], jnp.zeros((1, tm), jnp.int32)
    for k in range(1, EXPERTS_PER_GROUP):
        upd = cand[k] > v1
        i1 = jnp.where(upd, k, i1)
        v1 = jnp.where(upd, cand[k], v1)
    v2, i2 = jnp.full((1, tm), -1.0, F32), jnp.zeros((1, tm), jnp.int32)
    for k in range(EXPERTS_PER_GROUP):
        c = jnp.where(i1 == k, -1.0, cand[k])
        upd = c > v2
        i2 = jnp.where(upd, k, i2)
        v2 = jnp.where(upd, c, v2)
    e0 = gsel * EXPERTS_PER_GROUP + i1
    e1 = gsel * EXPERTS_PER_GROUP + i2
    tot = v1 + v2

    eio = lax.broadcasted_iota(jnp.int32, (N_EXPERTS, tm), 0)
    hit0, hit1 = eio == e0, eio == e1
    onehot = jnp.where(hit0 | hit1, 1.0, 0.0)
    before = (lax.broadcasted_iota(jnp.int32, (tm, tm), 0)
              < lax.broadcasted_iota(jnp.int32, (tm, tm), 1))
    rank = jnp.dot(onehot.astype(BF16), jnp.where(before, 1.0, 0.0).astype(BF16),
                   preferred_element_type=F32) + carry_ref[:, 0:1]
    pos0 = jnp.sum(jnp.where(hit0, rank, 0.0), axis=0, keepdims=True)
    pos1 = jnp.sum(jnp.where(hit1, rank, 0.0), axis=0, keepdims=True)
    carry_ref[...] = carry_ref[...] + jnp.sum(onehot, axis=1, keepdims=True)
    cnt_ref[...] = carry_ref[...]

    zi = jnp.zeros((6, tm), jnp.int32)
    ids_ref[...] = jnp.concatenate([e0, e1, zi], axis=0)
    pos_ref[...] = jnp.concatenate([pos0.astype(jnp.int32), pos1.astype(jnp.int32), zi], axis=0)
    wts_ref[...] = jnp.concatenate([v1 / tot, v2 / tot, jnp.zeros((6, tm), F32)], axis=0)


def _router(logits, *, tm=512):
    T = logits.shape[0]
    tm = min(tm, T)
    tok = pl.BlockSpec((8, tm), lambda i: (0, i))
    return pl.pallas_call(
        functools.partial(_router_kernel, tm=tm),
        grid=(T // tm,),
        in_specs=[pl.BlockSpec((tm, LANE), lambda i: (i, 0))],
        out_specs=[tok, tok, tok, pl.BlockSpec((N_EXPERTS, LANE), lambda i: (0, 0))],
        out_shape=[jax.ShapeDtypeStruct((8, T), jnp.int32), jax.ShapeDtypeStruct((8, T), F32),
                   jax.ShapeDtypeStruct((8, T), jnp.int32),
                   jax.ShapeDtypeStruct((N_EXPERTS, LANE), F32)],
        scratch_shapes=[pltpu.VMEM((N_EXPERTS, LANE), F32)],
        compiler_params=_cparams(("arbitrary",)),
        name="moe_router",
    )(logits)


def _router_params(router_w, router_b):
    pad = LANE - N_EXPERTS
    w = jnp.pad(router_w, ((0, 0), (0, pad)))
    w_hi = w.astype(BF16)
    w_lo = (w - w_hi.astype(F32)).astype(BF16)
    return (jnp.stack([w_hi, w_lo]), jnp.pad(router_b, (0, pad)).reshape(1, LANE))


def _row_copy(src, src_row, dst, dst_row, sem, n=1):
    return pltpu.make_async_copy(src.at[pl.ds(src_row, n)], dst.at[pl.ds(dst_row, n)], sem)


def _row_tokens_kernel(n_ref, lo_ref, hi_ref, d0_ref, d1_ref, tok_ref):
    step = 8

    def clear(r, c):
        tok_ref[r] = 0
        return c

    def put(i, c):
        for k in range(step):
            t = i * step + k
            tok_ref[d0_ref[t]] = t
            tok_ref[d1_ref[t]] = t
        return c

    for e in range(lo_ref.shape[0]):
        lax.fori_loop(lo_ref[e], hi_ref[e], clear, 0)
    lax.fori_loop(0, n_ref[0] // step, put, 0)


def _row_tokens(d0, d1, free_lo, free_hi, n_rows):
    n_tok = d0.shape[0]
    assert n_tok % 8 == 0
    return pl.pallas_call(
        _row_tokens_kernel,
        grid_spec=pltpu.PrefetchScalarGridSpec(
            num_scalar_prefetch=5, grid=(1,), in_specs=[],
            out_specs=pl.BlockSpec(memory_space=pltpu.SMEM)),
        out_shape=jax.ShapeDtypeStruct((n_rows,), jnp.int32),
        compiler_params=pltpu.CompilerParams(dimension_semantics=("arbitrary",)),
        name="moe_row_tokens",
    )(jnp.array([n_tok], jnp.int32), free_lo, free_hi, d0, d1)


def _ffn_kernel(be_ref, nact_ref, tok_ref, x_hbm, wg_ref, wu_ref, wd_ref, y_ref, xbuf, sems):
    i = pl.program_id(0)
    n_act = nact_ref[0]
    R = EXPERT_ROWS

    def gather(block, slot):
        def issue(r, c):
            _row_copy(x_hbm, tok_ref[block * R + r], xbuf.at[slot], r, sems.at[slot]).start()
            return c
        lax.fori_loop(0, R, issue, 0, unroll=8)

    def run(slot):
        @pl.when(i + 1 < n_act)
        def _():
            gather(i + 1, 1 - slot)

        def drain(r, c):
            _row_copy(x_hbm, 0, xbuf.at[slot], 0, sems.at[slot]).wait()
            return c
        lax.fori_loop(0, R, drain, 0, unroll=8)
        xb = xbuf[slot].astype(BF16)
        g = jnp.dot(xb, wg_ref[...], preferred_element_type=F32)
        u = jnp.dot(xb, wu_ref[...], preferred_element_type=F32)
        h = (jax.nn.silu(g) * u).astype(BF16)
        y_ref[...] = jnp.dot(h, wd_ref[...], preferred_element_type=F32)

    @pl.when(i == 0)
    def _():
        gather(0, 0)

    for slot in range(2):
        @pl.when((i < n_act) & (i % 2 == slot))
        def _():
            run(slot)

    @pl.when(i >= n_act)
    def _():
        y_ref[...] = jnp.zeros_like(y_ref)


def _expert_ffn(x32, row_tok, blk_e, n_act, w_gate, w_up, w_down, layer):
    D = x32.shape[1]
    R = EXPERT_ROWS
    n_rows = row_tok.shape[0]
    wmap = lambda i, be, na, tok: (layer, be[i], 0, 0)
    return pl.pallas_call(
        _ffn_kernel,
        grid_spec=pltpu.PrefetchScalarGridSpec(
            num_scalar_prefetch=3, grid=(n_rows // R,),
            in_specs=[pl.BlockSpec(memory_space=pl.ANY),
                      pl.BlockSpec((None, None, D, D_EXPERT), wmap),
                      pl.BlockSpec((None, None, D, D_EXPERT), wmap),
                      pl.BlockSpec((None, None, D_EXPERT, D), wmap)],
            out_specs=pl.BlockSpec((R, D), lambda i, be, na, tok: (i, 0)),
            scratch_shapes=[pltpu.VMEM((2, R, D), F32), pltpu.SemaphoreType.DMA((2,))]),
        out_shape=jax.ShapeDtypeStruct((n_rows, D), F32),
        compiler_params=_cparams(("arbitrary",)),
        name="moe_ffn",
    )(blk_e, n_act, row_tok, x32, w_gate, w_up, w_down)


def _combine_kernel(d0_ref, d1_ref, res_ref, w_ref, g_ref, b_ref, y_hbm, of_ref, ob_ref,
                    buf, sem, *, tm):
    i = pl.program_id(0)

    def issue(r, c):
        t = i * tm + r
        _row_copy(y_hbm, d0_ref[t], buf.at[0], r, sem).start()
        _row_copy(y_hbm, d1_ref[t], buf.at[1], r, sem).start()
        return c

    def drain(r, c):
        _row_copy(y_hbm, 0, buf.at[0], 0, sem).wait()
        _row_copy(y_hbm, 0, buf.at[1], 0, sem).wait()
        return c

    lax.fori_loop(0, tm, issue, 0)
    lax.fori_loop(0, tm, drain, 0)
    w = w_ref[...]
    z = ALPHA * res_ref[...] + (w[:, 0:1] * buf[0] + w[:, 1:2] * buf[1])
    y = _layer_norm_rows(z, g_ref[...], b_ref[...])
    of_ref[...] = y
    ob_ref[...] = y.astype(BF16)


def _combine_ln(y, d0, d1, wts, res, g, b, *, tm=256):
    T, D = res.shape
    tm = min(tm, T)
    row = pl.BlockSpec((tm, D), lambda i, a, c: (i, 0))
    vec = pl.BlockSpec((1, D), lambda i, a, c: (0, 0))
    return pl.pallas_call(
        functools.partial(_combine_kernel, tm=tm),
        grid_spec=pltpu.PrefetchScalarGridSpec(
            num_scalar_prefetch=2, grid=(T // tm,),
            in_specs=[row, pl.BlockSpec((tm, 2), lambda i, a, c: (i, 0)), vec, vec,
                      pl.BlockSpec(memory_space=pl.ANY)],
            out_specs=[row, row],
            scratch_shapes=[pltpu.VMEM((2, tm, D), F32), pltpu.SemaphoreType.DMA(())]),
        out_shape=[jax.ShapeDtypeStruct((T, D), F32), jax.ShapeDtypeStruct((T, D), BF16)],
        compiler_params=_cparams(("arbitrary",)),
        name="moe_combine_ln",
    )(d0, d1, res, wts, g.reshape(1, D), b.reshape(1, D), y)


def _moe_layer(x32, logits, w_gate, w_up, w_down, layer, ln_g, ln_b):
    T = x32.shape[0]
    R = EXPERT_ROWS
    ids, wts, pos, cnt = _router(logits)
    counts = cnt[:, 0].astype(jnp.int32)
    padded = (counts + R - 1) // R * R
    p_end = jnp.cumsum(padded)
    offs = p_end - padded
    d0 = offs[ids[0]] + pos[0]
    d1 = offs[ids[1]] + pos[1]
    n_blocks = 2 * T // R + N_EXPERTS
    n_act = p_end[-1] // R
    step = jnp.arange(n_blocks, dtype=jnp.int32)
    live = step < n_act
    src = jnp.where(live, step, n_act - 1)
    blk_e = jnp.minimum(jnp.sum(src[:, None] >= (p_end // R)[None, :], axis=1),
                        N_EXPERTS - 1).astype(jnp.int32)
    n_rows = n_blocks * R
    free_lo = jnp.concatenate([offs + counts, p_end[-1:]]).astype(jnp.int32)
    free_hi = jnp.concatenate([p_end, jnp.array([n_rows], p_end.dtype)]).astype(jnp.int32)
    row_tok = _row_tokens(d0, d1, free_lo, free_hi, n_rows)
    y = _expert_ffn(x32, row_tok, blk_e, n_act.reshape(1).astype(jnp.int32), w_gate, w_up, w_down, layer)
    return _combine_ln(y, d0, d1, wts[:2].T, x32, ln_g, ln_b)


def _t5_bucket(dist):
    n = jnp.maximum(dist, 0)
    max_exact = NUM_BUCKETS // 2
    nf = jnp.maximum(n, 1).astype(F32)
    large = max_exact + (jnp.log(nf / max_exact) / math.log(MAX_DISTANCE / max_exact)
                         * (NUM_BUCKETS - max_exact)).astype(jnp.int32)
    large = jnp.minimum(large, NUM_BUCKETS - 1)
    return jnp.where(n < max_exact, n, large)


def _stack_heads(tbl):
    parts = tbl.reshape((N_KV, GROUP) + tbl.shape[1:])
    return jnp.concatenate([parts[:, g] for g in range(GROUP)], axis=-1)


def _bias_tables(rel_bias, S):
    off = CMP_STRIDE * CMP_TILE + Q_BLOCK
    n_dist = CMP_NEAR * Q_BLOCK
    bvec = rel_bias[_t5_bucket(jnp.arange(n_dist))].T
    last = rel_bias[NUM_BUCKETS - 1]
    masked = jnp.full((N_HEADS, off), NEG, F32)
    w_abs = jnp.concatenate([masked, bvec * LOG2E], axis=1)
    w_rel = jnp.concatenate([masked, (bvec - last[:, None]) * LOG2E], axis=1)
    w_win = jnp.where(jnp.arange(-off, n_dist) < WINDOW, w_abs, NEG)

    def toeplitz(w, first, step, n):
        lo = first - step * (n - 1) + off
        span = step * (n - 1) + Q_BLOCK
        u = w[:, lo:lo + span]
        width = span + step
        tiled = jnp.tile(u, (1, -(-n * width // span)))[:, :n * width]
        return tiled.reshape(N_HEADS, n, width)[:, ::-1, :Q_BLOCK]

    def hankel(u, n, m, step):
        span = step * (n - 1) + m
        width = span + step
        u = u[..., :span, :]
        tiled = jnp.tile(u, (1,) * (u.ndim - 2) + (-(-n * width // span), 1))[..., :n * width, :]
        return tiled.reshape(u.shape[:-2] + (n, width, u.shape[-1]))[..., :m, :]

    per = Q_BLOCK // CMP_STRIDE
    n_seq = per * (CMP_NEAR - 1) + (per - 1) + CMP_TILE
    lo = off - (CMP_LEN - 1) - CMP_STRIDE * (CMP_TILE - 1)
    seq = w_abs[:, lo:lo + CMP_STRIDE * n_seq].reshape(N_HEADS, n_seq, CMP_STRIDE)
    by_tile = hankel(seq, CMP_NEAR, per - 1 + CMP_TILE, per)
    near_c = hankel(by_tile, per, CMP_TILE, 1)
    near_c = near_c[:, :, :, ::-1, :].transpose(0, 1, 3, 2, 4).reshape(N_HEADS, CMP_NEAR, CMP_TILE, Q_BLOCK)
    const = lambda v: jnp.broadcast_to(v, (N_HEADS, 1, CMP_TILE, Q_BLOCK)).astype(F32)
    tc = jnp.concatenate([const(NEG), near_c, const(last[:, None, None, None] * LOG2E)], axis=1)
    zero = jnp.zeros((N_HEADS, 1, SEL_CHUNK, Q_BLOCK), F32)
    ts = jnp.stack([
        jnp.concatenate(
            [zero] + [toeplitz(w_rel, Q_BLOCK * sub + SEL_CHUNK * (SEL_NEAR - 1 - i), 1, SEL_CHUNK)[:, None]
                      for i in range(SEL_NEAR)] + [zero + NEG], axis=1)
        for sub in range(SEL_CHUNK // Q_BLOCK)], axis=1)
    n_keys = WINDOW + WIN_PAIR * Q_BLOCK
    tw = jnp.concatenate([_stack_heads(toeplitz(w_win, WINDOW + Q_BLOCK * c, 1, n_keys))
                          for c in range(WIN_PAIR)], axis=-1)
    ts = _stack_heads(ts)
    ts = jnp.concatenate([ts[:, sub] for sub in range(SEL_PAIR)], axis=-1)
    return _stack_heads(tc), ts, tw


def _compress_kernel(x_ref, pe_ref, w1_ref, w2_ref, o_ref, *, transposed):
    n = x_ref.shape[0] // CMP_STRIDE
    a = b = None
    for tok in range(CMP_STRIDE):
        xt = x_ref[pl.ds(tok, n, stride=CMP_STRIDE), :]
        as_first = (xt + pe_ref[tok:tok + 1, :]).astype(BF16)
        as_second = (xt + pe_ref[CMP_STRIDE + tok:CMP_STRIDE + tok + 1, :]).astype(BF16)
        da = jnp.dot(as_first, w1_ref[tok * HEAD_DIM:(tok + 1) * HEAD_DIM], preferred_element_type=F32)
        db = jnp.dot(as_second, w1_ref[(CMP_STRIDE + tok) * HEAD_DIM:(CMP_STRIDE + tok + 1) * HEAD_DIM],
                     preferred_element_type=F32)
        a = da if a is None else a + da
        b = db if b is None else b + db
    pre = a + pltpu.roll(b, n - 1, 0)
    rowi = lax.broadcasted_iota(jnp.int32, pre.shape, 0)
    hid = jnp.where(rowi < n - 1, jax.nn.gelu(pre), 0.0).astype(BF16)
    if transposed:
        out = lax.dot_general(w2_ref[...], hid, (((1,), (1,)), ((), ())), preferred_element_type=F32)
    else:
        out = jnp.dot(hid, w2_ref[...], preferred_element_type=F32)
    o_ref[...] = out.astype(BF16)


def _compress(slabs, first, pe, w1, w2, *, B, S, transposed):
    n = S // CMP_STRIDE
    wide = CMP_STRIDE * HEAD_DIM
    out_shape = (N_KV, B, HEAD_DIM, n) if transposed else (N_KV, B, n, HEAD_DIM)
    w2_in = w2.T if transposed else w2
    return pl.pallas_call(
        functools.partial(_compress_kernel, transposed=transposed),
        grid=(N_KV, B),
        in_specs=[pl.BlockSpec((None, None, S, HEAD_DIM), lambda h, b: (first + h, b, 0, 0)),
                  pl.BlockSpec((CMP_LEN, HEAD_DIM), lambda h, b: (0, 0)),
                  pl.BlockSpec((2 * wide, HEAD_DIM), lambda h, b: (0, 0)),
                  pl.BlockSpec((HEAD_DIM, HEAD_DIM), lambda h, b: (0, 0))],
        out_specs=pl.BlockSpec((None, None) + out_shape[2:], lambda h, b: (h, b, 0, 0)),
        out_shape=jax.ShapeDtypeStruct(out_shape, BF16),
        compiler_params=_cparams(("parallel", "parallel")),
        name="nsa_compress",
    )(slabs, pe, w1, w2_in)


def _query_cols(q_ref):
    return jnp.concatenate([q_ref[g * HEAD_DIM:(g + 1) * HEAD_DIM, :] for g in range(GROUP)], axis=1)


def _cmp_body(ib, n_tiles, q_ref, kc_ref, vct_ref, tc_ref, ov_ref, oc_ref, mb_ref):
    n_cmp = n_tiles * CMP_TILE
    n_sel = n_cmp * CMP_STRIDE // SEL_LEN
    nq = CMP_PAIR * Q_BLOCK
    qt = jnp.concatenate([q_ref[g * HEAD_DIM:(g + 1) * HEAD_DIM, c * Q_BLOCK:(c + 1) * Q_BLOCK]
                          for c in range(CMP_PAIR) for g in range(GROUP)], axis=1)
    s = jnp.dot(kc_ref[:n_cmp], qt, preferred_element_type=F32)
    tiles = []
    for cb in range(n_tiles):
        bias = [tc_ref[jnp.clip(ib + c - (CMP_TILE * CMP_STRIDE // Q_BLOCK) * cb + 1, 0, CMP_NEAR + 1)]
                for c in range(CMP_PAIR)]
        tiles.append(s[cb * CMP_TILE:(cb + 1) * CMP_TILE] + jnp.concatenate(bias, axis=1))
    s = jnp.concatenate(tiles, axis=0)
    m = jnp.max(s, axis=0, keepdims=True)
    p = jnp.exp2(s - m)
    inv = jnp.where(m > 0.5 * NEG, 1.0 / jnp.maximum(jnp.sum(p, axis=0, keepdims=True), 1e-30), 0.0)
    p = p * inv
    oc = jnp.dot(vct_ref[:, :n_cmp], p.astype(BF16), preferred_element_type=F32)
    for c in range(CMP_PAIR):
        oc_ref[c] = oc[:, c * QG:(c + 1) * QG]
    ov = ov_ref[:n_sel, :n_cmp]

    sums = []
    for c in range(CMP_PAIR):
        acc = p[:, c * QG:c * QG + Q_BLOCK]
        for g in range(1, GROUP):
            acc = acc + p[:, c * QG + g * Q_BLOCK:c * QG + (g + 1) * Q_BLOCK]
        sums.append(acc)
    psum = jnp.concatenate(sums, axis=1)
    p_hi = psum.astype(BF16)
    p_lo = (psum - p_hi.astype(F32)).astype(BF16)
    imp = (jnp.dot(ov, p_hi, preferred_element_type=F32)
           + jnp.dot(ov, p_lo, preferred_element_type=F32))
    blk = lax.broadcasted_iota(jnp.int32, (n_sel, nq), 0)
    t = ib * Q_BLOCK + lax.broadcasted_iota(jnp.int32, (n_sel, nq), 1)
    cur = jnp.right_shift(t, int(math.log2(SEL_LEN)))
    ok = blk * SEL_LEN <= t
    forced = ok & ((blk == 0) | (blk == cur) | (blk == cur - 1))
    score = jnp.where(forced, -jnp.inf, jnp.where(ok, imp, NEG))
    taken = forced
    for _ in range(SEL_TOP - 3):
        top = jnp.max(score, axis=0, keepdims=True)
        first = jnp.min(jnp.where(score == top, blk, n_sel), axis=0, keepdims=True)
        pick = blk == first
        taken = taken | pick
        score = jnp.where(pick, -jnp.inf, score)
    mb_ref[:n_sel] = jnp.where(taken & ok, 0.0, NEG).astype(BF16)
    if n_sel < mb_ref.shape[0]:
        mb_ref[n_sel:] = jnp.full((mb_ref.shape[0] - n_sel, nq), NEG, BF16)


def _cmp_kernel(*refs, n_tiles):
    ib = CMP_PAIR * pl.program_id(2)
    per_tile = CMP_TILE * CMP_STRIDE // Q_BLOCK
    lo = 0
    for nt in range(1, n_tiles + 1):
        hi = nt * per_tile

        @pl.when((ib >= lo) & (ib < hi))
        def _():
            _cmp_body(ib, nt, *refs)
        lo = hi


def _cmp_attention(qt, kc, vct, tc, overlap_t, *, B, S):
    nq = S // Q_BLOCK
    npair = nq // CMP_PAIR
    n_cmp = S // CMP_STRIDE
    n_sel = S // SEL_LEN
    n_tiles = n_cmp // CMP_TILE
    return pl.pallas_call(
        functools.partial(_cmp_kernel, n_tiles=n_tiles),
        grid=(N_KV, B, npair),
        in_specs=[pl.BlockSpec((GROUP * HEAD_DIM, CMP_PAIR * Q_BLOCK), lambda h, b, i: (h, b * npair + i)),
                  pl.BlockSpec((None, None, n_cmp, HEAD_DIM), lambda h, b, i: (h, b, 0, 0)),
                  pl.BlockSpec((None, None, HEAD_DIM, n_cmp), lambda h, b, i: (h, b, 0, 0)),
                  pl.BlockSpec((None, CMP_NEAR + 2, CMP_TILE, QG), lambda h, b, i: (h, 0, 0, 0)),
                  pl.BlockSpec((n_sel, n_cmp), lambda h, b, i: (0, 0))],
        out_specs=[pl.BlockSpec((None, CMP_PAIR, HEAD_DIM, QG), lambda h, b, i: (h, b * npair + i, 0, 0)),
                   pl.BlockSpec((None, None, n_sel, CMP_PAIR * Q_BLOCK), lambda h, b, i: (h, b, 0, i))],
        out_shape=[jax.ShapeDtypeStruct((N_KV, B * nq, HEAD_DIM, QG), F32),
                   jax.ShapeDtypeStruct((N_KV, B, n_sel, S), BF16)],
        compiler_params=_cparams(("parallel", "parallel", "parallel")),
        name="nsa_cmp_select",
    )(qt, kc, vct, tc, overlap_t)


def _sel_kernel(q_ref, mb_ref, k_ref, oh_ref, vt_ref, ts_ref, o_ref, m_ref, acc_ref, sa_ref, sb_ref):
    cq = pl.program_id(2)
    qt = jnp.concatenate([q_ref[g * HEAD_DIM:(g + 1) * HEAD_DIM, c * Q_BLOCK:(c + 1) * Q_BLOCK]
                          for c in range(SEL_PAIR) for g in range(GROUP)], axis=1)
    blocks_per_chunk = SEL_CHUNK // SEL_LEN
    chunks_per_group = LANE // blocks_per_chunk

    m_ref[...] = jnp.full(m_ref.shape, NEG, F32)
    acc_ref[...] = jnp.zeros_like(acc_ref)

    qq = cq // SEL_FAR
    rows = SEL_FAR * SEL_CHUNK

    def raw_scores(quad, n=SEL_FAR):
        rows = n * SEL_CHUNK
        c0 = pl.multiple_of(quad * SEL_FAR, SEL_FAR)
        grp = c0 // chunks_per_group
        mb = mb_ref[pl.ds(pl.multiple_of(grp * LANE, LANE), LANE), :]
        mask = jnp.concatenate([mb[:, c * Q_BLOCK:(c + 1) * Q_BLOCK]
                                for c in range(SEL_PAIR) for _ in range(GROUP)], axis=1)
        qa = jnp.concatenate([qt, mask], axis=0)
        o0 = pl.multiple_of(c0 % oh_ref.shape[0], SEL_FAR)
        keys = jnp.concatenate([k_ref[pl.ds(c0, n)].reshape(rows, HEAD_DIM),
                                oh_ref[pl.ds(o0, n)].reshape(rows, HEAD_DIM)], axis=1)
        return jnp.dot(keys, qa, preferred_element_type=F32)

    def update(s, quad, n=SEL_FAR):
        c0 = quad * SEL_FAR
        v = jnp.concatenate([vt_ref[c0 + k] for k in range(n)], axis=1)
        m_old = m_ref[0:1, :]
        m_new = jnp.maximum(m_old, jnp.max(s, axis=0, keepdims=True))
        p = jnp.exp2((s - m_new).astype(BF16))
        acc_ref[...] = (jnp.exp2(m_old - m_new) * acc_ref[...]
                        + jnp.dot(v, p, preferred_element_type=F32))
        m_ref[...] = jnp.broadcast_to(m_new, m_ref.shape)

    def bias(quad, live, n=SEL_FAR):
        tiles = []
        for k in range(n):
            near = jnp.clip(quad * SEL_FAR + k - cq + SEL_NEAR, 0, SEL_NEAR + 1)
            tiles.append(ts_ref[jnp.where(live, near, SEL_NEAR + 1)])
        return jnp.concatenate(tiles, axis=0)

    def finish_n(near0_ref, n):
        s1 = raw_scores(qq, n)
        update(near0_ref[...] + bias(qq - 1, qq >= 1), jnp.maximum(qq - 1, 0))
        update(s1 + bias(qq, True, n), qq, n)
        out = acc_ref[:HEAD_DIM] / jnp.maximum(acc_ref[HEAD_DIM:HEAD_DIM + 1], 1e-30)
        for c in range(SEL_PAIR):
            o_ref[c] = out[:, c * QG:(c + 1) * QG]

    def finish(near0_ref):
        for n in range(1, SEL_FAR + 1):
            @pl.when(cq % SEL_FAR == n - 1)
            def _():
                finish_n(near0_ref, n)

    n_far = jnp.maximum(qq - 1, 0)
    sa_ref[...] = raw_scores(0)

    def far_pair(i, carry):
        sb_ref[...] = raw_scores(2 * i + 1)
        update(sa_ref[...], 2 * i)
        sa_ref[...] = raw_scores(2 * i + 2)
        update(sb_ref[...], 2 * i + 1)
        return carry

    lax.fori_loop(0, n_far // 2, far_pair, 0)

    @pl.when(n_far % 2 == 1)
    def _():
        sb_ref[...] = raw_scores(n_far)
        update(sa_ref[...], n_far - 1)
        finish(sb_ref)

    @pl.when(n_far % 2 == 0)
    def _():
        finish(sa_ref)


def _sel_attention(qt, mbt, k_sel, onehot, vt, ts, *, B, S):
    nq = S // Q_BLOCK
    n_sel = mbt.shape[2]
    nc = S // SEL_CHUNK
    width = SEL_PAIR * QG
    once = pl.Buffered(1)
    return pl.pallas_call(
        _sel_kernel,
        grid=(N_KV, B, nc),
        in_specs=[pl.BlockSpec((GROUP * HEAD_DIM, SEL_CHUNK), lambda h, b, i: (h, b * nc + i)),
                  pl.BlockSpec((None, None, n_sel, SEL_CHUNK), lambda h, b, i: (h, b, 0, i)),
                  pl.BlockSpec((None, None, nc, SEL_CHUNK, HEAD_DIM), lambda h, b, i: (h, b, 0, 0, 0)),
                  pl.BlockSpec(onehot.shape, lambda h, b, i: (0, 0, 0), pipeline_mode=once),
                  pl.BlockSpec((None, None, nc, V_ROWS, SEL_CHUNK), lambda h, b, i: (h, b, 0, 0, 0)),
                  pl.BlockSpec((None, SEL_NEAR + 2, SEL_CHUNK, width), lambda h, b, i: (h, 0, 0, 0),
                               pipeline_mode=once)],
        out_specs=pl.BlockSpec((None, SEL_PAIR, HEAD_DIM, QG), lambda h, b, i: (h, b * nc + i, 0, 0)),
        out_shape=jax.ShapeDtypeStruct((N_KV, B * nq, HEAD_DIM, QG), F32),
        scratch_shapes=[pltpu.VMEM((8, width), F32), pltpu.VMEM((V_ROWS, width), F32),
                        pltpu.VMEM((SEL_FAR * SEL_CHUNK, width), F32),
                        pltpu.VMEM((SEL_FAR * SEL_CHUNK, width), F32)],
        compiler_params=_cparams(("parallel", "parallel", "arbitrary")),
        name="nsa_selected",
    )(qt, mbt, k_sel, onehot, vt, ts)


def _win_kernel(*refs):
    nk = WINDOW // Q_BLOCK + WIN_PAIR
    q_ref = refs[0]
    k_refs = refs[1:1 + nk]
    v_refs = refs[1 + nk:1 + 2 * nk]
    tw_ref, gate_ref, oc_ref, os_ref, o_ref = refs[1 + 2 * nk:]
    first = WIN_PAIR * pl.program_id(2) - WINDOW // Q_BLOCK
    qt = jnp.concatenate([q_ref[g * HEAD_DIM:(g + 1) * HEAD_DIM, c * Q_BLOCK:(c + 1) * Q_BLOCK]
                          for c in range(WIN_PAIR) for g in range(GROUP)], axis=1)
    parts = []
    for i in range(nk):
        s = jnp.dot(k_refs[i][...], qt, preferred_element_type=F32)
        s = s + tw_ref[i * Q_BLOCK:(i + 1) * Q_BLOCK, :]
        parts.append(jnp.where(first + i >= 0, s, NEG))
    s = jnp.concatenate(parts, axis=0)
    m = jnp.max(s, axis=0, keepdims=True)
    pb = jnp.exp2((s - m).astype(BF16))
    ow = jnp.dot(v_refs[0][...], pb[0:Q_BLOCK], preferred_element_type=F32)
    for i in range(1, nk):
        ow = ow + jnp.dot(v_refs[i][...], pb[i * Q_BLOCK:(i + 1) * Q_BLOCK], preferred_element_type=F32)
    ow = ow[:HEAD_DIM] / jnp.maximum(ow[HEAD_DIM:HEAD_DIM + 1], 1e-30)

    def gate(j):
        return jnp.concatenate([gate_ref[j, g:g + 1, c * Q_BLOCK:(c + 1) * Q_BLOCK]
                                for c in range(WIN_PAIR) for g in range(GROUP)], axis=1)

    def both(ref):
        return jnp.concatenate([ref[c] for c in range(WIN_PAIR)], axis=1)

    o = gate(0) * both(oc_ref) + gate(1) * both(os_ref) + gate(2) * ow
    for c in range(WIN_PAIR):
        for g in range(GROUP):
            col = c * QG + g * Q_BLOCK
            o_ref[c * Q_BLOCK:(c + 1) * Q_BLOCK, g * HEAD_DIM:(g + 1) * HEAD_DIM] = (
                o[:, col:col + Q_BLOCK].T.astype(BF16))


def _win_attention(qt, kw, vwt, tw, gates_t, oc, osel, *, B, S):
    nq = S // Q_BLOCK
    npair = nq // WIN_PAIR
    back = WINDOW // Q_BLOCK
    nk = back + WIN_PAIR

    def kmap(j):
        return lambda h, b, i: (N_KV + h, b * nq + jnp.maximum(WIN_PAIR * i - back + j, 0), 0)

    def vmap_(j):
        return lambda h, b, i: (h, 0, b * nq + jnp.maximum(WIN_PAIR * i - back + j, 0))

    blk = pl.BlockSpec((None, WIN_PAIR, HEAD_DIM, QG), lambda h, b, i: (h, b * npair + i, 0, 0))
    return pl.pallas_call(
        _win_kernel,
        grid=(N_KV, B, npair),
        in_specs=([pl.BlockSpec((GROUP * HEAD_DIM, WIN_PAIR * Q_BLOCK), lambda h, b, i: (h, b * npair + i))]
                  + [pl.BlockSpec((None, Q_BLOCK, HEAD_DIM), kmap(j)) for j in range(nk)]
                  + [pl.BlockSpec((None, V_ROWS, Q_BLOCK), vmap_(j)) for j in range(nk)]
                  + [pl.BlockSpec((None, nk * Q_BLOCK, WIN_PAIR * QG), lambda h, b, i: (h, 0, 0)),
                     pl.BlockSpec((3, None, GROUP, WIN_PAIR * Q_BLOCK),
                                  lambda h, b, i: (0, h, 0, b * npair + i)),
                     blk, blk]),
        out_specs=pl.BlockSpec((WIN_PAIR * Q_BLOCK, GROUP * HEAD_DIM), lambda h, b, i: (b * npair + i, h)),
        out_shape=jax.ShapeDtypeStruct((B * S, N_HEADS * HEAD_DIM), BF16),
        compiler_params=_cparams(("parallel", "parallel", "parallel")),
        name="nsa_window_merge",
    )(qt, *([kw] * nk), *([vwt] * nk), tw, gates_t, oc, osel)


def _nsa_layer(x32, xb, p, tables, consts, ln_g, ln_b, *, B, S):
    T = B * S
    tc, ts, tw = tables
    overlap_t, sel_onehot = consts
    kd = N_KV * HEAD_DIM
    qv = _proj_nt(p["w_qv_t"], xb, out_dtype=BF16, scale=HEAD_DIM ** -0.5 * LOG2E,
                  n_scaled=N_HEADS * HEAD_DIM // 512)
    qt = qv
    k2 = _proj_nn(xb, p["w_k"], out_dtype=BF16, head_major=True)
    cmp = _proj_nn(xb, p["w_cmp"], out_dtype=F32, head_major=True)
    gates_t = _proj_nt(p["w_gate_t"], xb, out_dtype=F32, act="sigmoid")
    gates_t = gates_t.reshape(3, N_KV, GROUP, T)

    cmp = cmp.reshape(2 * N_KV, B, S, HEAD_DIM)
    kc = _compress(cmp, 0, p["pe_k"], p["w1_k"], p["w2_k"], B=B, S=S, transposed=False)
    vct = _compress(cmp, N_KV, p["pe_v"], p["w1_v"], p["w2_v"], B=B, S=S, transposed=True)
    oc, mbt = _cmp_attention(qt, kc, vct, tc, overlap_t, B=B, S=S)
    if mbt.shape[2] % LANE:
        mbt = jnp.pad(mbt, ((0, 0), (0, 0), (0, -mbt.shape[2] % LANE), (0, 0)), constant_values=NEG)

    nc = S // SEL_CHUNK
    k_sel = k2.reshape(2 * N_KV, B, nc, SEL_CHUNK, HEAD_DIM)
    v_sel_t = qv[N_HEADS * HEAD_DIM:N_HEADS * HEAD_DIM + kd].reshape(N_KV, HEAD_DIM, B, nc, SEL_CHUNK)
    v_sel_t = v_sel_t.transpose(0, 2, 3, 1, 4)
    ones_rows = (jnp.arange(V_ROWS - HEAD_DIM) == 0).astype(BF16)[:, None]
    v_sel_t = jnp.concatenate(
        [v_sel_t, jnp.broadcast_to(ones_rows, v_sel_t.shape[:3] + (V_ROWS - HEAD_DIM, SEL_CHUNK))], axis=3)
    osel = _sel_attention(qt, mbt, k_sel, sel_onehot, v_sel_t, ts, B=B, S=S)

    v_win_t = qv[N_HEADS * HEAD_DIM + kd:].reshape(N_KV, HEAD_DIM, T)
    v_win_t = jnp.concatenate(
        [v_win_t, jnp.broadcast_to(ones_rows, (N_KV, V_ROWS - HEAD_DIM, T))], axis=1)
    attn = _win_attention(qt, k2, v_win_t, tw, gates_t, oc, osel, B=B, S=S)
    return _proj_ln(attn, p["w_out"], x32, ln_g, ln_b, *p["router"])


def _nsa_params(w_in, pe_k, w1_k, w2_k, pe_v, w1_v, w2_v, w_out):
    D = w_in.shape[0]
    qd, kd = N_HEADS * HEAD_DIM, N_KV * HEAD_DIM
    w_kv = w_in[:, qd:qd + 6 * kd].reshape(D, 6, kd)
    gcol = np.array([[[(kv * GROUP + g) * 3 + j for g in range(GROUP)] for kv in range(N_KV)]
                     for j in range(3)]).reshape(-1)
    return dict(
        w_qv_t=jnp.concatenate([w_in[:, :qd], w_kv[:, 3], w_kv[:, 5]], axis=1).T.astype(BF16),
        w_k=jnp.concatenate([w_kv[:, 2], w_kv[:, 4]], axis=1).astype(BF16),
        w_cmp=jnp.concatenate([w_kv[:, 0], w_kv[:, 1]], axis=1).astype(BF16),
        w_gate_t=w_in[:, qd + 6 * kd:][:, gcol].T.astype(BF16),
        pe_k=pe_k, w1_k=w1_k.astype(BF16), w2_k=w2_k.astype(BF16),
        pe_v=pe_v, w1_v=w1_v.astype(BF16), w2_v=w2_v.astype(BF16),
        w_out=w_out.astype(BF16))


def _nsa_constants(S):
    n_cmp, n_sel = S // CMP_STRIDE, S // SEL_LEN
    cj = np.arange(n_cmp) * CMP_STRIDE
    sb = np.arange(n_sel) * SEL_LEN
    ov = np.maximum(np.minimum(cj[None, :] + CMP_LEN, sb[:, None] + SEL_LEN)
                    - np.maximum(cj[None, :], sb[:, None]), 0).astype(np.float32) / CMP_LEN
    ov[:, n_cmp - 1] = 0.0
    key = np.arange(SEL_CHUNK)
    chunk = np.arange(min(S // SEL_CHUNK, LANE * SEL_LEN // SEL_CHUNK))
    blk = (chunk[:, None] * (SEL_CHUNK // SEL_LEN) + key[None, :] // SEL_LEN) % LANE
    onehot = (blk[:, :, None] == np.arange(LANE)[None, None, :]).astype(np.float32)
    return jnp.asarray(ov, BF16), jnp.asarray(onehot, BF16)


def kernel(x, rel_bias, router_w, router_b, ln1_g, ln1_b, ln2_g, ln2_b, lru_w_in, lru_conv_w,
           lru_conv_b, lru_w_a, lru_b_a, lru_w_i, lru_b_i, lru_lambda, lru_w_out, nsa_w_in,
           nsa_pe_k, nsa_w1_k, nsa_w2_k, nsa_pe_v, nsa_w1_v, nsa_w2_v, nsa_w_out,
           moe_w_gate, moe_w_up, moe_w_down):
    B, S, D = x.shape
    T = B * S
    assert D == D_MODEL and S % (CMP_TILE * CMP_STRIDE) == 0 and S // SEL_LEN >= SEL_TOP
    qd, kd = N_HEADS * HEAD_DIM, N_KV * HEAD_DIM
    tables = _bias_tables(rel_bias, S)
    consts = _nsa_constants(S)
    router = _router_params(router_w, router_b)
    moe_bf16 = [w.astype(BF16) for w in (moe_w_gate, moe_w_up, moe_w_down)]

    x32 = x.reshape(T, D)
    xb = x32.astype(BF16)
    for layer in range(DEPTH):
        j = layer // 2
        if layer % 2 == 0:
            w_in = lru_w_in[j].astype(BF16)
            p = dict(w_gate=w_in[:, :D], w_y=w_in[:, D:], conv_w=lru_conv_w[j], conv_b=lru_conv_b[j],
                     w_a=lru_w_a[j].astype(BF16), b_a=lru_b_a[j], w_i=lru_w_i[j].astype(BF16),
                     b_i=lru_b_i[j], lam=lru_lambda[j], w_out=lru_w_out[j].astype(BF16), router=router)
            x32, xb, logits = _rglru_layer(x32, xb, p, ln1_g[layer], ln1_b[layer], B=B, S=S)
        else:
            p = _nsa_params(nsa_w_in[j], nsa_pe_k[j], nsa_w1_k[j], nsa_w2_k[j], nsa_pe_v[j],
                            nsa_w1_v[j], nsa_w2_v[j], nsa_w_out[j])
            p["router"] = router
            x32, xb, logits = _nsa_layer(x32, xb, p, tables, consts, ln1_g[layer], ln1_b[layer], B=B, S=S)
        x32, xb = _moe_layer(x32, logits, moe_bf16[0], moe_bf16[1], moe_bf16[2], layer,
                             ln2_g[layer], ln2_b[layer])
    return x32.reshape(B, S, D)
```

```python
import functools
import math

import numpy as np
import jax
import jax.numpy as jnp
from jax import lax
from jax.experimental import pallas as pl
from jax.experimental.pallas import tpu as pltpu

F32 = jnp.float32
BF16 = jnp.bfloat16

D_MODEL = 2048
DEPTH = 4
LRU_BLOCK = 256
LRU_NBLK = D_MODEL // LRU_BLOCK
CONV_W = 4
LRU_C = 8.0
HEAD_DIM = 128
N_HEADS = 16
N_KV = 4
GROUP = 4
CMP_LEN = 32
CMP_STRIDE = 16
SEL_LEN = 64
SEL_TOP = 16
WINDOW = 512
Q_BLOCK = 128
NUM_BUCKETS = 32
MAX_DISTANCE = 1024
N_EXPERTS = 16
N_GROUPS = 4
EXPERTS_PER_GROUP = 4
D_EXPERT = 1024
ALPHA = (2 * DEPTH) ** 0.25
LN_EPS = 1e-5
NEG = -1e30

LANE = 128
QG = GROUP * Q_BLOCK
EXPERT_ROWS = 256
SEL_CHUNK = 256
SEL_NEAR = 5
SEL_FAR = 4
V_ROWS = HEAD_DIM + 16
SEL_PAIR = SEL_CHUNK // Q_BLOCK
WIN_PAIR = 2
CMP_PAIR = 2
LOG2E = math.log2(math.e)
CMP_TILE = 128
CMP_NEAR = 23
VMEM_LIMIT = 56 << 20


def _cparams(sem, vmem=VMEM_LIMIT):
    return pltpu.CompilerParams(dimension_semantics=sem, vmem_limit_bytes=vmem)


def _act(x, act):
    if act == "gelu":
        return jax.nn.gelu(x)
    if act == "sigmoid":
        return jax.nn.sigmoid(x)
    return x


def _proj_nn_kernel(x_ref, w_ref, o_ref, *, act, head_major):
    acc = jnp.dot(x_ref[...], w_ref[...], preferred_element_type=F32)
    acc = _act(acc, act)
    if head_major:
        for h in range(o_ref.shape[0]):
            o_ref[h] = acc[:, h * LANE:(h + 1) * LANE].astype(o_ref.dtype)
    else:
        o_ref[...] = acc.astype(o_ref.dtype)


def _proj_nn(x, w, *, out_dtype, act=None, head_major=False, tm=1024, tn=1024):
    T, K = x.shape
    N = w.shape[1]
    tm, tn = min(tm, T), min(tn, N)
    if head_major:
        out_shape = (N // LANE, T, LANE)
        out_spec = pl.BlockSpec((tn // LANE, tm, LANE), lambda j, i: (j, i, 0))
    else:
        out_shape = (T, N)
        out_spec = pl.BlockSpec((tm, tn), lambda j, i: (i, j))
    return pl.pallas_call(
        functools.partial(_proj_nn_kernel, act=act, head_major=head_major),
        grid=(N // tn, T // tm),
        in_specs=[pl.BlockSpec((tm, K), lambda j, i: (i, 0)),
                  pl.BlockSpec((K, tn), lambda j, i: (0, j))],
        out_specs=out_spec,
        out_shape=jax.ShapeDtypeStruct(out_shape, out_dtype),
        compiler_params=_cparams(("parallel", "parallel")),
        name="proj_nn",
    )(x, w)


def _proj_nt_kernel(wt_ref, x_ref, o_ref, *, act, scale, n_scaled):
    acc = lax.dot_general(wt_ref[...], x_ref[...], (((1,), (1,)), ((), ())),
                          preferred_element_type=F32)
    if n_scaled:
        acc = acc * jnp.where(pl.program_id(0) < n_scaled, scale, 1.0).astype(F32)
    o_ref[...] = _act(acc, act).astype(o_ref.dtype)


def _proj_nt(wt, x, *, out_dtype, act=None, scale=1.0, n_scaled=0, tm=1024, tn=512):
    N, K = wt.shape
    T = x.shape[0]
    tm, tn = min(tm, T), min(tn, N)
    return pl.pallas_call(
        functools.partial(_proj_nt_kernel, act=act, scale=scale, n_scaled=n_scaled),
        grid=(N // tn, T // tm),
        in_specs=[pl.BlockSpec((tn, K), lambda j, i: (j, 0)),
                  pl.BlockSpec((tm, K), lambda j, i: (i, 0))],
        out_specs=pl.BlockSpec((tn, tm), lambda j, i: (j, i)),
        out_shape=jax.ShapeDtypeStruct((N, T), out_dtype),
        compiler_params=_cparams(("parallel", "parallel")),
        name="proj_nt",
    )(wt, x)


def _layer_norm_rows(z, g, b):
    mu = jnp.mean(z, axis=-1, keepdims=True)
    zc = z - mu
    var = jnp.mean(zc * zc, axis=-1, keepdims=True)
    return zc * lax.rsqrt(var + LN_EPS) * g + b


def _proj_ln_kernel(a_ref, w_ref, res_ref, g_ref, b_ref, rw_ref, rb_ref, of_ref, ob_ref, lg_ref):
    acc = jnp.dot(a_ref[...], w_ref[...], preferred_element_type=F32)
    y = _layer_norm_rows(ALPHA * res_ref[...] + acc, g_ref[...], b_ref[...])
    of_ref[...] = y
    y_hi = y.astype(BF16)
    ob_ref[...] = y_hi
    y_lo = (y - y_hi.astype(F32)).astype(BF16)
    rows = y.shape[0]
    both = jnp.dot(jnp.concatenate([y_hi, y_lo], axis=0), rw_ref[0], preferred_element_type=F32)
    lg_ref[...] = (both[:rows] + both[rows:] + jnp.dot(y_hi, rw_ref[1], preferred_element_type=F32)
                   + rb_ref[...])


def _proj_ln(a, w, res, g, b, router_w, router_b, *, tm=256):
    T, K = a.shape
    D = w.shape[1]
    tm = min(tm, T)
    row = pl.BlockSpec((tm, D), lambda i: (i, 0))
    vec = pl.BlockSpec((1, D), lambda i: (0, 0))
    lgt = pl.BlockSpec((tm, LANE), lambda i: (i, 0))
    return pl.pallas_call(
        _proj_ln_kernel,
        grid=(T // tm,),
        in_specs=[pl.BlockSpec((tm, K), lambda i: (i, 0)),
                  pl.BlockSpec((K, D), lambda i: (0, 0)),
                  row, vec, vec,
                  pl.BlockSpec((2, D, LANE), lambda i: (0, 0, 0)),
                  pl.BlockSpec((1, LANE), lambda i: (0, 0))],
        out_specs=[row, row, lgt],
        out_shape=[jax.ShapeDtypeStruct((T, D), F32), jax.ShapeDtypeStruct((T, D), BF16),
                   jax.ShapeDtypeStruct((T, LANE), F32)],
        compiler_params=_cparams(("parallel",)),
        name="proj_ln",
    )(a, w, res, g.reshape(1, D), b.reshape(1, D), router_w, router_b)


def _lru_kernel(y_ref, gate_ref, cw_ref, cb_ref, wa_ref, ba_ref, wi_ref, bi_ref, lam_ref,
                o_ref, h_ref, tail_ref, *, rows):
    @pl.when(pl.program_id(2) == 0)
    def _():
        h_ref[...] = jnp.zeros_like(h_ref)
        tail_ref[...] = jnp.zeros_like(tail_ref)

    y = y_ref[...]
    tail = tail_ref[...]
    cw = cw_ref[...]
    row8 = lax.broadcasted_iota(jnp.int32, (8, LRU_BLOCK), 0)
    yc = cb_ref[...] + y * cw[CONV_W - 1:CONV_W, :]
    for d in range(1, CONV_W):
        ys = pltpu.roll(y, d, 0)
        first = jnp.where(row8 < d, pltpu.roll(tail, d, 0), ys[:8])
        ys = jnp.concatenate([first, ys[8:]], axis=0)
        yc = yc + ys * cw[CONV_W - 1 - d:CONV_W - d, :]
    tail_ref[...] = y[rows - 8:]

    yb = yc.astype(BF16)
    r = jax.nn.sigmoid(jnp.dot(yb, wa_ref[...], preferred_element_type=F32) + ba_ref[...])
    ig = jax.nn.sigmoid(jnp.dot(yb, wi_ref[...], preferred_element_type=F32) + bi_ref[...])
    z = -lam_ref[...]
    softplus = jnp.maximum(z, 0.0) + jnp.log1p(jnp.exp(-jnp.abs(z)))
    log_a = (-LRU_C * softplus) * r
    a = jnp.exp(log_a)
    h = jnp.sqrt(1.0 - a * a) * (ig * yc)

    rowi = lax.broadcasted_iota(jnp.int32, (rows, LRU_BLOCK), 0)
    s = 1
    while s < rows:
        if s < 8:
            keep = rowi >= s
            h = jnp.where(keep, h + a * pltpu.roll(h, s, 0), h)
            a = jnp.where(keep, a * pltpu.roll(a, s, 0), a)
        else:
            h = jnp.concatenate([h[:s], h[s:] + a[s:] * h[:rows - s]], axis=0)
            a = jnp.concatenate([a[:s], a[s:] * a[:rows - s]], axis=0)
        s *= 2
    h = h + a * h_ref[0:1, :]
    h_ref[...] = jnp.broadcast_to(h[rows - 1:rows, :], h_ref.shape)
    o_ref[...] = (h * gate_ref[...].astype(F32)).astype(BF16)


def _lru_core(y, gate, conv_w, conv_b, w_a, b_a, w_i, b_i, lam, *, B, S, rows=512):
    T, W = y.shape
    rows = min(rows, S)
    ns = S // rows
    blk = pl.BlockSpec((rows, LRU_BLOCK), lambda n, b, s: (b * ns + s, n))
    vec = pl.BlockSpec((1, LRU_BLOCK), lambda n, b, s: (0, n))
    wblk = pl.BlockSpec((None, LRU_BLOCK, LRU_BLOCK), lambda n, b, s: (n, 0, 0))
    return pl.pallas_call(
        functools.partial(_lru_kernel, rows=rows),
        grid=(LRU_NBLK, B, ns),
        in_specs=[blk, blk,
                  pl.BlockSpec((CONV_W, LRU_BLOCK), lambda n, b, s: (0, n)), vec,
                  wblk, vec, wblk, vec, vec],
        out_specs=blk,
        out_shape=jax.ShapeDtypeStruct((T, W), BF16),
        scratch_shapes=[pltpu.VMEM((8, LRU_BLOCK), F32), pltpu.VMEM((8, LRU_BLOCK), F32)],
        compiler_params=_cparams(("parallel", "parallel", "arbitrary")),
        name="lru_core",
    )(y, gate, conv_w, conv_b.reshape(1, W), w_a, b_a.reshape(1, W), w_i, b_i.reshape(1, W),
      lam.reshape(1, W))


def _rglru_layer(x32, xb, p, ln_g, ln_b, *, B, S):
    gate = _proj_nn(xb, p["w_gate"], out_dtype=BF16, act="gelu")
    y = _proj_nn(xb, p["w_y"], out_dtype=F32)
    hg = _lru_core(y, gate, p["conv_w"], p["conv_b"], p["w_a"], p["b_a"], p["w_i"], p["b_i"],
                   p["lam"], B=B, S=S)
    return _proj_ln(hg, p["w_out"], x32, ln_g, ln_b, *p["router"])


def _router_kernel(lg_ref, ids_ref, wts_ref, pos_ref, cnt_ref, carry_ref, *, tm):
    @pl.when(pl.program_id(0) == 0)
    def _():
        carry_ref[...] = jnp.zeros_like(carry_ref)

    logits = lg_ref[...].T[:N_EXPERTS]
    m = jnp.max(logits, axis=0, keepdims=True)
    e = jnp.exp(logits - m)
    p = e / jnp.sum(e, axis=0, keepdims=True)
    rows = [p[k:k + 1, :] for k in range(N_EXPERTS)]

    scores = []
    for g in range(N_GROUPS):
        a, b, c, d = rows[4 * g:4 * g + 4]
        hi1, lo1, hi2, lo2 = jnp.maximum(a, b), jnp.minimum(a, b), jnp.maximum(c, d), jnp.minimum(c, d)
        scores.append(jnp.maximum(hi1, hi2) + jnp.maximum(jnp.minimum(hi1, hi2), jnp.maximum(lo1, lo2)))
    best, gsel = scores[0], jnp.zeros((1, tm), jnp.int32)
    for g in range(1, N_GROUPS):
        upd = scores[g] > best
        gsel = jnp.where(upd, g, gsel)
        best = jnp.where(upd, scores[g], best)

    cand = []
    for k in range(EXPERTS_PER_GROUP):
        c = rows[k]
        for g in range(1, N_GROUPS):
            c = jnp.where(gsel == g, rows[4 * g + k], c)
        cand.append(c)
    v1, i1 = cand[0], jnp.zeros((1, tm), jnp.int32)
    for k in range(1, EXPERTS_PER_GROUP):
        upd = cand[k] > v1
        i1 = jnp.where(upd, k, i1)
        v1 = jnp.where(upd, cand[k], v1)
    v2, i2 = jnp.full((1, tm), -1.0, F32), jnp.zeros((1, tm), jnp.int32)
    for k in range(EXPERTS_PER_GROUP):
        c = jnp.where(i1 == k, -1.0, cand[k])
        upd = c > v2
        i2 = jnp.where(upd, k, i2)
        v2 = jnp.where(upd, c, v2)
    e0 = gsel * EXPERTS_PER_GROUP + i1
    e1 = gsel * EXPERTS_PER_GROUP + i2
    tot = v1 + v2

    eio = lax.broadcasted_iota(jnp.int32, (N_EXPERTS, tm), 0)
    hit0, hit1 = eio == e0, eio == e1
    onehot = jnp.where(hit0 | hit1, 1.0, 0.0)
    before = (lax.broadcasted_iota(jnp.int32, (tm, tm), 0)
              < lax.broadcasted_iota(jnp.int32, (tm, tm), 1))
    rank = jnp.dot(onehot.astype(BF16), jnp.where(before, 1.0, 0.0).astype(BF16),
                   preferred_element_type=F32) + carry_ref[:, 0:1]
    pos0 = jnp.sum(jnp.where(hit0, rank, 0.0), axis=0, keepdims=True)
    pos1 = jnp.sum(jnp.where(hit1, rank, 0.0), axis=0, keepdims=True)
    carry_ref[...] = carry_ref[...] + jnp.sum(onehot, axis=1, keepdims=True)
    cnt_ref[...] = carry_ref[...]

    zi = jnp.zeros((6, tm), jnp.int32)
    ids_ref[...] = jnp.concatenate([e0, e1, zi], axis=0)
    pos_ref[...] = jnp.concatenate([pos0.astype(jnp.int32), pos1.astype(jnp.int32), zi], axis=0)
    wts_ref[...] = jnp.concatenate([v1 / tot, v2 / tot, jnp.zeros((6, tm), F32)], axis=0)


def _router(logits, *, tm=512):
    T = logits.shape[0]
    tm = min(tm, T)
    tok = pl.BlockSpec((8, tm), lambda i: (0, i))
    return pl.pallas_call(
        functools.partial(_router_kernel, tm=tm),
        grid=(T // tm,),
        in_specs=[pl.BlockSpec((tm, LANE), lambda i: (i, 0))],
        out_specs=[tok, tok, tok, pl.BlockSpec((N_EXPERTS, LANE), lambda i: (0, 0))],
        out_shape=[jax.ShapeDtypeStruct((8, T), jnp.int32), jax.ShapeDtypeStruct((8, T), F32),
                   jax.ShapeDtypeStruct((8, T), jnp.int32),
                   jax.ShapeDtypeStruct((N_EXPERTS, LANE), F32)],
        scratch_shapes=[pltpu.VMEM((N_EXPERTS, LANE), F32)],
        compiler_params=_cparams(("arbitrary",)),
        name="moe_router",
    )(logits)


def _router_params(router_w, router_b):
    pad = LANE - N_EXPERTS
    w = jnp.pad(router_w, ((0, 0), (0, pad)))
    w_hi = w.astype(BF16)
    w_lo = (w - w_hi.astype(F32)).astype(BF16)
    return (jnp.stack([w_hi, w_lo]), jnp.pad(router_b, (0, pad)).reshape(1, LANE))


def _row_copy(src, src_row, dst, dst_row, sem, n=1):
    return pltpu.make_async_copy(src.at[pl.ds(src_row, n)], dst.at[pl.ds(dst_row, n)], sem)


def _row_tokens_kernel(n_ref, lo_ref, hi_ref, d0_ref, d1_ref, tok_ref):
    step = 8

    def clear(r, c):
        tok_ref[r] = 0
        return c

    def put(i, c):
        for k in range(step):
            t = i * step + k
            tok_ref[d0_ref[t]] = t
            tok_ref[d1_ref[t]] = t
        return c

    for e in range(lo_ref.shape[0]):
        lax.fori_loop(lo_ref[e], hi_ref[e], clear, 0)
    lax.fori_loop(0, n_ref[0] // step, put, 0)


def _row_tokens(d0, d1, free_lo, free_hi, n_rows):
    n_tok = d0.shape[0]
    assert n_tok % 8 == 0
    return pl.pallas_call(
        _row_tokens_kernel,
        grid_spec=pltpu.PrefetchScalarGridSpec(
            num_scalar_prefetch=5, grid=(1,), in_specs=[],
            out_specs=pl.BlockSpec(memory_space=pltpu.SMEM)),
        out_shape=jax.ShapeDtypeStruct((n_rows,), jnp.int32),
        compiler_params=pltpu.CompilerParams(dimension_semantics=("arbitrary",)),
        name="moe_row_tokens",
    )(jnp.array([n_tok], jnp.int32), free_lo, free_hi, d0, d1)


def _ffn_kernel(be_ref, nact_ref, tok_ref, x_hbm, wg_ref, wu_ref, wd_ref, y_ref, xbuf, sems):
    i = pl.program_id(0)
    n_act = nact_ref[0]
    R = EXPERT_ROWS

    def gather(block, slot):
        def issue(r, c):
            _row_copy(x_hbm, tok_ref[block * R + r], xbuf.at[slot], r, sems.at[slot]).start()
            return c
        lax.fori_loop(0, R, issue, 0, unroll=8)

    def run(slot):
        @pl.when(i + 1 < n_act)
        def _():
            gather(i + 1, 1 - slot)

        def drain(r, c):
            _row_copy(x_hbm, 0, xbuf.at[slot], 0, sems.at[slot]).wait()
            return c
        lax.fori_loop(0, R, drain, 0, unroll=8)
        xb = xbuf[slot].astype(BF16)
        g = jnp.dot(xb, wg_ref[...], preferred_element_type=F32)
        u = jnp.dot(xb, wu_ref[...], preferred_element_type=F32)
        h = (jax.nn.silu(g) * u).astype(BF16)
        y_ref[...] = jnp.dot(h, wd_ref[...], preferred_element_type=F32)

    @pl.when(i == 0)
    def _():
        gather(0, 0)

    for slot in range(2):
        @pl.when((i < n_act) & (i % 2 == slot))
        def _():
            run(slot)

    @pl.when(i >= n_act)
    def _():
        y_ref[...] = jnp.zeros_like(y_ref)


def _expert_ffn(x32, row_tok, blk_e, n_act, w_gate, w_up, w_down, layer):
    D = x32.shape[1]
    R = EXPERT_ROWS
    n_rows = row_tok.shape[0]
    wmap = lambda i, be, na, tok: (layer, be[i], 0, 0)
    return pl.pallas_call(
        _ffn_kernel,
        grid_spec=pltpu.PrefetchScalarGridSpec(
            num_scalar_prefetch=3, grid=(n_rows // R,),
            in_specs=[pl.BlockSpec(memory_space=pl.ANY),
                      pl.BlockSpec((None, None, D, D_EXPERT), wmap),
                      pl.BlockSpec((None, None, D, D_EXPERT), wmap),
                      pl.BlockSpec((None, None, D_EXPERT, D), wmap)],
            out_specs=pl.BlockSpec((R, D), lambda i, be, na, tok: (i, 0)),
            scratch_shapes=[pltpu.VMEM((2, R, D), F32), pltpu.SemaphoreType.DMA((2,))]),
        out_shape=jax.ShapeDtypeStruct((n_rows, D), F32),
        compiler_params=_cparams(("arbitrary",)),
        name="moe_ffn",
    )(blk_e, n_act, row_tok, x32, w_gate, w_up, w_down)


def _combine_kernel(d0_ref, d1_ref, res_ref, w_ref, g_ref, b_ref, y_hbm, of_ref, ob_ref,
                    buf, sem, *, tm):
    i = pl.program_id(0)

    def issue(r, c):
        t = i * tm + r
        _row_copy(y_hbm, d0_ref[t], buf.at[0], r, sem).start()
        _row_copy(y_hbm, d1_ref[t], buf.at[1], r, sem).start()
        return c

    def drain(r, c):
        _row_copy(y_hbm, 0, buf.at[0], 0, sem).wait()
        _row_copy(y_hbm, 0, buf.at[1], 0, sem).wait()
        return c

    lax.fori_loop(0, tm, issue, 0)
    lax.fori_loop(0, tm, drain, 0)
    w = w_ref[...]
    z = ALPHA * res_ref[...] + (w[:, 0:1] * buf[0] + w[:, 1:2] * buf[1])
    y = _layer_norm_rows(z, g_ref[...], b_ref[...])
    of_ref[...] = y
    ob_ref[...] = y.astype(BF16)


def _combine_ln(y, d0, d1, wts, res, g, b, *, tm=256):
    T, D = res.shape
    tm = min(tm, T)
    row = pl.BlockSpec((tm, D), lambda i, a, c: (i, 0))
    vec = pl.BlockSpec((1, D), lambda i, a, c: (0, 0))
    return pl.pallas_call(
        functools.partial(_combine_kernel, tm=tm),
        grid_spec=pltpu.PrefetchScalarGridSpec(
            num_scalar_prefetch=2, grid=(T // tm,),
            in_specs=[row, pl.BlockSpec((tm, 2), lambda i, a, c: (i, 0)), vec, vec,
                      pl.BlockSpec(memory_space=pl.ANY)],
            out_specs=[row, row],
            scratch_shapes=[pltpu.VMEM((2, tm, D), F32), pltpu.SemaphoreType.DMA(())]),
        out_shape=[jax.ShapeDtypeStruct((T, D), F32), jax.ShapeDtypeStruct((T, D), BF16)],
        compiler_params=_cparams(("arbitrary",)),
        name="moe_combine_ln",
    )(d0, d1, res, wts, g.reshape(1, D), b.reshape(1, D), y)


def _moe_layer(x32, logits, w_gate, w_up, w_down, layer, ln_g, ln_b):
    T = x32.shape[0]
    R = EXPERT_ROWS
    ids, wts, pos, cnt = _router(logits)
    counts = cnt[:, 0].astype(jnp.int32)
    padded = (counts + R - 1) // R * R
    p_end = jnp.cumsum(padded)
    offs = p_end - padded
    d0 = offs[ids[0]] + pos[0]
    d1 = offs[ids[1]] + pos[1]
    n_blocks = 2 * T // R + N_EXPERTS
    n_act = p_end[-1] // R
    step = jnp.arange(n_blocks, dtype=jnp.int32)
    live = step < n_act
    src = jnp.where(live, step, n_act - 1)
    blk_e = jnp.minimum(jnp.sum(src[:, None] >= (p_end // R)[None, :], axis=1),
                        N_EXPERTS - 1).astype(jnp.int32)
    n_rows = n_blocks * R
    free_lo = jnp.concatenate([offs + counts, p_end[-1:]]).astype(jnp.int32)
    free_hi = jnp.concatenate([p_end, jnp.array([n_rows], p_end.dtype)]).astype(jnp.int32)
    row_tok = _row_tokens(d0, d1, free_lo, free_hi, n_rows)
    y = _expert_ffn(x32, row_tok, blk_e, n_act.reshape(1).astype(jnp.int32), w_gate, w_up, w_down, layer)
    return _combine_ln(y, d0, d1, wts[:2].T, x32, ln_g, ln_b)


def _t5_bucket(dist):
    n = jnp.maximum(dist, 0)
    max_exact = NUM_BUCKETS // 2
    nf = jnp.maximum(n, 1).astype(F32)
    large = max_exact + (jnp.log(nf / max_exact) / math.log(MAX_DISTANCE / max_exact)
                         * (NUM_BUCKETS - max_exact)).astype(jnp.int32)
    large = jnp.minimum(large, NUM_BUCKETS - 1)
    return jnp.where(n < max_exact, n, large)


def _stack_heads(tbl):
    parts = tbl.reshape((N_KV, GROUP) + tbl.shape[1:])
    return jnp.concatenate([parts[:, g] for g in range(GROUP)], axis=-1)


def _bias_tables(rel_bias, S):
    off = CMP_STRIDE * CMP_TILE + Q_BLOCK
    n_dist = CMP_NEAR * Q_BLOCK
    bvec = rel_bias[_t5_bucket(jnp.arange(n_dist))].T
    last = rel_bias[NUM_BUCKETS - 1]
    masked = jnp.full((N_HEADS, off), NEG, F32)
    w_abs = jnp.concatenate([masked, bvec * LOG2E], axis=1)
    w_rel = jnp.concatenate([masked, (bvec - last[:, None]) * LOG2E], axis=1)
    w_win = jnp.where(jnp.arange(-off, n_dist) < WINDOW, w_abs, NEG)

    def toeplitz(w, first, step, n):
        lo = first - step * (n - 1) + off
        span = step * (n - 1) + Q_BLOCK
        u = w[:, lo:lo + span]
        width = span + step
        tiled = jnp.tile(u, (1, -(-n * width // span)))[:, :n * width]
        return tiled.reshape(N_HEADS, n, width)[:, ::-1, :Q_BLOCK]

    def hankel(u, n, m, step):
        span = step * (n - 1) + m
        width = span + step
        u = u[..., :span, :]
        tiled = jnp.tile(u, (1,) * (u.ndim - 2) + (-(-n * width // span), 1))[..., :n * width, :]
        return tiled.reshape(u.shape[:-2] + (n, width, u.shape[-1]))[..., :m, :]

    per = Q_BLOCK // CMP_STRIDE
    n_seq = per * (CMP_NEAR - 1) + (per - 1) + CMP_TILE
    lo = off - (CMP_LEN - 1) - CMP_STRIDE * (CMP_TILE - 1)
    seq = w_abs[:, lo:lo + CMP_STRIDE * n_seq].reshape(N_HEADS, n_seq, CMP_STRIDE)
    by_tile = hankel(seq, CMP_NEAR, per - 1 + CMP_TILE, per)
    near_c = hankel(by_tile, per, CMP_TILE, 1)
    near_c = near_c[:, :, :, ::-1, :].transpose(0, 1, 3, 2, 4).reshape(N_HEADS, CMP_NEAR, CMP_TILE, Q_BLOCK)
    const = lambda v: jnp.broadcast_to(v, (N_HEADS, 1, CMP_TILE, Q_BLOCK)).astype(F32)
    tc = jnp.concatenate([const(NEG), near_c, const(last[:, None, None, None] * LOG2E)], axis=1)
    zero = jnp.zeros((N_HEADS, 1, SEL_CHUNK, Q_BLOCK), F32)
    ts = jnp.stack([
        jnp.concatenate(
            [zero] + [toeplitz(w_rel, Q_BLOCK * sub + SEL_CHUNK * (SEL_NEAR - 1 - i), 1, SEL_CHUNK)[:, None]
                      for i in range(SEL_NEAR)] + [zero + NEG], axis=1)
        for sub in range(SEL_CHUNK // Q_BLOCK)], axis=1)
    n_keys = WINDOW + WIN_PAIR * Q_BLOCK
    tw = jnp.concatenate([_stack_heads(toeplitz(w_win, WINDOW + Q_BLOCK * c, 1, n_keys))
                          for c in range(WIN_PAIR)], axis=-1)
    ts = _stack_heads(ts)
    ts = jnp.concatenate([ts[:, sub] for sub in range(SEL_PAIR)], axis=-1)
    return _stack_heads(tc), ts, tw


def _compress_kernel(x_ref, pe_ref, w1_ref, w2_ref, o_ref, *, transposed):
    n = x_ref.shape[0] // CMP_STRIDE
    a = b = None
    for tok in range(CMP_STRIDE):
        xt = x_ref[pl.ds(tok, n, stride=CMP_STRIDE), :]
        as_first = (xt + pe_ref[tok:tok + 1, :]).astype(BF16)
        as_second = (xt + pe_ref[CMP_STRIDE + tok:CMP_STRIDE + tok + 1, :]).astype(BF16)
        da = jnp.dot(as_first, w1_ref[tok * HEAD_DIM:(tok + 1) * HEAD_DIM], preferred_element_type=F32)
        db = jnp.dot(as_second, w1_ref[(CMP_STRIDE + tok) * HEAD_DIM:(CMP_STRIDE + tok + 1) * HEAD_DIM],
                     preferred_element_type=F32)
        a = da if a is None else a + da
        b = db if b is None else b + db
    pre = a + pltpu.roll(b, n - 1, 0)
    rowi = lax.broadcasted_iota(jnp.int32, pre.shape, 0)
    hid = jnp.where(rowi < n - 1, jax.nn.gelu(pre), 0.0).astype(BF16)
    if transposed:
        out = lax.dot_general(w2_ref[...], hid, (((1,), (1,)), ((), ())), preferred_element_type=F32)
    else:
        out = jnp.dot(hid, w2_ref[...], preferred_element_type=F32)
    o_ref[...] = out.astype(BF16)


def _compress(slabs, first, pe, w1, w2, *, B, S, transposed):
    n = S // CMP_STRIDE
    wide = CMP_STRIDE * HEAD_DIM
    out_shape = (N_KV, B, HEAD_DIM, n) if transposed else (N_KV, B, n, HEAD_DIM)
    w2_in = w2.T if transposed else w2
    return pl.pallas_call(
        functools.partial(_compress_kernel, transposed=transposed),
        grid=(N_KV, B),
        in_specs=[pl.BlockSpec((None, None, S, HEAD_DIM), lambda h, b: (first + h, b, 0, 0)),
                  pl.BlockSpec((CMP_LEN, HEAD_DIM), lambda h, b: (0, 0)),
                  pl.BlockSpec((2 * wide, HEAD_DIM), lambda h, b: (0, 0)),
                  pl.BlockSpec((HEAD_DIM, HEAD_DIM), lambda h, b: (0, 0))],
        out_specs=pl.BlockSpec((None, None) + out_shape[2:], lambda h, b: (h, b, 0, 0)),
        out_shape=jax.ShapeDtypeStruct(out_shape, BF16),
        compiler_params=_cparams(("parallel", "parallel")),
        name="nsa_compress",
    )(slabs, pe, w1, w2_in)


def _stacked_queries(q_ref, n_blocks):
    return jnp.concatenate([q_ref[g * HEAD_DIM:(g + 1) * HEAD_DIM, c * Q_BLOCK:(c + 1) * Q_BLOCK]
                            for c in range(n_blocks) for g in range(GROUP)], axis=1)


def _cmp_body(ib, n_tiles, q_ref, kc_ref, vct_ref, tc_ref, ov_ref, oc_ref, mb_ref):
    n_cmp = n_tiles * CMP_TILE
    n_sel = n_cmp * CMP_STRIDE // SEL_LEN
    nq = CMP_PAIR * Q_BLOCK
    qt = _stacked_queries(q_ref, CMP_PAIR)
    s = jnp.dot(kc_ref[:n_cmp], qt, preferred_element_type=F32)
    tiles = []
    for cb in range(n_tiles):
        bias = [tc_ref[jnp.clip(ib + c - (CMP_TILE * CMP_STRIDE // Q_BLOCK) * cb + 1, 0, CMP_NEAR + 1)]
                for c in range(CMP_PAIR)]
        tiles.append(s[cb * CMP_TILE:(cb + 1) * CMP_TILE] + jnp.concatenate(bias, axis=1))
    s = jnp.concatenate(tiles, axis=0)
    m = jnp.max(s, axis=0, keepdims=True)
    p = jnp.exp2(s - m)
    inv = jnp.where(m > 0.5 * NEG, 1.0 / jnp.maximum(jnp.sum(p, axis=0, keepdims=True), 1e-30), 0.0)
    p = p * inv
    oc = jnp.dot(vct_ref[:, :n_cmp], p.astype(BF16), preferred_element_type=F32)
    for c in range(CMP_PAIR):
        oc_ref[c] = oc[:, c * QG:(c + 1) * QG]
    ov = ov_ref[:n_sel, :n_cmp]

    sums = []
    for c in range(CMP_PAIR):
        acc = p[:, c * QG:c * QG + Q_BLOCK]
        for g in range(1, GROUP):
            acc = acc + p[:, c * QG + g * Q_BLOCK:c * QG + (g + 1) * Q_BLOCK]
        sums.append(acc)
    psum = jnp.concatenate(sums, axis=1)
    p_hi = psum.astype(BF16)
    p_lo = (psum - p_hi.astype(F32)).astype(BF16)
    imp = (jnp.dot(ov, p_hi, preferred_element_type=F32)
           + jnp.dot(ov, p_lo, preferred_element_type=F32))
    blk = lax.broadcasted_iota(jnp.int32, (n_sel, nq), 0)
    t = ib * Q_BLOCK + lax.broadcasted_iota(jnp.int32, (n_sel, nq), 1)
    cur = jnp.right_shift(t, int(math.log2(SEL_LEN)))
    ok = blk * SEL_LEN <= t
    forced = ok & ((blk == 0) | (blk == cur) | (blk == cur - 1))
    score = jnp.where(forced, -jnp.inf, jnp.where(ok, imp, NEG))
    taken = forced
    for _ in range(SEL_TOP - 3):
        top = jnp.max(score, axis=0, keepdims=True)
        first = jnp.min(jnp.where(score == top, blk, n_sel), axis=0, keepdims=True)
        pick = blk == first
        taken = taken | pick
        score = jnp.where(pick, -jnp.inf, score)
    mb_ref[:n_sel] = jnp.where(taken & ok, 0.0, NEG).astype(BF16)
    if n_sel < mb_ref.shape[0]:
        mb_ref[n_sel:] = jnp.full((mb_ref.shape[0] - n_sel, nq), NEG, BF16)


def _cmp_kernel(*refs, n_tiles):
    ib = CMP_PAIR * pl.program_id(2)
    per_tile = CMP_TILE * CMP_STRIDE // Q_BLOCK
    lo = 0
    for nt in range(1, n_tiles + 1):
        hi = nt * per_tile

        @pl.when((ib >= lo) & (ib < hi))
        def _():
            _cmp_body(ib, nt, *refs)
        lo = hi


def _cmp_attention(qt, kc, vct, tc, overlap_t, *, B, S):
    nq = S // Q_BLOCK
    npair = nq // CMP_PAIR
    n_cmp = S // CMP_STRIDE
    n_sel = S // SEL_LEN
    n_tiles = n_cmp // CMP_TILE
    return pl.pallas_call(
        functools.partial(_cmp_kernel, n_tiles=n_tiles),
        grid=(N_KV, B, npair),
        in_specs=[pl.BlockSpec((GROUP * HEAD_DIM, CMP_PAIR * Q_BLOCK), lambda h, b, i: (h, b * npair + i)),
                  pl.BlockSpec((None, None, n_cmp, HEAD_DIM), lambda h, b, i: (h, b, 0, 0)),
                  pl.BlockSpec((None, None, HEAD_DIM, n_cmp), lambda h, b, i: (h, b, 0, 0)),
                  pl.BlockSpec((None, CMP_NEAR + 2, CMP_TILE, QG), lambda h, b, i: (h, 0, 0, 0)),
                  pl.BlockSpec((n_sel, n_cmp), lambda h, b, i: (0, 0))],
        out_specs=[pl.BlockSpec((None, CMP_PAIR, HEAD_DIM, QG), lambda h, b, i: (h, b * npair + i, 0, 0)),
                   pl.BlockSpec((None, None, n_sel, CMP_PAIR * Q_BLOCK), lambda h, b, i: (h, b, 0, i))],
        out_shape=[jax.ShapeDtypeStruct((N_KV, B * nq, HEAD_DIM, QG), F32),
                   jax.ShapeDtypeStruct((N_KV, B, n_sel, S), BF16)],
        compiler_params=_cparams(("parallel", "parallel", "parallel")),
        name="nsa_cmp_select",
    )(qt, kc, vct, tc, overlap_t)


def _sel_kernel(q_ref, mb_ref, k_ref, oh_ref, vt_ref, ts_ref, o_ref, m_ref, acc_ref, sa_ref, sb_ref):
    cq = pl.program_id(2)
    qt = _stacked_queries(q_ref, SEL_PAIR)
    blocks_per_chunk = SEL_CHUNK // SEL_LEN
    chunks_per_group = LANE // blocks_per_chunk

    m_ref[...] = jnp.full(m_ref.shape, NEG, F32)
    acc_ref[...] = jnp.zeros_like(acc_ref)

    qq = cq // SEL_FAR

    def raw_scores(quad, n=SEL_FAR):
        rows = n * SEL_CHUNK
        c0 = pl.multiple_of(quad * SEL_FAR, SEL_FAR)
        grp = c0 // chunks_per_group
        mb = mb_ref[pl.ds(pl.multiple_of(grp * LANE, LANE), LANE), :]
        mask = jnp.concatenate([mb[:, c * Q_BLOCK:(c + 1) * Q_BLOCK]
                                for c in range(SEL_PAIR) for _ in range(GROUP)], axis=1)
        qa = jnp.concatenate([qt, mask], axis=0)
        o0 = pl.multiple_of(c0 % oh_ref.shape[0], SEL_FAR)
        keys = jnp.concatenate([k_ref[pl.ds(c0, n)].reshape(rows, HEAD_DIM),
                                oh_ref[pl.ds(o0, n)].reshape(rows, HEAD_DIM)], axis=1)
        return jnp.dot(keys, qa, preferred_element_type=F32)

    def update(s, quad, n=SEL_FAR):
        c0 = quad * SEL_FAR
        v = jnp.concatenate([vt_ref[c0 + k] for k in range(n)], axis=1)
        m_old = m_ref[0:1, :]
        m_new = jnp.maximum(m_old, jnp.max(s, axis=0, keepdims=True))
        p = jnp.exp2((s - m_new).astype(BF16))
        acc_ref[...] = (jnp.exp2(m_old - m_new) * acc_ref[...]
                        + jnp.dot(v, p, preferred_element_type=F32))
        m_ref[...] = jnp.broadcast_to(m_new, m_ref.shape)

    def bias(quad, live, n=SEL_FAR):
        tiles = []
        for k in range(n):
            near = jnp.clip(quad * SEL_FAR + k - cq + SEL_NEAR, 0, SEL_NEAR + 1)
            tiles.append(ts_ref[jnp.where(live, near, SEL_NEAR + 1)])
        return jnp.concatenate(tiles, axis=0)

    def finish_n(near0_ref, n):
        s1 = raw_scores(qq, n)
        update(near0_ref[...] + bias(qq - 1, qq >= 1), jnp.maximum(qq - 1, 0))
        update(s1 + bias(qq, True, n), qq, n)
        out = acc_ref[:HEAD_DIM] / jnp.maximum(acc_ref[HEAD_DIM:HEAD_DIM + 1], 1e-30)
        for c in range(SEL_PAIR):
            o_ref[c] = out[:, c * QG:(c + 1) * QG]

    def finish(near0_ref):
        for n in range(1, SEL_FAR + 1):
            @pl.when(cq % SEL_FAR == n - 1)
            def _():
                finish_n(near0_ref, n)

    n_far = jnp.maximum(qq - 1, 0)
    sa_ref[...] = raw_scores(0)

    def far_pair(i, carry):
        sb_ref[...] = raw_scores(2 * i + 1)
        update(sa_ref[...], 2 * i)
        sa_ref[...] = raw_scores(2 * i + 2)
        update(sb_ref[...], 2 * i + 1)
        return carry

    lax.fori_loop(0, n_far // 2, far_pair, 0)

    @pl.when(n_far % 2 == 1)
    def _():
        sb_ref[...] = raw_scores(n_far)
        update(sa_ref[...], n_far - 1)
        finish(sb_ref)

    @pl.when(n_far % 2 == 0)
    def _():
        finish(sa_ref)


def _sel_attention(qt, mbt, k_sel, onehot, vt, ts, *, B, S):
    nq = S // Q_BLOCK
    n_sel = mbt.shape[2]
    nc = S // SEL_CHUNK
    width = SEL_PAIR * QG
    once = pl.Buffered(1)
    return pl.pallas_call(
        _sel_kernel,
        grid=(N_KV, B, nc),
        in_specs=[pl.BlockSpec((GROUP * HEAD_DIM, SEL_CHUNK), lambda h, b, i: (h, b * nc + i)),
                  pl.BlockSpec((None, None, n_sel, SEL_CHUNK), lambda h, b, i: (h, b, 0, i)),
                  pl.BlockSpec((None, None, nc, SEL_CHUNK, HEAD_DIM), lambda h, b, i: (h, b, 0, 0, 0)),
                  pl.BlockSpec(onehot.shape, lambda h, b, i: (0, 0, 0), pipeline_mode=once),
                  pl.BlockSpec((None, None, nc, V_ROWS, SEL_CHUNK), lambda h, b, i: (h, b, 0, 0, 0)),
                  pl.BlockSpec((None, SEL_NEAR + 2, SEL_CHUNK, width), lambda h, b, i: (h, 0, 0, 0),
                               pipeline_mode=once)],
        out_specs=pl.BlockSpec((None, SEL_PAIR, HEAD_DIM, QG), lambda h, b, i: (h, b * nc + i, 0, 0)),
        out_shape=jax.ShapeDtypeStruct((N_KV, B * nq, HEAD_DIM, QG), F32),
        scratch_shapes=[pltpu.VMEM((8, width), F32), pltpu.VMEM((V_ROWS, width), F32),
                        pltpu.VMEM((SEL_FAR * SEL_CHUNK, width), F32),
                        pltpu.VMEM((SEL_FAR * SEL_CHUNK, width), F32)],
        compiler_params=_cparams(("parallel", "parallel", "arbitrary")),
        name="nsa_selected",
    )(qt, mbt, k_sel, onehot, vt, ts)


def _win_kernel(*refs):
    nk = WINDOW // Q_BLOCK + WIN_PAIR
    q_ref = refs[0]
    k_refs = refs[1:1 + nk]
    v_refs = refs[1 + nk:1 + 2 * nk]
    tw_ref, gate_ref, oc_ref, os_ref, o_ref = refs[1 + 2 * nk:]
    first = WIN_PAIR * pl.program_id(2) - WINDOW // Q_BLOCK
    qt = _stacked_queries(q_ref, WIN_PAIR)
    parts = []
    for i in range(nk):
        s = jnp.dot(k_refs[i][...], qt, preferred_element_type=F32)
        s = s + tw_ref[i * Q_BLOCK:(i + 1) * Q_BLOCK, :]
        parts.append(jnp.where(first + i >= 0, s, NEG))
    s = jnp.concatenate(parts, axis=0)
    m = jnp.max(s, axis=0, keepdims=True)
    pb = jnp.exp2((s - m).astype(BF16))
    ow = jnp.dot(v_refs[0][...], pb[0:Q_BLOCK], preferred_element_type=F32)
    for i in range(1, nk):
        ow = ow + jnp.dot(v_refs[i][...], pb[i * Q_BLOCK:(i + 1) * Q_BLOCK], preferred_element_type=F32)
    ow = ow[:HEAD_DIM] / jnp.maximum(ow[HEAD_DIM:HEAD_DIM + 1], 1e-30)

    def gate(j):
        return jnp.concatenate([gate_ref[j, g:g + 1, c * Q_BLOCK:(c + 1) * Q_BLOCK]
                                for c in range(WIN_PAIR) for g in range(GROUP)], axis=1)

    def both(ref):
        return jnp.concatenate([ref[c] for c in range(WIN_PAIR)], axis=1)

    o = gate(0) * both(oc_ref) + gate(1) * both(os_ref) + gate(2) * ow
    for c in range(WIN_PAIR):
        for g in range(GROUP):
            col = c * QG + g * Q_BLOCK
            o_ref[c * Q_BLOCK:(c + 1) * Q_BLOCK, g * HEAD_DIM:(g + 1) * HEAD_DIM] = (
                o[:, col:col + Q_BLOCK].T.astype(BF16))


def _win_attention(qt, kw, vwt, tw, gates_t, oc, osel, *, B, S):
    nq = S // Q_BLOCK
    npair = nq // WIN_PAIR
    back = WINDOW // Q_BLOCK
    nk = back + WIN_PAIR

    def kmap(j):
        return lambda h, b, i: (N_KV + h, b * nq + jnp.maximum(WIN_PAIR * i - back + j, 0), 0)

    def vmap_(j):
        return lambda h, b, i: (h, 0, b * nq + jnp.maximum(WIN_PAIR * i - back + j, 0))

    blk = pl.BlockSpec((None, WIN_PAIR, HEAD_DIM, QG), lambda h, b, i: (h, b * npair + i, 0, 0))
    return pl.pallas_call(
        _win_kernel,
        grid=(N_KV, B, npair),
        in_specs=([pl.BlockSpec((GROUP * HEAD_DIM, WIN_PAIR * Q_BLOCK), lambda h, b, i: (h, b * npair + i))]
                  + [pl.BlockSpec((None, Q_BLOCK, HEAD_DIM), kmap(j)) for j in range(nk)]
                  + [pl.BlockSpec((None, V_ROWS, Q_BLOCK), vmap_(j)) for j in range(nk)]
                  + [pl.BlockSpec((None, nk * Q_BLOCK, WIN_PAIR * QG), lambda h, b, i: (h, 0, 0)),
                     pl.BlockSpec((3, None, GROUP, WIN_PAIR * Q_BLOCK),
                                  lambda h, b, i: (0, h, 0, b * npair + i)),
                     blk, blk]),
        out_specs=pl.BlockSpec((WIN_PAIR * Q_BLOCK, GROUP * HEAD_DIM), lambda h, b, i: (b * npair + i, h)),
        out_shape=jax.ShapeDtypeStruct((B * S, N_HEADS * HEAD_DIM), BF16),
        compiler_params=_cparams(("parallel", "parallel", "parallel")),
        name="nsa_window_merge",
    )(qt, *([kw] * nk), *([vwt] * nk), tw, gates_t, oc, osel)


def _nsa_layer(x32, xb, p, tables, consts, ln_g, ln_b, *, B, S):
    T = B * S
    tc, ts, tw = tables
    overlap_t, sel_onehot = consts
    kd = N_KV * HEAD_DIM
    qv = _proj_nt(p["w_qv_t"], xb, out_dtype=BF16, scale=HEAD_DIM ** -0.5 * LOG2E,
                  n_scaled=N_HEADS * HEAD_DIM // 512)
    qt = qv
    k2 = _proj_nn(xb, p["w_k"], out_dtype=BF16, head_major=True)
    cmp = _proj_nn(xb, p["w_cmp"], out_dtype=F32, head_major=True)
    gates_t = _proj_nt(p["w_gate_t"], xb, out_dtype=F32, act="sigmoid")
    gates_t = gates_t.reshape(3, N_KV, GROUP, T)

    cmp = cmp.reshape(2 * N_KV, B, S, HEAD_DIM)
    kc = _compress(cmp, 0, p["pe_k"], p["w1_k"], p["w2_k"], B=B, S=S, transposed=False)
    vct = _compress(cmp, N_KV, p["pe_v"], p["w1_v"], p["w2_v"], B=B, S=S, transposed=True)
    oc, mbt = _cmp_attention(qt, kc, vct, tc, overlap_t, B=B, S=S)
    if mbt.shape[2] % LANE:
        mbt = jnp.pad(mbt, ((0, 0), (0, 0), (0, -mbt.shape[2] % LANE), (0, 0)), constant_values=NEG)

    nc = S // SEL_CHUNK
    k_sel = k2.reshape(2 * N_KV, B, nc, SEL_CHUNK, HEAD_DIM)
    v_sel_t = qv[N_HEADS * HEAD_DIM:N_HEADS * HEAD_DIM + kd].reshape(N_KV, HEAD_DIM, B, nc, SEL_CHUNK)
    v_sel_t = v_sel_t.transpose(0, 2, 3, 1, 4)
    ones_rows = (jnp.arange(V_ROWS - HEAD_DIM) == 0).astype(BF16)[:, None]
    v_sel_t = jnp.concatenate(
        [v_sel_t, jnp.broadcast_to(ones_rows, v_sel_t.shape[:3] + (V_ROWS - HEAD_DIM, SEL_CHUNK))], axis=3)
    osel = _sel_attention(qt, mbt, k_sel, sel_onehot, v_sel_t, ts, B=B, S=S)

    v_win_t = qv[N_HEADS * HEAD_DIM + kd:].reshape(N_KV, HEAD_DIM, T)
    v_win_t = jnp.concatenate(
        [v_win_t, jnp.broadcast_to(ones_rows, (N_KV, V_ROWS - HEAD_DIM, T))], axis=1)
    attn = _win_attention(qt, k2, v_win_t, tw, gates_t, oc, osel, B=B, S=S)
    return _proj_ln(attn, p["w_out"], x32, ln_g, ln_b, *p["router"])


def _nsa_params(w_in, pe_k, w1_k, w2_k, pe_v, w1_v, w2_v, w_out):
    D = w_in.shape[0]
    qd, kd = N_HEADS * HEAD_DIM, N_KV * HEAD_DIM
    w_kv = w_in[:, qd:qd + 6 * kd].reshape(D, 6, kd)
    gcol = np.array([[[(kv * GROUP + g) * 3 + j for g in range(GROUP)] for kv in range(N_KV)]
                     for j in range(3)]).reshape(-1)
    return dict(
        w_qv_t=jnp.concatenate([w_in[:, :qd], w_kv[:, 3], w_kv[:, 5]], axis=1).T.astype(BF16),
        w_k=jnp.concatenate([w_kv[:, 2], w_kv[:, 4]], axis=1).astype(BF16),
        w_cmp=jnp.concatenate([w_kv[:, 0], w_kv[:, 1]], axis=1).astype(BF16),
        w_gate_t=w_in[:, qd + 6 * kd:][:, gcol].T.astype(BF16),
        pe_k=pe_k, w1_k=w1_k.astype(BF16), w2_k=w2_k.astype(BF16),
        pe_v=pe_v, w1_v=w1_v.astype(BF16), w2_v=w2_v.astype(BF16),
        w_out=w_out.astype(BF16))


def _nsa_constants(S):
    n_cmp, n_sel = S // CMP_STRIDE, S // SEL_LEN
    cj = np.arange(n_cmp) * CMP_STRIDE
    sb = np.arange(n_sel) * SEL_LEN
    ov = np.maximum(np.minimum(cj[None, :] + CMP_LEN, sb[:, None] + SEL_LEN)
                    - np.maximum(cj[None, :], sb[:, None]), 0).astype(np.float32) / CMP_LEN
    ov[:, n_cmp - 1] = 0.0
    key = np.arange(SEL_CHUNK)
    chunk = np.arange(min(S // SEL_CHUNK, LANE * SEL_LEN // SEL_CHUNK))
    blk = (chunk[:, None] * (SEL_CHUNK // SEL_LEN) + key[None, :] // SEL_LEN) % LANE
    onehot = (blk[:, :, None] == np.arange(LANE)[None, None, :]).astype(np.float32)
    return jnp.asarray(ov, BF16), jnp.asarray(onehot, BF16)


def kernel(x, rel_bias, router_w, router_b, ln1_g, ln1_b, ln2_g, ln2_b, lru_w_in, lru_conv_w,
           lru_conv_b, lru_w_a, lru_b_a, lru_w_i, lru_b_i, lru_lambda, lru_w_out, nsa_w_in,
           nsa_pe_k, nsa_w1_k, nsa_w2_k, nsa_pe_v, nsa_w1_v, nsa_w2_v, nsa_w_out,
           moe_w_gate, moe_w_up, moe_w_down):
    B, S, D = x.shape
    T = B * S
    assert D == D_MODEL and S % (CMP_TILE * CMP_STRIDE) == 0 and S // SEL_LEN >= SEL_TOP
    tables = _bias_tables(rel_bias, S)
    consts = _nsa_constants(S)
    router = _router_params(router_w, router_b)
    moe_bf16 = [w.astype(BF16) for w in (moe_w_gate, moe_w_up, moe_w_down)]

    x32 = x.reshape(T, D)
    xb = x32.astype(BF16)
    for layer in range(DEPTH):
        j = layer // 2
        if layer % 2 == 0:
            w_in = lru_w_in[j].astype(BF16)
            p = dict(w_gate=w_in[:, :D], w_y=w_in[:, D:], conv_w=lru_conv_w[j], conv_b=lru_conv_b[j],
                     w_a=lru_w_a[j].astype(BF16), b_a=lru_b_a[j], w_i=lru_w_i[j].astype(BF16),
                     b_i=lru_b_i[j], lam=lru_lambda[j], w_out=lru_w_out[j].astype(BF16), router=router)
            x32, xb, logits = _rglru_layer(x32, xb, p, ln1_g[layer], ln1_b[layer], B=B, S=S)
        else:
            p = _nsa_params(nsa_w_in[j], nsa_pe_k[j], nsa_w1_k[j], nsa_w2_k[j], nsa_pe_v[j],
                            nsa_w1_v[j], nsa_w2_v[j], nsa_w_out[j])
            p["router"] = router
            x32, xb, logits = _nsa_layer(x32, xb, p, tables, consts, ln1_g[layer], ln1_b[layer], B=B, S=S)
        x32, xb = _moe_layer(x32, logits, moe_bf16[0], moe_bf16[1], moe_bf16[2], layer,
                             ln2_g[layer], ln2_b[layer])
    return x32.reshape(B, S, D)
```

```python
import functools
import math

import numpy as np
import jax
import jax.numpy as jnp
from jax import lax
from jax.experimental import pallas as pl
from jax.experimental.pallas import tpu as pltpu

F32 = jnp.float32
BF16 = jnp.bfloat16

D_MODEL = 2048
DEPTH = 4
LRU_BLOCK = 256
LRU_NBLK = D_MODEL // LRU_BLOCK
CONV_W = 4
LRU_C = 8.0
HEAD_DIM = 128
N_HEADS = 16
N_KV = 4
GROUP = 4
CMP_LEN = 32
CMP_STRIDE = 16
SEL_LEN = 64
SEL_TOP = 16
WINDOW = 512
Q_BLOCK = 128
NUM_BUCKETS = 32
MAX_DISTANCE = 1024
N_EXPERTS = 16
N_GROUPS = 4
EXPERTS_PER_GROUP = 4
D_EXPERT = 1024
ALPHA = (2 * DEPTH) ** 0.25
LN_EPS = 1e-5
NEG = -1e30

LANE = 128
QG = GROUP * Q_BLOCK
EXPERT_ROWS = 256
SEL_CHUNK = 256
SEL_NEAR = 5
SEL_FAR = 4
V_ROWS = HEAD_DIM + 16
SEL_PAIR = SEL_CHUNK // Q_BLOCK
WIN_PAIR = 2
CMP_PAIR = 4
LOG2E = math.log2(math.e)
CMP_TILE = 128
CMP_NEAR = 23
VMEM_LIMIT = 56 << 20


def _cparams(sem, vmem=VMEM_LIMIT):
    return pltpu.CompilerParams(dimension_semantics=sem, vmem_limit_bytes=vmem)


def _act(x, act):
    if act == "gelu":
        return jax.nn.gelu(x)
    if act == "sigmoid":
        return jax.nn.sigmoid(x)
    return x


def _proj_nn_kernel(x_ref, w_ref, o_ref, *, act, head_major):
    acc = jnp.dot(x_ref[...], w_ref[...], preferred_element_type=F32)
    acc = _act(acc, act)
    if head_major:
        for h in range(o_ref.shape[0]):
            o_ref[h] = acc[:, h * LANE:(h + 1) * LANE].astype(o_ref.dtype)
    else:
        o_ref[...] = acc.astype(o_ref.dtype)


def _proj_nn(x, w, *, out_dtype, act=None, head_major=False, tm=1024, tn=1024):
    T, K = x.shape
    N = w.shape[1]
    tm, tn = min(tm, T), min(tn, N)
    if head_major:
        out_shape = (N // LANE, T, LANE)
        out_spec = pl.BlockSpec((tn // LANE, tm, LANE), lambda j, i: (j, i, 0))
    else:
        out_shape = (T, N)
        out_spec = pl.BlockSpec((tm, tn), lambda j, i: (i, j))
    return pl.pallas_call(
        functools.partial(_proj_nn_kernel, act=act, head_major=head_major),
        grid=(N // tn, T // tm),
        in_specs=[pl.BlockSpec((tm, K), lambda j, i: (i, 0)),
                  pl.BlockSpec((K, tn), lambda j, i: (0, j))],
        out_specs=out_spec,
        out_shape=jax.ShapeDtypeStruct(out_shape, out_dtype),
        compiler_params=_cparams(("parallel", "parallel")),
        name="proj_nn",
    )(x, w)


def _proj_nt_kernel(wt_ref, x_ref, o_ref, *, act, scale, n_scaled):
    acc = lax.dot_general(wt_ref[...], x_ref[...], (((1,), (1,)), ((), ())),
                          preferred_element_type=F32)
    if n_scaled:
        acc = acc * jnp.where(pl.program_id(0) < n_scaled, scale, 1.0).astype(F32)
    o_ref[...] = _act(acc, act).astype(o_ref.dtype)


def _proj_nt(wt, x, *, out_dtype, act=None, scale=1.0, n_scaled=0, tm=1024, tn=512):
    N, K = wt.shape
    T = x.shape[0]
    tm, tn = min(tm, T), min(tn, N)
    return pl.pallas_call(
        functools.partial(_proj_nt_kernel, act=act, scale=scale, n_scaled=n_scaled),
        grid=(N // tn, T // tm),
        in_specs=[pl.BlockSpec((tn, K), lambda j, i: (j, 0)),
                  pl.BlockSpec((tm, K), lambda j, i: (i, 0))],
        out_specs=pl.BlockSpec((tn, tm), lambda j, i: (j, i)),
        out_shape=jax.ShapeDtypeStruct((N, T), out_dtype),
        compiler_params=_cparams(("parallel", "parallel")),
        name="proj_nt",
    )(wt, x)


def _layer_norm_rows(z, g, b):
    mu = jnp.mean(z, axis=-1, keepdims=True)
    zc = z - mu
    var = jnp.mean(zc * zc, axis=-1, keepdims=True)
    return zc * lax.rsqrt(var + LN_EPS) * g + b


def _proj_ln_kernel(a_ref, w_ref, res_ref, g_ref, b_ref, rw_ref, rb_ref, of_ref, ob_ref, lg_ref):
    acc = jnp.dot(a_ref[...], w_ref[...], preferred_element_type=F32)
    y = _layer_norm_rows(ALPHA * res_ref[...] + acc, g_ref[...], b_ref[...])
    of_ref[...] = y
    y_hi = y.astype(BF16)
    ob_ref[...] = y_hi
    y_lo = (y - y_hi.astype(F32)).astype(BF16)
    rows = y.shape[0]
    both = jnp.dot(jnp.concatenate([y_hi, y_lo], axis=0), rw_ref[0], preferred_element_type=F32)
    lg_ref[...] = (both[:rows] + both[rows:] + jnp.dot(y_hi, rw_ref[1], preferred_element_type=F32)
                   + rb_ref[...])


def _proj_ln(a, w, res, g, b, router_w, router_b, *, tm=256):
    T, K = a.shape
    D = w.shape[1]
    tm = min(tm, T)
    row = pl.BlockSpec((tm, D), lambda i: (i, 0))
    vec = pl.BlockSpec((1, D), lambda i: (0, 0))
    lgt = pl.BlockSpec((tm, LANE), lambda i: (i, 0))
    return pl.pallas_call(
        _proj_ln_kernel,
        grid=(T // tm,),
        in_specs=[pl.BlockSpec((tm, K), lambda i: (i, 0)),
                  pl.BlockSpec((K, D), lambda i: (0, 0)),
                  row, vec, vec,
                  pl.BlockSpec((2, D, LANE), lambda i: (0, 0, 0)),
                  pl.BlockSpec((1, LANE), lambda i: (0, 0))],
        out_specs=[row, row, lgt],
        out_shape=[jax.ShapeDtypeStruct((T, D), F32), jax.ShapeDtypeStruct((T, D), BF16),
                   jax.ShapeDtypeStruct((T, LANE), F32)],
        compiler_params=_cparams(("parallel",)),
        name="proj_ln",
    )(a, w, res, g.reshape(1, D), b.reshape(1, D), router_w, router_b)


def _lru_kernel(y_ref, gate_ref, cw_ref, cb_ref, wa_ref, ba_ref, wi_ref, bi_ref, lam_ref,
                o_ref, h_ref, tail_ref, *, rows):
    @pl.when(pl.program_id(2) == 0)
    def _():
        h_ref[...] = jnp.zeros_like(h_ref)
        tail_ref[...] = jnp.zeros_like(tail_ref)

    y = y_ref[...]
    tail = tail_ref[...]
    cw = cw_ref[...]
    row8 = lax.broadcasted_iota(jnp.int32, (8, LRU_BLOCK), 0)
    yc = cb_ref[...] + y * cw[CONV_W - 1:CONV_W, :]
    for d in range(1, CONV_W):
        ys = pltpu.roll(y, d, 0)
        first = jnp.where(row8 < d, pltpu.roll(tail, d, 0), ys[:8])
        ys = jnp.concatenate([first, ys[8:]], axis=0)
        yc = yc + ys * cw[CONV_W - 1 - d:CONV_W - d, :]
    tail_ref[...] = y[rows - 8:]

    yb = yc.astype(BF16)
    r = jax.nn.sigmoid(jnp.dot(yb, wa_ref[...], preferred_element_type=F32) + ba_ref[...])
    ig = jax.nn.sigmoid(jnp.dot(yb, wi_ref[...], preferred_element_type=F32) + bi_ref[...])
    z = -lam_ref[...]
    softplus = jnp.maximum(z, 0.0) + jnp.log1p(jnp.exp(-jnp.abs(z)))
    log_a = (-LRU_C * softplus) * r
    a = jnp.exp(log_a)
    h = jnp.sqrt(1.0 - a * a) * (ig * yc)

    rowi = lax.broadcasted_iota(jnp.int32, (rows, LRU_BLOCK), 0)
    s = 1
    while s < rows:
        if s < 8:
            keep = rowi >= s
            h = jnp.where(keep, h + a * pltpu.roll(h, s, 0), h)
            a = jnp.where(keep, a * pltpu.roll(a, s, 0), a)
        else:
            h = jnp.concatenate([h[:s], h[s:] + a[s:] * h[:rows - s]], axis=0)
            a = jnp.concatenate([a[:s], a[s:] * a[:rows - s]], axis=0)
        s *= 2
    h = h + a * h_ref[0:1, :]
    h_ref[...] = jnp.broadcast_to(h[rows - 1:rows, :], h_ref.shape)
    o_ref[...] = (h * gate_ref[...].astype(F32)).astype(BF16)


def _lru_core(y, gate, conv_w, conv_b, w_a, b_a, w_i, b_i, lam, *, B, S, rows=512):
    T, W = y.shape
    rows = min(rows, S)
    ns = S // rows
    blk = pl.BlockSpec((rows, LRU_BLOCK), lambda n, b, s: (b * ns + s, n))
    vec = pl.BlockSpec((1, LRU_BLOCK), lambda n, b, s: (0, n))
    wblk = pl.BlockSpec((None, LRU_BLOCK, LRU_BLOCK), lambda n, b, s: (n, 0, 0))
    return pl.pallas_call(
        functools.partial(_lru_kernel, rows=rows),
        grid=(LRU_NBLK, B, ns),
        in_specs=[blk, blk,
                  pl.BlockSpec((CONV_W, LRU_BLOCK), lambda n, b, s: (0, n)), vec,
                  wblk, vec, wblk, vec, vec],
        out_specs=blk,
        out_shape=jax.ShapeDtypeStruct((T, W), BF16),
        scratch_shapes=[pltpu.VMEM((8, LRU_BLOCK), F32), pltpu.VMEM((8, LRU_BLOCK), F32)],
        compiler_params=_cparams(("parallel", "parallel", "arbitrary")),
        name="lru_core",
    )(y, gate, conv_w, conv_b.reshape(1, W), w_a, b_a.reshape(1, W), w_i, b_i.reshape(1, W),
      lam.reshape(1, W))


def _rglru_layer(x32, xb, p, ln_g, ln_b, *, B, S):
    gate = _proj_nn(xb, p["w_gate"], out_dtype=BF16, act="gelu")
    y = _proj_nn(xb, p["w_y"], out_dtype=F32)
    hg = _lru_core(y, gate, p["conv_w"], p["conv_b"], p["w_a"], p["b_a"], p["w_i"], p["b_i"],
                   p["lam"], B=B, S=S)
    return _proj_ln(hg, p["w_out"], x32, ln_g, ln_b, *p["router"])


def _router_kernel(lg_ref, ids_ref, wts_ref, pos_ref, cnt_ref, carry_ref, *, tm):
    @pl.when(pl.program_id(0) == 0)
    def _():
        carry_ref[...] = jnp.zeros_like(carry_ref)

    logits = lg_ref[...].T[:N_EXPERTS]
    m = jnp.max(logits, axis=0, keepdims=True)
    e = jnp.exp(logits - m)
    p = e / jnp.sum(e, axis=0, keepdims=True)
    rows = [p[k:k + 1, :] for k in range(N_EXPERTS)]

    scores = []
    for g in range(N_GROUPS):
        a, b, c, d = rows[4 * g:4 * g + 4]
        hi1, lo1, hi2, lo2 = jnp.maximum(a, b), jnp.minimum(a, b), jnp.maximum(c, d), jnp.minimum(c, d)
        scores.append(jnp.maximum(hi1, hi2) + jnp.maximum(jnp.minimum(hi1, hi2), jnp.maximum(lo1, lo2)))
    best, gsel = scores[0], jnp.zeros((1, tm), jnp.int32)
    for g in range(1, N_GROUPS):
        upd = scores[g] > best
        gsel = jnp.where(upd, g, gsel)
        best = jnp.where(upd, scores[g], best)

    cand = []
    for k in range(EXPERTS_PER_GROUP):
        c = rows[k]
        for g in range(1, N_GROUPS):
            c = jnp.where(gsel == g, rows[4 * g + k], c)
        cand.append(c)
    v1, i1 = cand[0], jnp.zeros((1, tm), jnp.int32)
    for k in range(1, EXPERTS_PER_GROUP):
        upd = cand[k] > v1
        i1 = jnp.where(upd, k, i1)
        v1 = jnp.where(upd, cand[k], v1)
    v2, i2 = jnp.full((1, tm), -1.0, F32), jnp.zeros((1, tm), jnp.int32)
    for k in range(EXPERTS_PER_GROUP):
        c = jnp.where(i1 == k, -1.0, cand[k])
        upd = c > v2
        i2 = jnp.where(upd, k, i2)
        v2 = jnp.where(upd, c, v2)
    e0 = gsel * EXPERTS_PER_GROUP + i1
    e1 = gsel * EXPERTS_PER_GROUP + i2
    tot = v1 + v2

    eio = lax.broadcasted_iota(jnp.int32, (N_EXPERTS, tm), 0)
    hit0, hit1 = eio == e0, eio == e1
    onehot = jnp.where(hit0 | hit1, 1.0, 0.0)
    before = (lax.broadcasted_iota(jnp.int32, (tm, tm), 0)
              < lax.broadcasted_iota(jnp.int32, (tm, tm), 1))
    rank = jnp.dot(onehot.astype(BF16), jnp.where(before, 1.0, 0.0).astype(BF16),
                   preferred_element_type=F32) + carry_ref[:, 0:1]
    pos0 = jnp.sum(jnp.where(hit0, rank, 0.0), axis=0, keepdims=True)
    pos1 = jnp.sum(jnp.where(hit1, rank, 0.0), axis=0, keepdims=True)
    carry_ref[...] = carry_ref[...] + jnp.sum(onehot, axis=1, keepdims=True)
    cnt_ref[...] = carry_ref[...]

    zi = jnp.zeros((6, tm), jnp.int32)
    ids_ref[...] = jnp.concatenate([e0, e1, zi], axis=0)
    pos_ref[...] = jnp.concatenate([pos0.astype(jnp.int32), pos1.astype(jnp.int32), zi], axis=0)
    wts_ref[...] = jnp.concatenate([v1 / tot, v2 / tot, jnp.zeros((6, tm), F32)], axis=0)


def _router(logits, *, tm=512):
    T = logits.shape[0]
    tm = min(tm, T)
    tok = pl.BlockSpec((8, tm), lambda i: (0, i))
    return pl.pallas_call(
        functools.partial(_router_kernel, tm=tm),
        grid=(T // tm,),
        in_specs=[pl.BlockSpec((tm, LANE), lambda i: (i, 0))],
        out_specs=[tok, tok, tok, pl.BlockSpec((N_EXPERTS, LANE), lambda i: (0, 0))],
        out_shape=[jax.ShapeDtypeStruct((8, T), jnp.int32), jax.ShapeDtypeStruct((8, T), F32),
                   jax.ShapeDtypeStruct((8, T), jnp.int32),
                   jax.ShapeDtypeStruct((N_EXPERTS, LANE), F32)],
        scratch_shapes=[pltpu.VMEM((N_EXPERTS, LANE), F32)],
        compiler_params=_cparams(("arbitrary",)),
        name="moe_router",
    )(logits)


def _router_params(router_w, router_b):
    pad = LANE - N_EXPERTS
    w = jnp.pad(router_w, ((0, 0), (0, pad)))
    w_hi = w.astype(BF16)
    w_lo = (w - w_hi.astype(F32)).astype(BF16)
    return (jnp.stack([w_hi, w_lo]), jnp.pad(router_b, (0, pad)).reshape(1, LANE))


def _row_copy(src, src_row, dst, dst_row, sem, n=1):
    return pltpu.make_async_copy(src.at[pl.ds(src_row, n)], dst.at[pl.ds(dst_row, n)], sem)


def _row_tokens_kernel(n_ref, lo_ref, hi_ref, d0_ref, d1_ref, tok_ref):
    step = 8

    def clear(r, c):
        tok_ref[r] = 0
        return c

    def put(i, c):
        for k in range(step):
            t = i * step + k
            tok_ref[d0_ref[t]] = t
            tok_ref[d1_ref[t]] = t
        return c

    for e in range(lo_ref.shape[0]):
        lax.fori_loop(lo_ref[e], hi_ref[e], clear, 0)
    lax.fori_loop(0, n_ref[0] // step, put, 0)


def _row_tokens(d0, d1, free_lo, free_hi, n_rows):
    n_tok = d0.shape[0]
    assert n_tok % 8 == 0
    return pl.pallas_call(
        _row_tokens_kernel,
        grid_spec=pltpu.PrefetchScalarGridSpec(
            num_scalar_prefetch=5, grid=(1,), in_specs=[],
            out_specs=pl.BlockSpec(memory_space=pltpu.SMEM)),
        out_shape=jax.ShapeDtypeStruct((n_rows,), jnp.int32),
        compiler_params=pltpu.CompilerParams(dimension_semantics=("arbitrary",)),
        name="moe_row_tokens",
    )(jnp.array([n_tok], jnp.int32), free_lo, free_hi, d0, d1)


def _ffn_kernel(be_ref, nact_ref, tok_ref, x_hbm, wg_ref, wu_ref, wd_ref, y_ref, xbuf, sems):
    i = pl.program_id(0)
    n_act = nact_ref[0]
    R = EXPERT_ROWS

    def gather(block, slot):
        def issue(r, c):
            _row_copy(x_hbm, tok_ref[block * R + r], xbuf.at[slot], r, sems.at[slot]).start()
            return c
        lax.fori_loop(0, R, issue, 0, unroll=8)

    def run(slot):
        @pl.when(i + 1 < n_act)
        def _():
            gather(i + 1, 1 - slot)

        def drain(r, c):
            _row_copy(x_hbm, 0, xbuf.at[slot], 0, sems.at[slot]).wait()
            return c
        lax.fori_loop(0, R, drain, 0, unroll=8)
        xb = xbuf[slot].astype(BF16)
        g = jnp.dot(xb, wg_ref[...], preferred_element_type=F32)
        u = jnp.dot(xb, wu_ref[...], preferred_element_type=F32)
        h = (jax.nn.silu(g) * u).astype(BF16)
        y_ref[...] = jnp.dot(h, wd_ref[...], preferred_element_type=F32)

    @pl.when(i == 0)
    def _():
        gather(0, 0)

    for slot in range(2):
        @pl.when((i < n_act) & (i % 2 == slot))
        def _():
            run(slot)

    @pl.when(i >= n_act)
    def _():
        y_ref[...] = jnp.zeros_like(y_ref)


def _expert_ffn(x32, row_tok, blk_e, n_act, w_gate, w_up, w_down, layer):
    D = x32.shape[1]
    R = EXPERT_ROWS
    n_rows = row_tok.shape[0]
    wmap = lambda i, be, na, tok: (layer, be[i], 0, 0)
    return pl.pallas_call(
        _ffn_kernel,
        grid_spec=pltpu.PrefetchScalarGridSpec(
            num_scalar_prefetch=3, grid=(n_rows // R,),
            in_specs=[pl.BlockSpec(memory_space=pl.ANY),
                      pl.BlockSpec((None, None, D, D_EXPERT), wmap),
                      pl.BlockSpec((None, None, D, D_EXPERT), wmap),
                      pl.BlockSpec((None, None, D_EXPERT, D), wmap)],
            out_specs=pl.BlockSpec((R, D), lambda i, be, na, tok: (i, 0)),
            scratch_shapes=[pltpu.VMEM((2, R, D), F32), pltpu.SemaphoreType.DMA((2,))]),
        out_shape=jax.ShapeDtypeStruct((n_rows, D), F32),
        compiler_params=_cparams(("arbitrary",)),
        name="moe_ffn",
    )(blk_e, n_act, row_tok, x32, w_gate, w_up, w_down)


def _combine_kernel(d0_ref, d1_ref, res_ref, w_ref, g_ref, b_ref, y_hbm, of_ref, ob_ref,
                    buf, sem, *, tm):
    i = pl.program_id(0)

    def issue(r, c):
        t = i * tm + r
        _row_copy(y_hbm, d0_ref[t], buf.at[0], r, sem).start()
        _row_copy(y_hbm, d1_ref[t], buf.at[1], r, sem).start()
        return c

    def drain(r, c):
        _row_copy(y_hbm, 0, buf.at[0], 0, sem).wait()
        _row_copy(y_hbm, 0, buf.at[1], 0, sem).wait()
        return c

    lax.fori_loop(0, tm, issue, 0)
    lax.fori_loop(0, tm, drain, 0)
    w = w_ref[...]
    z = ALPHA * res_ref[...] + (w[:, 0:1] * buf[0] + w[:, 1:2] * buf[1])
    y = _layer_norm_rows(z, g_ref[...], b_ref[...])
    of_ref[...] = y
    ob_ref[...] = y.astype(BF16)


def _combine_ln(y, d0, d1, wts, res, g, b, *, tm=256):
    T, D = res.shape
    tm = min(tm, T)
    row = pl.BlockSpec((tm, D), lambda i, a, c: (i, 0))
    vec = pl.BlockSpec((1, D), lambda i, a, c: (0, 0))
    return pl.pallas_call(
        functools.partial(_combine_kernel, tm=tm),
        grid_spec=pltpu.PrefetchScalarGridSpec(
            num_scalar_prefetch=2, grid=(T // tm,),
            in_specs=[row, pl.BlockSpec((tm, 2), lambda i, a, c: (i, 0)), vec, vec,
                      pl.BlockSpec(memory_space=pl.ANY)],
            out_specs=[row, row],
            scratch_shapes=[pltpu.VMEM((2, tm, D), F32), pltpu.SemaphoreType.DMA(())]),
        out_shape=[jax.ShapeDtypeStruct((T, D), F32), jax.ShapeDtypeStruct((T, D), BF16)],
        compiler_params=_cparams(("arbitrary",)),
        name="moe_combine_ln",
    )(d0, d1, res, wts, g.reshape(1, D), b.reshape(1, D), y)


def _moe_layer(x32, logits, w_gate, w_up, w_down, layer, ln_g, ln_b):
    T = x32.shape[0]
    R = EXPERT_ROWS
    ids, wts, pos, cnt = _router(logits)
    counts = cnt[:, 0].astype(jnp.int32)
    padded = (counts + R - 1) // R * R
    p_end = jnp.cumsum(padded)
    offs = p_end - padded
    d0 = offs[ids[0]] + pos[0]
    d1 = offs[ids[1]] + pos[1]
    n_blocks = 2 * T // R + N_EXPERTS
    n_act = p_end[-1] // R
    step = jnp.arange(n_blocks, dtype=jnp.int32)
    live = step < n_act
    src = jnp.where(live, step, n_act - 1)
    blk_e = jnp.minimum(jnp.sum(src[:, None] >= (p_end // R)[None, :], axis=1),
                        N_EXPERTS - 1).astype(jnp.int32)
    n_rows = n_blocks * R
    free_lo = jnp.concatenate([offs + counts, p_end[-1:]]).astype(jnp.int32)
    free_hi = jnp.concatenate([p_end, jnp.array([n_rows], p_end.dtype)]).astype(jnp.int32)
    row_tok = _row_tokens(d0, d1, free_lo, free_hi, n_rows)
    y = _expert_ffn(x32, row_tok, blk_e, n_act.reshape(1).astype(jnp.int32), w_gate, w_up, w_down, layer)
    return _combine_ln(y, d0, d1, wts[:2].T, x32, ln_g, ln_b)


def _t5_bucket(dist):
    n = jnp.maximum(dist, 0)
    max_exact = NUM_BUCKETS // 2
    nf = jnp.maximum(n, 1).astype(F32)
    large = max_exact + (jnp.log(nf / max_exact) / math.log(MAX_DISTANCE / max_exact)
                         * (NUM_BUCKETS - max_exact)).astype(jnp.int32)
    large = jnp.minimum(large, NUM_BUCKETS - 1)
    return jnp.where(n < max_exact, n, large)


def _stack_heads(tbl):
    parts = tbl.reshape((N_KV, GROUP) + tbl.shape[1:])
    return jnp.concatenate([parts[:, g] for g in range(GROUP)], axis=-1)


def _bias_tables(rel_bias, S):
    off = CMP_STRIDE * CMP_TILE + Q_BLOCK
    n_dist = CMP_NEAR * Q_BLOCK
    bvec = rel_bias[_t5_bucket(jnp.arange(n_dist))].T
    last = rel_bias[NUM_BUCKETS - 1]
    masked = jnp.full((N_HEADS, off), NEG, F32)
    w_abs = jnp.concatenate([masked, bvec * LOG2E], axis=1)
    w_rel = jnp.concatenate([masked, (bvec - last[:, None]) * LOG2E], axis=1)
    w_win = jnp.where(jnp.arange(-off, n_dist) < WINDOW, w_abs, NEG)

    def toeplitz(w, first, step, n):
        lo = first - step * (n - 1) + off
        span = step * (n - 1) + Q_BLOCK
        u = w[:, lo:lo + span]
        width = span + step
        tiled = jnp.tile(u, (1, -(-n * width // span)))[:, :n * width]
        return tiled.reshape(N_HEADS, n, width)[:, ::-1, :Q_BLOCK]

    def hankel(u, n, m, step):
        span = step * (n - 1) + m
        width = span + step
        u = u[..., :span, :]
        tiled = jnp.tile(u, (1,) * (u.ndim - 2) + (-(-n * width // span), 1))[..., :n * width, :]
        return tiled.reshape(u.shape[:-2] + (n, width, u.shape[-1]))[..., :m, :]

    per = Q_BLOCK // CMP_STRIDE
    n_seq = per * (CMP_NEAR - 1) + (per - 1) + CMP_TILE
    lo = off - (CMP_LEN - 1) - CMP_STRIDE * (CMP_TILE - 1)
    seq = w_abs[:, lo:lo + CMP_STRIDE * n_seq].reshape(N_HEADS, n_seq, CMP_STRIDE)
    by_tile = hankel(seq, CMP_NEAR, per - 1 + CMP_TILE, per)
    near_c = hankel(by_tile, per, CMP_TILE, 1)
    near_c = near_c[:, :, :, ::-1, :].transpose(0, 1, 3, 2, 4).reshape(N_HEADS, CMP_NEAR, CMP_TILE, Q_BLOCK)
    const = lambda v: jnp.broadcast_to(v, (N_HEADS, 1, CMP_TILE, Q_BLOCK)).astype(F32)
    tc = jnp.concatenate([const(NEG), near_c, const(last[:, None, None, None] * LOG2E)], axis=1)
    zero = jnp.zeros((N_HEADS, 1, SEL_CHUNK, Q_BLOCK), F32)
    ts = jnp.stack([
        jnp.concatenate(
            [zero] + [toeplitz(w_rel, Q_BLOCK * sub + SEL_CHUNK * (SEL_NEAR - 1 - i), 1, SEL_CHUNK)[:, None]
                      for i in range(SEL_NEAR)] + [zero + NEG], axis=1)
        for sub in range(SEL_CHUNK // Q_BLOCK)], axis=1)
    n_keys = WINDOW + WIN_PAIR * Q_BLOCK
    tw = jnp.concatenate([_stack_heads(toeplitz(w_win, WINDOW + Q_BLOCK * c, 1, n_keys))
                          for c in range(WIN_PAIR)], axis=-1)
    ts = _stack_heads(ts)
    ts = jnp.concatenate([ts[:, sub] for sub in range(SEL_PAIR)], axis=-1)
    return _stack_heads(tc), ts, tw


def _compress_kernel(x_ref, pe_ref, w1_ref, w2_ref, o_ref, *, transposed):
    n = x_ref.shape[0] // CMP_STRIDE
    a = b = None
    for tok in range(CMP_STRIDE):
        xt = x_ref[pl.ds(tok, n, stride=CMP_STRIDE), :]
        as_first = (xt + pe_ref[tok:tok + 1, :]).astype(BF16)
        as_second = (xt + pe_ref[CMP_STRIDE + tok:CMP_STRIDE + tok + 1, :]).astype(BF16)
        da = jnp.dot(as_first, w1_ref[tok * HEAD_DIM:(tok + 1) * HEAD_DIM], preferred_element_type=F32)
        db = jnp.dot(as_second, w1_ref[(CMP_STRIDE + tok) * HEAD_DIM:(CMP_STRIDE + tok + 1) * HEAD_DIM],
                     preferred_element_type=F32)
        a = da if a is None else a + da
        b = db if b is None else b + db
    pre = a + pltpu.roll(b, n - 1, 0)
    rowi = lax.broadcasted_iota(jnp.int32, pre.shape, 0)
    hid = jnp.where(rowi < n - 1, jax.nn.gelu(pre), 0.0).astype(BF16)
    if transposed:
        out = lax.dot_general(w2_ref[...], hid, (((1,), (1,)), ((), ())), preferred_element_type=F32)
    else:
        out = jnp.dot(hid, w2_ref[...], preferred_element_type=F32)
    o_ref[...] = out.astype(BF16)


def _compress(slabs, first, pe, w1, w2, *, B, S, transposed):
    n = S // CMP_STRIDE
    wide = CMP_STRIDE * HEAD_DIM
    out_shape = (N_KV, B, HEAD_DIM, n) if transposed else (N_KV, B, n, HEAD_DIM)
    w2_in = w2.T if transposed else w2
    return pl.pallas_call(
        functools.partial(_compress_kernel, transposed=transposed),
        grid=(N_KV, B),
        in_specs=[pl.BlockSpec((None, None, S, HEAD_DIM), lambda h, b: (first + h, b, 0, 0)),
                  pl.BlockSpec((CMP_LEN, HEAD_DIM), lambda h, b: (0, 0)),
                  pl.BlockSpec((2 * wide, HEAD_DIM), lambda h, b: (0, 0)),
                  pl.BlockSpec((HEAD_DIM, HEAD_DIM), lambda h, b: (0, 0))],
        out_specs=pl.BlockSpec((None, None) + out_shape[2:], lambda h, b: (h, b, 0, 0)),
        out_shape=jax.ShapeDtypeStruct(out_shape, BF16),
        compiler_params=_cparams(("parallel", "parallel")),
        name="nsa_compress",
    )(slabs, pe, w1, w2_in)


def _stacked_queries(q_ref, n_blocks):
    return jnp.concatenate([q_ref[g * HEAD_DIM:(g + 1) * HEAD_DIM, c * Q_BLOCK:(c + 1) * Q_BLOCK]
                            for c in range(n_blocks) for g in range(GROUP)], axis=1)


def _cmp_body(ib, n_tiles, q_ref, kc_ref, vct_ref, tc_ref, ov_ref, oc_ref, mb_ref):
    n_cmp = n_tiles * CMP_TILE
    n_sel = n_cmp * CMP_STRIDE // SEL_LEN
    nq = CMP_PAIR * Q_BLOCK
    qt = _stacked_queries(q_ref, CMP_PAIR)
    s = jnp.dot(kc_ref[:n_cmp], qt, preferred_element_type=F32)
    tiles = []
    for cb in range(n_tiles):
        bias = [tc_ref[jnp.clip(ib + c - (CMP_TILE * CMP_STRIDE // Q_BLOCK) * cb + 1, 0, CMP_NEAR + 1)]
                for c in range(CMP_PAIR)]
        tiles.append(s[cb * CMP_TILE:(cb + 1) * CMP_TILE] + jnp.concatenate(bias, axis=1))
    s = jnp.concatenate(tiles, axis=0)
    m = jnp.max(s, axis=0, keepdims=True)
    p = jnp.exp2(s - m)
    inv = jnp.where(m > 0.5 * NEG, 1.0 / jnp.maximum(jnp.sum(p, axis=0, keepdims=True), 1e-30), 0.0)
    p = p * inv
    oc = jnp.dot(vct_ref[:, :n_cmp], p.astype(BF16), preferred_element_type=F32)
    for c in range(CMP_PAIR):
        oc_ref[c] = oc[:, c * QG:(c + 1) * QG]
    ov = ov_ref[:n_sel, :n_cmp]

    sums = []
    for c in range(CMP_PAIR):
        acc = p[:, c * QG:c * QG + Q_BLOCK]
        for g in range(1, GROUP):
            acc = acc + p[:, c * QG + g * Q_BLOCK:c * QG + (g + 1) * Q_BLOCK]
        sums.append(acc)
    psum = jnp.concatenate(sums, axis=1)
    p_hi = psum.astype(BF16)
    p_lo = (psum - p_hi.astype(F32)).astype(BF16)
    imp = (jnp.dot(ov, p_hi, preferred_element_type=F32)
           + jnp.dot(ov, p_lo, preferred_element_type=F32))
    blk = lax.broadcasted_iota(jnp.int32, (n_sel, nq), 0)
    t = ib * Q_BLOCK + lax.broadcasted_iota(jnp.int32, (n_sel, nq), 1)
    cur = jnp.right_shift(t, int(math.log2(SEL_LEN)))
    ok = blk * SEL_LEN <= t
    forced = ok & ((blk == 0) | (blk == cur) | (blk == cur - 1))
    score = jnp.where(forced, -jnp.inf, jnp.where(ok, imp, NEG))
    taken = forced
    for _ in range(SEL_TOP - 3):
        top = jnp.max(score, axis=0, keepdims=True)
        first = jnp.min(jnp.where(score == top, blk, n_sel), axis=0, keepdims=True)
        pick = blk == first
        taken = taken | pick
        score = jnp.where(pick, -jnp.inf, score)
    mb_ref[:n_sel] = jnp.where(taken & ok, 0.0, NEG).astype(BF16)
    if n_sel < mb_ref.shape[0]:
        mb_ref[n_sel:] = jnp.full((mb_ref.shape[0] - n_sel, nq), NEG, BF16)


def _cmp_kernel(*refs, n_tiles):
    ib = CMP_PAIR * pl.program_id(2)
    per_tile = CMP_TILE * CMP_STRIDE // Q_BLOCK
    lo = 0
    for nt in range(1, n_tiles + 1):
        hi = nt * per_tile

        @pl.when((ib >= lo) & (ib < hi))
        def _():
            _cmp_body(ib, nt, *refs)
        lo = hi


def _cmp_attention(qt, kc, vct, tc, overlap_t, *, B, S):
    nq = S // Q_BLOCK
    npair = nq // CMP_PAIR
    n_cmp = S // CMP_STRIDE
    n_sel = S // SEL_LEN
    n_tiles = n_cmp // CMP_TILE
    return pl.pallas_call(
        functools.partial(_cmp_kernel, n_tiles=n_tiles),
        grid=(N_KV, B, npair),
        in_specs=[pl.BlockSpec((GROUP * HEAD_DIM, CMP_PAIR * Q_BLOCK), lambda h, b, i: (h, b * npair + i)),
                  pl.BlockSpec((None, None, n_cmp, HEAD_DIM), lambda h, b, i: (h, b, 0, 0)),
                  pl.BlockSpec((None, None, HEAD_DIM, n_cmp), lambda h, b, i: (h, b, 0, 0)),
                  pl.BlockSpec((None, CMP_NEAR + 2, CMP_TILE, QG), lambda h, b, i: (h, 0, 0, 0)),
                  pl.BlockSpec((n_sel, n_cmp), lambda h, b, i: (0, 0))],
        out_specs=[pl.BlockSpec((None, CMP_PAIR, HEAD_DIM, QG), lambda h, b, i: (h, b * npair + i, 0, 0)),
                   pl.BlockSpec((None, None, n_sel, CMP_PAIR * Q_BLOCK), lambda h, b, i: (h, b, 0, i))],
        out_shape=[jax.ShapeDtypeStruct((N_KV, B * nq, HEAD_DIM, QG), F32),
                   jax.ShapeDtypeStruct((N_KV, B, n_sel, S), BF16)],
        compiler_params=_cparams(("parallel", "parallel", "parallel")),
        name="nsa_cmp_select",
    )(qt, kc, vct, tc, overlap_t)


def _sel_kernel(q_ref, mb_ref, k_ref, oh_ref, vt_ref, ts_ref, o_ref, m_ref, acc_ref, sa_ref, sb_ref):
    cq = pl.program_id(2)
    qt = _stacked_queries(q_ref, SEL_PAIR)
    blocks_per_chunk = SEL_CHUNK // SEL_LEN
    chunks_per_group = LANE // blocks_per_chunk

    m_ref[...] = jnp.full(m_ref.shape, NEG, F32)
    acc_ref[...] = jnp.zeros_like(acc_ref)

    qq = cq // SEL_FAR

    def raw_scores(quad, n=SEL_FAR):
        rows = n * SEL_CHUNK
        c0 = pl.multiple_of(quad * SEL_FAR, SEL_FAR)
        grp = c0 // chunks_per_group
        mb = mb_ref[pl.ds(pl.multiple_of(grp * LANE, LANE), LANE), :]
        mask = jnp.concatenate([mb[:, c * Q_BLOCK:(c + 1) * Q_BLOCK]
                                for c in range(SEL_PAIR) for _ in range(GROUP)], axis=1)
        qa = jnp.concatenate([qt, mask], axis=0)
        o0 = pl.multiple_of(c0 % oh_ref.shape[0], SEL_FAR)
        keys = jnp.concatenate([k_ref[pl.ds(c0, n)].reshape(rows, HEAD_DIM),
                                oh_ref[pl.ds(o0, n)].reshape(rows, HEAD_DIM)], axis=1)
        return jnp.dot(keys, qa, preferred_element_type=F32)

    def update(s, quad, n=SEL_FAR):
        c0 = quad * SEL_FAR
        v = jnp.concatenate([vt_ref[c0 + k] for k in range(n)], axis=1)
        m_old = m_ref[0:1, :]
        m_new = jnp.maximum(m_old, jnp.max(s, axis=0, keepdims=True))
        p = jnp.exp2((s - m_new).astype(BF16))
        acc_ref[...] = (jnp.exp2(m_old - m_new) * acc_ref[...]
                        + jnp.dot(v, p, preferred_element_type=F32))
        m_ref[...] = jnp.broadcast_to(m_new, m_ref.shape)

    def bias(quad, live, n=SEL_FAR):
        tiles = []
        for k in range(n):
            near = jnp.clip(quad * SEL_FAR + k - cq + SEL_NEAR, 0, SEL_NEAR + 1)
            tiles.append(ts_ref[jnp.where(live, near, SEL_NEAR + 1)])
        return jnp.concatenate(tiles, axis=0)

    def finish_n(near0_ref, n):
        s1 = raw_scores(qq, n)
        update(near0_ref[...] + bias(qq - 1, qq >= 1), jnp.maximum(qq - 1, 0))
        update(s1 + bias(qq, True, n), qq, n)
        out = acc_ref[:HEAD_DIM] / jnp.maximum(acc_ref[HEAD_DIM:HEAD_DIM + 1], 1e-30)
        for c in range(SEL_PAIR):
            o_ref[c] = out[:, c * QG:(c + 1) * QG]

    def finish(near0_ref):
        for n in range(1, SEL_FAR + 1):
            @pl.when(cq % SEL_FAR == n - 1)
            def _():
                finish_n(near0_ref, n)

    n_far = jnp.maximum(qq - 1, 0)
    sa_ref[...] = raw_scores(0)

    def far_pair(i, carry):
        sb_ref[...] = raw_scores(2 * i + 1)
        update(sa_ref[...], 2 * i)
        sa_ref[...] = raw_scores(2 * i + 2)
        update(sb_ref[...], 2 * i + 1)
        return carry

    lax.fori_loop(0, n_far // 2, far_pair, 0)

    @pl.when(n_far % 2 == 1)
    def _():
        sb_ref[...] = raw_scores(n_far)
        update(sa_ref[...], n_far - 1)
        finish(sb_ref)

    @pl.when(n_far % 2 == 0)
    def _():
        finish(sa_ref)


def _sel_attention(qt, mbt, k_sel, onehot, vt, ts, *, B, S):
    nq = S // Q_BLOCK
    n_sel = mbt.shape[2]
    nc = S // SEL_CHUNK
    width = SEL_PAIR * QG
    once = pl.Buffered(1)
    return pl.pallas_call(
        _sel_kernel,
        grid=(N_KV, B, nc),
        in_specs=[pl.BlockSpec((GROUP * HEAD_DIM, SEL_CHUNK), lambda h, b, i: (h, b * nc + i)),
                  pl.BlockSpec((None, None, n_sel, SEL_CHUNK), lambda h, b, i: (h, b, 0, i)),
                  pl.BlockSpec((None, None, nc, SEL_CHUNK, HEAD_DIM), lambda h, b, i: (h, b, 0, 0, 0)),
                  pl.BlockSpec(onehot.shape, lambda h, b, i: (0, 0, 0), pipeline_mode=once),
                  pl.BlockSpec((None, None, nc, V_ROWS, SEL_CHUNK), lambda h, b, i: (h, b, 0, 0, 0)),
                  pl.BlockSpec((None, SEL_NEAR + 2, SEL_CHUNK, width), lambda h, b, i: (h, 0, 0, 0),
                               pipeline_mode=once)],
        out_specs=pl.BlockSpec((None, SEL_PAIR, HEAD_DIM, QG), lambda h, b, i: (h, b * nc + i, 0, 0)),
        out_shape=jax.ShapeDtypeStruct((N_KV, B * nq, HEAD_DIM, QG), F32),
        scratch_shapes=[pltpu.VMEM((8, width), F32), pltpu.VMEM((V_ROWS, width), F32),
                        pltpu.VMEM((SEL_FAR * SEL_CHUNK, width), F32),
                        pltpu.VMEM((SEL_FAR * SEL_CHUNK, width), F32)],
        compiler_params=_cparams(("parallel", "parallel", "arbitrary")),
        name="nsa_selected",
    )(qt, mbt, k_sel, onehot, vt, ts)


def _win_kernel(*refs):
    nk = WINDOW // Q_BLOCK + WIN_PAIR
    q_ref = refs[0]
    k_refs = refs[1:1 + nk]
    v_refs = refs[1 + nk:1 + 2 * nk]
    tw_ref, gate_ref, oc_ref, os_ref, o_ref = refs[1 + 2 * nk:]
    first = WIN_PAIR * pl.program_id(2) - WINDOW // Q_BLOCK
    qt = _stacked_queries(q_ref, WIN_PAIR)
    parts = []
    for i in range(nk):
        s = jnp.dot(k_refs[i][...], qt, preferred_element_type=F32)
        s = s + tw_ref[i * Q_BLOCK:(i + 1) * Q_BLOCK, :]
        parts.append(jnp.where(first + i >= 0, s, NEG))
    s = jnp.concatenate(parts, axis=0)
    m = jnp.max(s, axis=0, keepdims=True)
    pb = jnp.exp2((s - m).astype(BF16))
    ow = jnp.dot(v_refs[0][...], pb[0:Q_BLOCK], preferred_element_type=F32)
    for i in range(1, nk):
        ow = ow + jnp.dot(v_refs[i][...], pb[i * Q_BLOCK:(i + 1) * Q_BLOCK], preferred_element_type=F32)
    ow = ow[:HEAD_DIM] / jnp.maximum(ow[HEAD_DIM:HEAD_DIM + 1], 1e-30)

    def gate(j):
        return jnp.concatenate([gate_ref[j, g:g + 1, c * Q_BLOCK:(c + 1) * Q_BLOCK]
                                for c in range(WIN_PAIR) for g in range(GROUP)], axis=1)

    def both(ref):
        return jnp.concatenate([ref[c] for c in range(WIN_PAIR)], axis=1)

    o = gate(0) * both(oc_ref) + gate(1) * both(os_ref) + gate(2) * ow
    for c in range(WIN_PAIR):
        for g in range(GROUP):
            col = c * QG + g * Q_BLOCK
            o_ref[c * Q_BLOCK:(c + 1) * Q_BLOCK, g * HEAD_DIM:(g + 1) * HEAD_DIM] = (
                o[:, col:col + Q_BLOCK].T.astype(BF16))


def _win_attention(qt, kw, vwt, tw, gates_t, oc, osel, *, B, S):
    nq = S // Q_BLOCK
    npair = nq // WIN_PAIR
    back = WINDOW // Q_BLOCK
    nk = back + WIN_PAIR

    def kmap(j):
        return lambda h, b, i: (N_KV + h, b * nq + jnp.maximum(WIN_PAIR * i - back + j, 0), 0)

    def vmap_(j):
        return lambda h, b, i: (h, 0, b * nq + jnp.maximum(WIN_PAIR * i - back + j, 0))

    blk = pl.BlockSpec((None, WIN_PAIR, HEAD_DIM, QG), lambda h, b, i: (h, b * npair + i, 0, 0))
    return pl.pallas_call(
        _win_kernel,
        grid=(N_KV, B, npair),
        in_specs=([pl.BlockSpec((GROUP * HEAD_DIM, WIN_PAIR * Q_BLOCK), lambda h, b, i: (h, b * npair + i))]
                  + [pl.BlockSpec((None, Q_BLOCK, HEAD_DIM), kmap(j)) for j in range(nk)]
                  + [pl.BlockSpec((None, V_ROWS, Q_BLOCK), vmap_(j)) for j in range(nk)]
                  + [pl.BlockSpec((None, nk * Q_BLOCK, WIN_PAIR * QG), lambda h, b, i: (h, 0, 0)),
                     pl.BlockSpec((3, None, GROUP, WIN_PAIR * Q_BLOCK),
                                  lambda h, b, i: (0, h, 0, b * npair + i)),
                     blk, blk]),
        out_specs=pl.BlockSpec((WIN_PAIR * Q_BLOCK, GROUP * HEAD_DIM), lambda h, b, i: (b * npair + i, h)),
        out_shape=jax.ShapeDtypeStruct((B * S, N_HEADS * HEAD_DIM), BF16),
        compiler_params=_cparams(("parallel", "parallel", "parallel")),
        name="nsa_window_merge",
    )(qt, *([kw] * nk), *([vwt] * nk), tw, gates_t, oc, osel)


def _nsa_layer(x32, xb, p, tables, consts, ln_g, ln_b, *, B, S):
    T = B * S
    tc, ts, tw = tables
    overlap_t, sel_onehot = consts
    kd = N_KV * HEAD_DIM
    qv = _proj_nt(p["w_qv_t"], xb, out_dtype=BF16, scale=HEAD_DIM ** -0.5 * LOG2E,
                  n_scaled=N_HEADS * HEAD_DIM // 512)
    qt = qv
    k2 = _proj_nn(xb, p["w_k"], out_dtype=BF16, head_major=True)
    cmp = _proj_nn(xb, p["w_cmp"], out_dtype=F32, head_major=True)
    gates_t = _proj_nt(p["w_gate_t"], xb, out_dtype=F32, act="sigmoid")
    gates_t = gates_t.reshape(3, N_KV, GROUP, T)

    cmp = cmp.reshape(2 * N_KV, B, S, HEAD_DIM)
    kc = _compress(cmp, 0, p["pe_k"], p["w1_k"], p["w2_k"], B=B, S=S, transposed=False)
    vct = _compress(cmp, N_KV, p["pe_v"], p["w1_v"], p["w2_v"], B=B, S=S, transposed=True)
    oc, mbt = _cmp_attention(qt, kc, vct, tc, overlap_t, B=B, S=S)
    if mbt.shape[2] % LANE:
        mbt = jnp.pad(mbt, ((0, 0), (0, 0), (0, -mbt.shape[2] % LANE), (0, 0)), constant_values=NEG)

    nc = S // SEL_CHUNK
    k_sel = k2.reshape(2 * N_KV, B, nc, SEL_CHUNK, HEAD_DIM)
    v_sel_t = qv[N_HEADS * HEAD_DIM:N_HEADS * HEAD_DIM + kd].reshape(N_KV, HEAD_DIM, B, nc, SEL_CHUNK)
    v_sel_t = v_sel_t.transpose(0, 2, 3, 1, 4)
    ones_rows = (jnp.arange(V_ROWS - HEAD_DIM) == 0).astype(BF16)[:, None]
    v_sel_t = jnp.concatenate(
        [v_sel_t, jnp.broadcast_to(ones_rows, v_sel_t.shape[:3] + (V_ROWS - HEAD_DIM, SEL_CHUNK))], axis=3)
    osel = _sel_attention(qt, mbt, k_sel, sel_onehot, v_sel_t, ts, B=B, S=S)

    v_win_t = qv[N_HEADS * HEAD_DIM + kd:].reshape(N_KV, HEAD_DIM, T)
    v_win_t = jnp.concatenate(
        [v_win_t, jnp.broadcast_to(ones_rows, (N_KV, V_ROWS - HEAD_DIM, T))], axis=1)
    attn = _win_attention(qt, k2, v_win_t, tw, gates_t, oc, osel, B=B, S=S)
    return _proj_ln(attn, p["w_out"], x32, ln_g, ln_b, *p["router"])


def _nsa_params(w_in, pe_k, w1_k, w2_k, pe_v, w1_v, w2_v, w_out):
    D = w_in.shape[0]
    qd, kd = N_HEADS * HEAD_DIM, N_KV * HEAD_DIM
    w_kv = w_in[:, qd:qd + 6 * kd].reshape(D, 6, kd)
    gcol = np.array([[[(kv * GROUP + g) * 3 + j for g in range(GROUP)] for kv in range(N_KV)]
                     for j in range(3)]).reshape(-1)
    return dict(
        w_qv_t=jnp.concatenate([w_in[:, :qd], w_kv[:, 3], w_kv[:, 5]], axis=1).T.astype(BF16),
        w_k=jnp.concatenate([w_kv[:, 2], w_kv[:, 4]], axis=1).astype(BF16),
        w_cmp=jnp.concatenate([w_kv[:, 0], w_kv[:, 1]], axis=1).astype(BF16),
        w_gate_t=w_in[:, qd + 6 * kd:][:, gcol].T.astype(BF16),
        pe_k=pe_k, w1_k=w1_k.astype(BF16), w2_k=w2_k.astype(BF16),
        pe_v=pe_v, w1_v=w1_v.astype(BF16), w2_v=w2_v.astype(BF16),
        w_out=w_out.astype(BF16))


def _nsa_constants(S):
    n_cmp, n_sel = S // CMP_STRIDE, S // SEL_LEN
    cj = np.arange(n_cmp) * CMP_STRIDE
    sb = np.arange(n_sel) * SEL_LEN
    ov = np.maximum(np.minimum(cj[None, :] + CMP_LEN, sb[:, None] + SEL_LEN)
                    - np.maximum(cj[None, :], sb[:, None]), 0).astype(np.float32) / CMP_LEN
    ov[:, n_cmp - 1] = 0.0
    key = np.arange(SEL_CHUNK)
    chunk = np.arange(min(S // SEL_CHUNK, LANE * SEL_LEN // SEL_CHUNK))
    blk = (chunk[:, None] * (SEL_CHUNK // SEL_LEN) + key[None, :] // SEL_LEN) % LANE
    onehot = (blk[:, :, None] == np.arange(LANE)[None, None, :]).astype(np.float32)
    return jnp.asarray(ov, BF16), jnp.asarray(onehot, BF16)


def kernel(x, rel_bias, router_w, router_b, ln1_g, ln1_b, ln2_g, ln2_b, lru_w_in, lru_conv_w,
           lru_conv_b, lru_w_a, lru_b_a, lru_w_i, lru_b_i, lru_lambda, lru_w_out, nsa_w_in,
           nsa_pe_k, nsa_w1_k, nsa_w2_k, nsa_pe_v, nsa_w1_v, nsa_w2_v, nsa_w_out,
           moe_w_gate, moe_w_up, moe_w_down):
    B, S, D = x.shape
    T = B * S
    assert D == D_MODEL and S % (CMP_TILE * CMP_STRIDE) == 0 and S // SEL_LEN >= SEL_TOP
    tables = _bias_tables(rel_bias, S)
    consts = _nsa_constants(S)
    router = _router_params(router_w, router_b)
    moe_bf16 = [w.astype(BF16) for w in (moe_w_gate, moe_w_up, moe_w_down)]

    x32 = x.reshape(T, D)
    xb = x32.astype(BF16)
    for layer in range(DEPTH):
        j = layer // 2
        if layer % 2 == 0:
            w_in = lru_w_in[j].astype(BF16)
            p = dict(w_gate=w_in[:, :D], w_y=w_in[:, D:], conv_w=lru_conv_w[j], conv_b=lru_conv_b[j],
                     w_a=lru_w_a[j].astype(BF16), b_a=lru_b_a[j], w_i=lru_w_i[j].astype(BF16),
                     b_i=lru_b_i[j], lam=lru_lambda[j], w_out=lru_w_out[j].astype(BF16), router=router)
            x32, xb, logits = _rglru_layer(x32, xb, p, ln1_g[layer], ln1_b[layer], B=B, S=S)
        else:
            p = _nsa_params(nsa_w_in[j], nsa_pe_k[j], nsa_w1_k[j], nsa_w2_k[j], nsa_pe_v[j],
                            nsa_w1_v[j], nsa_w2_v[j], nsa_w_out[j])
            p["router"] = router
            x32, xb, logits = _nsa_layer(x32, xb, p, tables, consts, ln1_g[layer], ln1_b[layer], B=B, S=S)
        x32, xb = _moe_layer(x32, logits, moe_bf16[0], moe_bf16[1], moe_bf16[2], layer,
                             ln2_g[layer], ln2_b[layer])
    return x32.reshape(B, S, D)
```
